```python
import math
import jax
import jax.numpy as jnp
from jax import lax
import numpy as np

D_MODEL = 1024
BATCH = 4
SEQ = 4096
DEPTH = 4

CTX_LEN = 256
GRID_W = 64

A_HEADS = 4
A_HEAD_DIM = 64
A_WIDTH = A_HEADS * 2 * A_HEAD_DIM

G_HEADS = 4
G_KEY_DIM = 128
G_VAL_DIM = 128
G_QK_WIDTH = G_HEADS * G_KEY_DIM
G_V_WIDTH = G_HEADS * G_VAL_DIM
CONV_K = 5
CHUNK = 64

N_EXPERTS = 16
N_GROUPS = 4
EXPERTS_PER_GROUP = N_EXPERTS // N_GROUPS
TOP_K = 2
GROUP_SCORE_TOPK = 2
D_EXPERT = 512
MOE_BLOCK = 256

Q_BLOCK = 128
ROPE_BASE = 10000.0
EPS = 1e-6

IN_SIZES = (A_WIDTH, A_WIDTH, A_WIDTH, 2 * G_QK_WIDTH + G_V_WIDTH, G_V_WIDTH, 2 * G_HEADS, 2 * G_HEADS, 2 * D_MODEL)
IN_OFFSETS = tuple(sum(IN_SIZES[:i + 1]) for i in range(len(IN_SIZES) - 1))
N_IN = sum(IN_SIZES)

kernel_name = 'hybrid_diffattn_gdn_moe_dit'


def layer_norm(x, g, b):
    xf = x.astype(jnp.float32)
    mu = jnp.mean(xf, axis=-1, keepdims=True)
    var = jnp.mean(jnp.square(xf - mu), axis=-1, keepdims=True)
    return ((xf - mu) * lax.rsqrt(var + EPS)).astype(x.dtype) * g + b


def rms_norm(x, g):
    xf = x.astype(jnp.float32)
    return (xf * lax.rsqrt(jnp.mean(jnp.square(xf), axis=-1, keepdims=True) + EPS)).astype(x.dtype) * g


def l2_normalize(x):
    xf = x.astype(jnp.float32)
    return (xf * lax.rsqrt(jnp.sum(jnp.square(xf), axis=-1, keepdims=True) + EPS)).astype(x.dtype)


def modulate(x, shift, scale):
    return x * (1.0 + scale) + shift


def split_in(p):
    return jnp.split(p, IN_OFFSETS, axis=-1)


def axial_rope(n_tokens, dim):
    rows = n_tokens // GRID_W
    row = jnp.repeat(jnp.arange(rows, dtype=jnp.float32), GRID_W)
    col = jnp.tile(jnp.arange(GRID_W, dtype=jnp.float32), rows)
    n_freq = dim // 4
    inv = ROPE_BASE ** (-jnp.arange(n_freq, dtype=jnp.float32) / n_freq)
    ang = jnp.stack([row[:, None] * inv, col[:, None] * inv], axis=1)
    return jnp.cos(ang), jnp.sin(ang)


def rope_2d(x, cos, sin):
    xs = x.reshape(x.shape[:-1] + (2, 2, -1))
    c = cos[:, None, None, :, :]
    s = sin[:, None, None, :, :]
    x1 = xs[..., 0, :]
    x2 = xs[..., 1, :]
    out = jnp.stack([x1 * c - x2 * s, x2 * c + x1 * s], axis=-2)
    return out.reshape(x.shape)


def diff_attend(q, k, v, lam):
    s = jnp.einsum('bqhnd,bkhnd->bnhqk', q, k).astype(jnp.float32) * (A_HEAD_DIM ** -0.5)
    p = jax.nn.softmax(s, axis=-1)
    p = p[:, 0] - lam * p[:, 1]
    return jnp.einsum('bhqk,bkhe->bqhe', p.astype(v.dtype), v)


def short_conv(x, w):
    return lax.conv_general_dilated(x, w[:, None, :], window_strides=(1,),
                                    padding=[(CONV_K // 2, CONV_K // 2)],
                                    dimension_numbers=('NWC', 'WIO', 'NWC'),
                                    feature_group_count=x.shape[-1])


def gated_delta_rule(q, k, v, g, beta, state):
    B, L, H, DK = q.shape
    DV = v.shape[-1]
    n = L // CHUNK
    f32 = jnp.float32

    def chunk4(t):
        return t.astype(f32).reshape(B, n, CHUNK, H, t.shape[-1]).transpose(1, 0, 3, 2, 4)

    def chunk3(t):
        return t.astype(f32).reshape(B, n, CHUNK, H).transpose(1, 0, 3, 2)

    qc = chunk4(q) * (DK ** -0.5)
    kc = chunk4(k)
    vc = chunk4(v)
    bc = chunk3(beta)
    gcum = jnp.cumsum(chunk3(g), axis=-1)
    idx = jnp.arange(CHUNK)
    causal = idx[:, None] >= idx[None, :]
    strict = idx[:, None] > idx[None, :]
    decay = jnp.exp(jnp.where(causal, gcum[..., :, None] - gcum[..., None, :], -jnp.inf))
    kb = kc * bc[..., None]
    a_mat = jnp.where(strict, jnp.einsum('nbhid,nbhjd->nbhij', kb, kc) * decay, 0.0)
    rhs = jnp.concatenate([vc * bc[..., None], kb * jnp.exp(gcum)[..., None]], axis=-1)
    sol = lax.linalg.triangular_solve(a_mat + jnp.eye(CHUNK, dtype=f32), rhs, left_side=True,
                                      lower=True, unit_diagonal=True)
    u, w = sol[..., :DV], sol[..., DV:]
    qk = jnp.where(causal, jnp.einsum('nbhid,nbhjd->nbhij', qc, kc) * decay, 0.0)
    q_dec = qc * jnp.exp(gcum)[..., None]
    k_dec = kc * jnp.exp(gcum[..., -1:] - gcum)[..., None]
    g_last = jnp.exp(gcum[..., -1])

    def step(s, xs):
        q_d, k_d, u_n, w_n, qk_n, gl = xs
        v_new = u_n - jnp.einsum('bhcd,bhde->bhce', w_n, s)
        o = jnp.einsum('bhcd,bhde->bhce', q_d, s) + jnp.einsum('bhij,bhje->bhie', qk_n, v_new)
        s = s * gl[..., None, None] + jnp.einsum('bhcd,bhce->bhde', k_d, v_new)
        return s, o

    s_final, o = lax.scan(step, state.astype(f32), (q_dec, k_dec, u, w, qk, g_last))
    o = o.transpose(1, 0, 3, 2, 4).reshape(B, L, H, DV)
    return o, s_final


def token_mixer(h_lat, h_ctx, cos, sin, lam_init, ctx_out, w_in, conv_w, lam_q1, lam_k1, lam_q2, lam_k2,
                subln_g, a_log, dt_bias, onorm_g, w_pa, w_pb, w_o):
    B, S, _ = h_lat.shape
    qa_l, ka_l, va_l, qkv_l, z_l, a_l, b_l, gate_l = split_in(h_lat @ w_in)
    qa_c, ka_c, va_c, qkv_c, z_c, a_c, b_c, gate_c = split_in(h_ctx @ w_in)

    def heads_a(t):
        return t.reshape(t.shape[:2] + (A_HEADS, 2, A_HEAD_DIM))

    def heads_v(t):
        return t.reshape(t.shape[:2] + (A_HEADS, 2 * A_HEAD_DIM))

    def subln(o):
        return (rms_norm(o, subln_g) * (1.0 - lam_init)).reshape(o.shape[:2] + (A_WIDTH,))

    lam = (jnp.exp(jnp.sum(lam_q1 * lam_k1).astype(jnp.float32))
           - jnp.exp(jnp.sum(lam_q2 * lam_k2).astype(jnp.float32)) + lam_init)
    qa_l = rope_2d(heads_a(qa_l), cos, sin)
    k_ctx = heads_a(ka_c)
    v_ctx = heads_v(va_c)
    k_all = jnp.concatenate([k_ctx, rope_2d(heads_a(ka_l), cos, sin)], axis=1)
    v_all = jnp.concatenate([v_ctx, heads_v(va_l)], axis=1)
    q_blocks = qa_l.reshape((B, S // Q_BLOCK, Q_BLOCK) + qa_l.shape[2:]).swapaxes(0, 1)
    oa_l = lax.map(lambda qb: diff_attend(qb, k_all, v_all, lam), q_blocks)
    ya_l = subln(oa_l.swapaxes(0, 1).reshape(B, S, A_HEADS, 2 * A_HEAD_DIM))

    def gdn_branch(qkv, z, a, b, states):
        L = qkv.shape[1]
        qkv = jax.nn.silu(short_conv(qkv, conv_w))
        q, k, v = jnp.split(qkv, [G_QK_WIDTH, 2 * G_QK_WIDTH], axis=-1)
        q = l2_normalize(q.reshape(B, L, G_HEADS, G_KEY_DIM))
        k = l2_normalize(k.reshape(B, L, G_HEADS, G_KEY_DIM))
        v = v.reshape(B, L, G_HEADS, G_VAL_DIM)
        g = -jnp.exp(a_log.astype(jnp.float32)) * jax.nn.softplus(
            a.reshape(B, L, 2, G_HEADS).astype(jnp.float32) + dt_bias.astype(jnp.float32))
        beta = jax.nn.sigmoid(b.reshape(B, L, 2, G_HEADS).astype(jnp.float32))
        flip = lambda t: jnp.flip(t, axis=1)
        o_f, s_f = gated_delta_rule(q, k, v, g[:, :, 0], beta[:, :, 0], states[0])
        o_b, s_b = gated_delta_rule(flip(q), flip(k), flip(v), flip(g[:, :, 1]), flip(beta[:, :, 1]), states[1])
        o = (o_f + flip(o_b)).astype(z.dtype)
        y = rms_norm(o, onorm_g) * jax.nn.silu(z.reshape(B, L, G_HEADS, G_VAL_DIM))
        return y.reshape(B, L, G_V_WIDTH), (s_f, s_b)

    zeros = jnp.zeros((B, G_HEADS, G_KEY_DIM, G_VAL_DIM), jnp.float32)
    yb_c, ctx_states = gdn_branch(qkv_c, z_c, a_c, b_c, (zeros, zeros))
    yb_l, _ = gdn_branch(qkv_l, z_l, a_l, b_l, ctx_states)

    def merge(ya, yb, gates):
        g_a, g_b = jnp.split(jax.nn.sigmoid(gates), 2, axis=-1)
        return (g_a * (ya @ w_pa) + g_b * (yb @ w_pb)) @ w_o

    y_lat = merge(ya_l, yb_l, gate_l)
    if not ctx_out:
        return y_lat, None
    ya_c = subln(diff_attend(heads_a(qa_c), k_ctx, v_ctx, lam))
    y_ctx = merge(ya_c, yb_c, gate_c)
    return y_lat, y_ctx


def moe_ffn(h, router_w, router_b, w1, w3, w2):
    T, D = h.shape
    scores = jax.nn.sigmoid((h @ router_w).astype(jnp.float32))
    sel = (scores + router_b.astype(jnp.float32)).reshape(T, N_GROUPS, EXPERTS_PER_GROUP)
    group_score = jnp.sum(lax.top_k(sel, GROUP_SCORE_TOPK)[0], axis=-1)
    best = jnp.argmax(group_score, axis=-1).astype(jnp.int32)
    in_group = jnp.take_along_axis(sel, best[:, None, None], axis=1)[:, 0]
    _, local = lax.top_k(in_group, TOP_K)
    expert_idx = best[:, None] * EXPERTS_PER_GROUP + local
    w = jnp.take_along_axis(scores, expert_idx, axis=1)
    w = w / jnp.sum(w, axis=-1, keepdims=True)

    n_assign = T * TOP_K
    flat_e = expert_idx.reshape(-1)
    order = jnp.argsort(flat_e)
    sorted_e = flat_e[order]
    counts = jnp.bincount(flat_e, length=N_EXPERTS)
    padded = (counts + MOE_BLOCK - 1) // MOE_BLOCK * MOE_BLOCK
    pad_end = jnp.cumsum(padded)
    pad_start = pad_end - padded
    start = jnp.cumsum(counts) - counts
    dest_sorted = pad_start[sorted_e] + jnp.arange(n_assign) - start[sorted_e]
    dest = jnp.zeros_like(dest_sorted).at[order].set(dest_sorted)
    n_blocks = -(-n_assign // MOE_BLOCK) + N_EXPERTS
    tok = jnp.arange(n_assign, dtype=jnp.int32) // TOP_K
    slot_tok = jnp.full((n_blocks * MOE_BLOCK,), T, jnp.int32).at[dest].set(tok)
    h_pad = jnp.concatenate([h, jnp.zeros((1, D), h.dtype)], axis=0)
    xs = h_pad[slot_tok].reshape(n_blocks, MOE_BLOCK, D)
    block_e = jnp.minimum(jnp.searchsorted(pad_end, jnp.arange(n_blocks) * MOE_BLOCK, side='right'),
                          N_EXPERTS - 1)

    def expert_block(args):
        xb, e = args
        return (jax.nn.silu(xb @ w1[e]) * (xb @ w3[e])) @ w2[e]

    ys = lax.map(expert_block, (xs, block_e)).reshape(-1, D)
    y = ys[dest].reshape(T, TOP_K, D)
    return jnp.einsum('tk,tkd->td', w.astype(h.dtype), y)


def setup_inputs(seed: int = 0) -> dict:
    key = jax.random.key(seed)
    ks = jax.random.split(key, 32)
    f32 = jnp.float32
    beta_dn = (8.0 * DEPTH) ** -0.25

    def nrm(k, shape, scale):
        return jax.random.normal(k, shape, f32) * scale

    def gain(k, shape):
        return 1.0 + 0.02 * jax.random.normal(k, shape, f32)

    dt = jnp.exp(jax.random.uniform(ks[12], (DEPTH, 2, G_HEADS), f32, math.log(1e-3), math.log(1e-1)))
    return {
        'x': nrm(ks[0], (BATCH, SEQ, D_MODEL), 1.0),
        'c': nrm(ks[1], (BATCH, D_MODEL), 1.0),
        'ctx': nrm(ks[2], (BATCH, CTX_LEN, D_MODEL), 1.0),
        'c_ctx': nrm(ks[3], (D_MODEL,), 1.0),
        'mod_w': nrm(ks[4], (DEPTH, D_MODEL, 6 * D_MODEL), 0.5 * D_MODEL ** -0.5),
        'mod_b': nrm(ks[5], (DEPTH, 6 * D_MODEL), 0.02),
        'w_in': nrm(ks[6], (DEPTH, D_MODEL, N_IN), D_MODEL ** -0.5),
        'conv_w': nrm(ks[7], (DEPTH, CONV_K, 2 * G_QK_WIDTH + G_V_WIDTH), CONV_K ** -0.5),
        'lam_q1': nrm(ks[8], (DEPTH, A_HEAD_DIM), 0.1),
        'lam_k1': nrm(ks[9], (DEPTH, A_HEAD_DIM), 0.1),
        'lam_q2': nrm(ks[10], (DEPTH, A_HEAD_DIM), 0.1),
        'lam_k2': nrm(ks[11], (DEPTH, A_HEAD_DIM), 0.1),
        'subln_g': gain(ks[13], (DEPTH, 2 * A_HEAD_DIM)),
        'a_log': jnp.log(jax.random.uniform(ks[14], (DEPTH, 2, G_HEADS), f32, 1.0, 16.0)),
        'dt_bias': dt + jnp.log(-jnp.expm1(-dt)),
        'onorm_g': gain(ks[15], (DEPTH, G_VAL_DIM)),
        'w_pa': nrm(ks[16], (DEPTH, A_WIDTH, D_MODEL), beta_dn * A_WIDTH ** -0.5),
        'w_pb': nrm(ks[17], (DEPTH, G_V_WIDTH, D_MODEL), beta_dn * G_V_WIDTH ** -0.5),
        'w_o': nrm(ks[18], (DEPTH, D_MODEL, D_MODEL), beta_dn * D_MODEL ** -0.5),
        'ln1_g': gain(ks[19], (DEPTH, D_MODEL)),
        'ln1_b': nrm(ks[20], (DEPTH, D_MODEL), 0.02),
        'router_w': nrm(ks[21], (D_MODEL, N_EXPERTS), D_MODEL ** -0.5),
        'router_b': nrm(ks[22], (N_EXPERTS,), 0.01),
        'w_exp1': nrm(ks[23], (DEPTH, N_EXPERTS, D_MODEL, D_EXPERT), D_MODEL ** -0.5),
        'w_exp3': nrm(ks[24], (DEPTH, N_EXPERTS, D_MODEL, D_EXPERT), D_MODEL ** -0.5),
        'w_exp2': nrm(ks[25], (DEPTH, N_EXPERTS, D_EXPERT, D_MODEL), beta_dn * D_EXPERT ** -0.5),
        'ln2_g': gain(ks[26], (DEPTH, D_MODEL)),
        'ln2_b': nrm(ks[27], (DEPTH, D_MODEL), 0.02),
    }


def reference(x, c, ctx, c_ctx, mod_w, mod_b, w_in, conv_w, lam_q1, lam_k1, lam_q2, lam_k2, subln_g,
              a_log, dt_bias, onorm_g, w_pa, w_pb, w_o, ln1_g, ln1_b, router_w, router_b,
              w_exp1, w_exp3, w_exp2, ln2_g, ln2_b):
    B, S, D = x.shape
    n_ctx = ctx.shape[1]
    alpha = (2.0 * DEPTH) ** 0.25
    cos, sin = axial_rope(S, A_HEAD_DIM)
    cos, sin = cos.astype(x.dtype), sin.astype(x.dtype)
    xc = ctx
    for l in range(DEPTH):
        last = l == DEPTH - 1
        lam_init = 0.8 - 0.6 * math.exp(-0.3 * l)
        m_lat = jnp.split(jax.nn.silu(c) @ mod_w[l] + mod_b[l], 6, axis=-1)
        m_ctx = jnp.split(jax.nn.silu(c_ctx) @ mod_w[l] + mod_b[l], 6, axis=-1)

        h_lat = modulate(x, m_lat[0][:, None], m_lat[1][:, None])
        h_ctx = modulate(xc, m_ctx[0], m_ctx[1])
        y_lat, y_ctx = token_mixer(h_lat, h_ctx, cos, sin, lam_init, not last, w_in[l], conv_w[l],
                                   lam_q1[l], lam_k1[l], lam_q2[l], lam_k2[l], subln_g[l], a_log[l],
                                   dt_bias[l], onorm_g[l], w_pa[l], w_pb[l], w_o[l])
        x = layer_norm(alpha * x + m_lat[2][:, None] * y_lat, ln1_g[l], ln1_b[l])

        h_lat = modulate(x, m_lat[3][:, None], m_lat[4][:, None])
        if last:
            y_lat = moe_ffn(h_lat.reshape(-1, D), router_w, router_b,
                            w_exp1[l], w_exp3[l], w_exp2[l]).reshape(B, S, D)
        else:
            xc = layer_norm(alpha * xc + m_ctx[2] * y_ctx, ln1_g[l], ln1_b[l])
            h_ctx = modulate(xc, m_ctx[3], m_ctx[4])
            tokens = jnp.concatenate([h_ctx.reshape(-1, D), h_lat.reshape(-1, D)], axis=0)
            y = moe_ffn(tokens, router_w, router_b, w_exp1[l], w_exp3[l], w_exp2[l])
            y_ctx = y[:B * n_ctx].reshape(B, n_ctx, D)
            y_lat = y[B * n_ctx:].reshape(B, S, D)
            xc = layer_norm(alpha * xc + m_ctx[5] * y_ctx, ln2_g[l], ln2_b[l])
        x = layer_norm(alpha * x + m_lat[5][:, None] * y_lat, ln2_g[l], ln2_b[l])
    return x
```

```python
import functools
import math

import jax
import jax.numpy as jnp
from jax import lax
from jax.experimental import pallas as pl
from jax.experimental.pallas import tpu as pltpu

F32 = jnp.float32
BF16 = jnp.bfloat16

GRID_W = 64
A_HEADS = 4
A_HEAD_DIM = 64
A_WIDTH = A_HEADS * 2 * A_HEAD_DIM
G_HEADS = 4
G_KEY_DIM = 128
G_VAL_DIM = 128
G_QK_WIDTH = G_HEADS * G_KEY_DIM
G_V_WIDTH = G_HEADS * G_VAL_DIM
G_QKV_WIDTH = 2 * G_QK_WIDTH + G_V_WIDTH
CONV_K = 5
CHUNK = 64
N_EXPERTS = 16
N_GROUPS = 4
EXPERTS_PER_GROUP = N_EXPERTS // N_GROUPS
MOE_BLOCK = 256
ROPE_BASE = 10000.0
EPS = 1e-6

LANES = 128
SUBLANES = 8
TM = 256
VMEM_LIMIT = 56 * 1024 * 1024

OFF_QK = 0
OFF_VA = 2 * A_WIDTH
OFF_G = OFF_VA + A_WIDTH
OFF_Z = OFF_G + G_QKV_WIDTH
OFF_AB = OFF_Z + G_V_WIDTH
OFF_GATE = OFF_AB + LANES
GC_CUM, GC_BETA, GC_EG, GC_EK, GC_ET = 0, 8, 16, 24, 32


def _sigmoid(x):
    return 1.0 / (1.0 + jnp.exp(-x))


def _silu(x):
    return x * _sigmoid(x)


def _dot(a, b):
    return jnp.dot(a, b, preferred_element_type=F32)


def _dot_nt(a, b):
    return lax.dot_general(a, b, (((1,), (1,)), ((), ())), preferred_element_type=F32)


def _dot_f32(a, b):
    return jnp.dot(a, b, preferred_element_type=F32, precision=lax.Precision.HIGHEST)


def _params(*sem):
    return pltpu.CompilerParams(dimension_semantics=sem, vmem_limit_bytes=VMEM_LIMIT)


def _mod_kernel(cc_ref, w_ref, b_ref, o_ref):
    h = _silu(cc_ref[...])
    o_ref[...] = _dot(h.astype(BF16), w_ref[...].astype(BF16)) + b_ref[...]


def _mod_tables(cc, mod_w, mod_b):
    depth, d, n = mod_w.shape
    tn = n // 6
    return pl.pallas_call(
        _mod_kernel,
        grid=(depth, n // tn),
        in_specs=[pl.BlockSpec((SUBLANES, d), lambda l, j: (0, 0)),
                  pl.BlockSpec((None, d, tn), lambda l, j: (l, 0, j)),
                  pl.BlockSpec((None, 1, tn), lambda l, j: (l, 0, j))],
        out_specs=pl.BlockSpec((None, SUBLANES, tn), lambda l, j: (l, 0, j)),
        out_shape=jax.ShapeDtypeStruct((depth, SUBLANES, n), F32),
        compiler_params=_params("parallel", "parallel"),
    )(cc, mod_w, mod_b.reshape(depth, 1, n))


def _inproj_kernel(x_ref, mod_ref, w_ref, cs_ref, qk_ref, va_ref, g_ref, z_ref, ab_ref, gate_ref):
    m = mod_ref[...]
    h = (x_ref[...] * (1.0 + m[1:2]) + m[0:1]).astype(BF16)
    cs = cs_ref[...]
    cos, sin = cs[:, :LANES], cs[:, LANES:]
    qk = _dot(h, w_ref[:, OFF_QK:OFF_VA])
    for j in range(2 * A_HEADS):
        a = qk[:, j * LANES:(j + 1) * LANES]
        qk_ref[:, j * LANES:(j + 1) * LANES] = (a * cos + pltpu.roll(a, LANES // 2, 1) * sin).astype(BF16)
    va_ref[...] = _dot(h, w_ref[:, OFF_VA:OFF_G]).astype(BF16)
    g_ref[...] = _dot(h, w_ref[:, OFF_G:OFF_Z])
    z_ref[...] = _dot(h, w_ref[:, OFF_Z:OFF_AB])
    ab_ref[...] = _dot(h, w_ref[:, OFF_AB:OFF_GATE])
    gate_ref[...] = _sigmoid(_dot(h, w_ref[:, OFF_GATE:]))


def _inproj(x_all, modt, w_cat, cs_tab, n_lat_tiles, tiles_per_seq):
    t, d = x_all.shape
    n = w_cat.shape[1]
    d_gate = n - OFF_GATE

    def row(i):
        return (i, 0)

    def cs_map(i):
        return (jnp.where(i < n_lat_tiles, i % tiles_per_seq, tiles_per_seq), 0)

    out_shape = [jax.ShapeDtypeStruct((t, 2 * A_WIDTH), BF16),
                 jax.ShapeDtypeStruct((t, A_WIDTH), BF16),
                 jax.ShapeDtypeStruct((t, G_QKV_WIDTH), F32),
                 jax.ShapeDtypeStruct((t, G_V_WIDTH), F32),
                 jax.ShapeDtypeStruct((t, LANES), F32),
                 jax.ShapeDtypeStruct((t, d_gate), F32)]
    return pl.pallas_call(
        _inproj_kernel,
        grid=(t // TM,),
        in_specs=[pl.BlockSpec((TM, d), row),
                  pl.BlockSpec((None, 6, d), lambda i: (i, 0, 0)),
                  pl.BlockSpec((d, n), lambda i: (0, 0)),
                  pl.BlockSpec((TM, 2 * LANES), cs_map)],
        out_specs=[pl.BlockSpec((TM, s.shape[1]), row) for s in out_shape],
        out_shape=out_shape,
        compiler_params=_params("parallel"),
    )(x_all, modt, w_cat, cs_tab)


def _attn_kernel(*refs, lam_init, has_lat):
    if has_lat:
        lam_ref, q_ref, kc_ref, vc_ref, kl_ref, vl_ref, g_ref, o_ref = refs
    else:
        lam_ref, q_ref, kc_ref, vc_ref, g_ref, o_ref = refs
    lv = lam_ref[...]
    lam = (jnp.exp(jnp.sum(lv[0:1] * lv[1:2], axis=-1, keepdims=True))
           - jnp.exp(jnp.sum(lv[2:3] * lv[3:4], axis=-1, keepdims=True)) + lam_init)
    q = q_ref[...]
    lane = lax.broadcasted_iota(jnp.int32, (1, LANES), 1)
    comp1 = ((lane // (A_HEAD_DIM // 2)) % 2) == 1
    outs = []
    for n in range(2):
        qn = jnp.where(comp1 if n else jnp.logical_not(comp1), q, jnp.zeros_like(q))
        sc = _dot_nt(qn, kc_ref[...])
        m = jnp.max(sc, axis=-1, keepdims=True)
        if has_lat:
            sl = _dot_nt(qn, kl_ref[...])
            m = jnp.maximum(m, jnp.max(sl, axis=-1, keepdims=True))
        pc = jnp.exp(sc - m)
        den = jnp.sum(pc, axis=-1, keepdims=True)
        o = _dot(pc.astype(BF16), vc_ref[...])
        if has_lat:
            pl_ = jnp.exp(sl - m)
            den = den + jnp.sum(pl_, axis=-1, keepdims=True)
            o = o + _dot(pl_.astype(BF16), vl_ref[...])
        outs.append(o / den)
    o = outs[0] - lam * outs[1]
    r = o * lax.rsqrt(jnp.mean(o * o, axis=-1, keepdims=True) + EPS)
    o_ref[...] = (r * g_ref[...] * (1.0 - lam_init)).astype(BF16)


def _attention(qk, va, lamv, subln_g, ya, *, lam_init, b, s, nc, latent):
    tq = TM
    t = qk.shape[0]
    ctx0 = (b * s) // nc
    if latent:
        grid = (b, A_HEADS, s // tq)
        q_map = lambda bi, h, i: (bi * (s // tq) + i, h)
    else:
        grid = (b, A_HEADS, nc // tq)
        q_map = lambda bi, h, i: ((b * s) // tq + bi * (nc // tq) + i, h)
    in_specs = [pl.BlockSpec((SUBLANES, LANES), lambda bi, h, i: (0, 0)),
                pl.BlockSpec((tq, LANES), q_map),
                pl.BlockSpec((nc, LANES), lambda bi, h, i: (ctx0 + bi, A_HEADS + h)),
                pl.BlockSpec((nc, LANES), lambda bi, h, i: (ctx0 + bi, h))]
    args = [lamv, qk, qk, va]
    if latent:
        in_specs += [pl.BlockSpec((s, LANES), lambda bi, h, i: (bi, A_HEADS + h)),
                     pl.BlockSpec((s, LANES), lambda bi, h, i: (bi, h))]
        args += [qk, va]
    in_specs += [pl.BlockSpec((1, LANES), lambda bi, h, i: (0, 0))]
    args += [subln_g]
    aliases = {}
    kern = functools.partial(_attn_kernel, lam_init=lam_init, has_lat=latent)
    if not latent:
        in_specs += [pl.BlockSpec(memory_space=pl.ANY)]
        args += [ya]
        aliases = {len(args) - 1: 0}
        kern = functools.partial(_attn_kernel_aliased, lam_init=lam_init)
    return pl.pallas_call(
        kern,
        grid=grid,
        in_specs=in_specs,
        out_specs=pl.BlockSpec((tq, LANES), q_map),
        out_shape=jax.ShapeDtypeStruct((t, A_WIDTH), BF16),
        input_output_aliases=aliases,
        compiler_params=_params("parallel", "parallel", "arbitrary"),
    )(*args)


def _attn_kernel_aliased(*refs, lam_init):
    _attn_kernel(*refs[:-2], refs[-1], lam_init=lam_init, has_lat=False)


def _gdn_prep_kernel(g_ref, prev_ref, next_ref, ab_ref, cw_ref, gp_ref, q_ref, k_ref, v_ref, gc_ref, ext_ref,
                     *, n_lat_tiles, tiles_per_seq):
    i = pl.program_id(0)
    is_lat = i < n_lat_tiles
    first = jnp.logical_or(jnp.logical_not(is_lat), i % tiles_per_seq == 0)
    last = jnp.logical_or(jnp.logical_not(is_lat), i % tiles_per_seq == tiles_per_seq - 1)
    ext_ref[0:SUBLANES, :] = jnp.where(first, 0.0, prev_ref[...])
    ext_ref[SUBLANES:SUBLANES + TM, :] = g_ref[...]
    ext_ref[SUBLANES + TM:, :] = jnp.where(last, 0.0, next_ref[...])
    cw = cw_ref[...]
    acc = None
    for j in range(CONV_K):
        term = ext_ref[pl.ds(SUBLANES - CONV_K // 2 + j, TM), :] * cw[j:j + 1]
        acc = term if acc is None else acc + term
    y = _silu(acc)
    for h in range(G_HEADS):
        for base, ref, scale in ((0, q_ref, G_KEY_DIM ** -0.5), (G_QK_WIDTH, k_ref, 1.0)):
            xh = y[:, base + h * LANES:base + (h + 1) * LANES]
            nrm = xh * lax.rsqrt(jnp.sum(xh * xh, axis=-1, keepdims=True) + EPS)
            ref[:, h * LANES:(h + 1) * LANES] = nrm * scale
    v_ref[...] = y[:, 2 * G_QK_WIDTH:]

    ab = ab_ref[...]
    gp = gp_ref[...]
    lane = lax.broadcasted_iota(jnp.int32, (1, LANES), 1)
    z = ab + gp[1:2]
    softplus = jnp.maximum(z, 0.0) + jnp.log(1.0 + jnp.exp(-jnp.abs(z)))
    g = jnp.where(lane < 2 * G_HEADS, -jnp.exp(gp[0:1]) * softplus, 0.0)
    beta = _sigmoid(ab)
    ri = lax.broadcasted_iota(jnp.int32, (TM, TM), 0)
    ci = lax.broadcasted_iota(jnp.int32, (TM, TM), 1)
    same = (ri // CHUNK) == (ci // CHUNK)
    lower = jnp.where(jnp.logical_and(same, ci <= ri), 1.0, 0.0)
    upper = jnp.where(jnp.logical_and(same, ci >= ri), 1.0, 0.0)
    cum_f = _dot_f32(lower, g)
    cum_b = _dot_f32(upper, g)
    total = cum_f + cum_b - g
    cum = jnp.where(lane < G_HEADS, cum_f, cum_b)
    eg = jnp.exp(cum)
    ek = jnp.exp(total - cum)
    et = jnp.exp(total)
    out = jnp.where(lane < GC_BETA, cum, 0.0)
    out = jnp.where(jnp.logical_and(lane >= GC_BETA, lane < GC_EG), beta, out)
    out = jnp.where(jnp.logical_and(lane >= GC_EG, lane < GC_EK), pltpu.roll(eg, GC_EG, 1), out)
    out = jnp.where(jnp.logical_and(lane >= GC_EK, lane < GC_ET), pltpu.roll(ek, GC_EK, 1), out)
    out = jnp.where(jnp.logical_and(lane >= GC_ET, lane < GC_ET + 8), pltpu.roll(et, GC_ET, 1), out)
    gc_ref[...] = out


def _gdn_prep(g, ab, conv_w8, gparams, n_lat_tiles, tiles_per_seq):
    t, c = g.shape
    nblk8 = t // SUBLANES
    per = TM // SUBLANES
    row = lambda i: (i, 0)
    out_shape = [jax.ShapeDtypeStruct((t, G_QK_WIDTH), F32),
                 jax.ShapeDtypeStruct((t, G_QK_WIDTH), F32),
                 jax.ShapeDtypeStruct((t, G_V_WIDTH), F32),
                 jax.ShapeDtypeStruct((t, LANES), F32)]
    return pl.pallas_call(
        functools.partial(_gdn_prep_kernel, n_lat_tiles=n_lat_tiles, tiles_per_seq=tiles_per_seq),
        grid=(t // TM,),
        in_specs=[pl.BlockSpec((TM, c), row),
                  pl.BlockSpec((SUBLANES, c), lambda i: (jnp.maximum(i * per - 1, 0), 0)),
                  pl.BlockSpec((SUBLANES, c), lambda i: (jnp.minimum((i + 1) * per, nblk8 - 1), 0)),
                  pl.BlockSpec((TM, LANES), row),
                  pl.BlockSpec((SUBLANES, c), lambda i: (0, 0)),
                  pl.BlockSpec((SUBLANES, LANES), lambda i: (0, 0))],
        out_specs=[pl.BlockSpec((TM, s.shape[1]), row) for s in out_shape],
        out_shape=out_shape,
        scratch_shapes=[pltpu.VMEM((TM + 2 * SUBLANES, c), F32)],
        compiler_params=_params("parallel"),
    )(g, g, g, ab, conv_w8, gparams)


def _gdn_chunk_terms(q, k, v, gc, gct, col, reverse):
    c = q.shape[0]
    gcol = gc[:, GC_CUM + col:GC_CUM + col + 1]
    beta = gc[:, GC_BETA + col:GC_BETA + col + 1]
    eg = gc[:, GC_EG + col:GC_EG + col + 1]
    ek = gc[:, GC_EK + col:GC_EK + col + 1]
    et = gc[0:1, GC_ET + col:GC_ET + col + 1]
    grow = gct[GC_CUM + col:GC_CUM + col + 1, :]
    ii = lax.broadcasted_iota(jnp.int32, (c, c), 0)
    jj = lax.broadcasted_iota(jnp.int32, (c, c), 1)
    causal = (ii <= jj) if reverse else (ii >= jj)
    strict = (ii < jj) if reverse else (ii > jj)
    decay = jnp.exp(jnp.where(causal, gcol - grow, -1e30))
    kb = k * beta
    k16 = k.astype(BF16)
    a = jnp.where(strict, _dot_nt(kb.astype(BF16), k16) * decay, 0.0)
    x = jnp.where(ii == jj, 1.0, 0.0) - a
    p = _dot_f32(a, a)
    n_sq = int(math.log2(c)) - 1
    for t in range(n_sq):
        x = x + _dot_f32(x, p)
        if t < n_sq - 1:
            p = _dot_f32(p, p)
    rhs = jnp.concatenate([v * beta, kb * eg], axis=1)
    sol = _dot_f32(x, rhs)
    dv = v.shape[1]
    u, w = sol[:, :dv], sol[:, dv:]
    qk = jnp.where(causal, _dot_nt(q.astype(BF16), k16) * decay, 0.0)
    qd = q * eg
    kdt = (k * ek).T
    return u, w.astype(BF16), qk.astype(BF16), qd.astype(BF16), kdt.astype(BF16), et


def _gdn_chunk_step(s, terms):
    u, w, qk, qd, kdt, et = terms
    s16 = s.astype(BF16)
    v_new = (u - _dot(w, s16)).astype(BF16)
    o = _dot(qd, s16) + _dot(qk, v_new)
    s = s * et + _dot(kdt, v_new)
    return s, o


def _gdn_kernel(qf_ref, kf_ref, vf_ref, gf_ref, qb_ref, kb_ref, vb_ref, gb_ref, of_ref, ob_ref, sf_ref, sb_ref):
    h = pl.program_id(1)

    @pl.when(pl.program_id(2) == 0)
    def _():
        sf_ref[...] = jnp.zeros_like(sf_ref)
        sb_ref[...] = jnp.zeros_like(sb_ref)

    n_chunks = qf_ref.shape[0] // CHUNK
    s_f = sf_ref[...]
    s_b = sb_ref[...]
    for n in range(n_chunks):
        rf = pl.ds(n * CHUNK, CHUNK)
        rb = pl.ds((n_chunks - 1 - n) * CHUNK, CHUNK)
        gcf = _select_head(gf_ref[rf, :], h, 0)
        gcb = _select_head(gb_ref[rb, :], h, 1)
        tf = _gdn_chunk_terms(qf_ref[rf, :], kf_ref[rf, :], vf_ref[rf, :], gcf, gcf.T, 0, False)
        tb = _gdn_chunk_terms(qb_ref[rb, :], kb_ref[rb, :], vb_ref[rb, :], gcb, gcb.T, 0, True)
        s_f, o_f = _gdn_chunk_step(s_f, tf)
        s_b, o_b = _gdn_chunk_step(s_b, tb)
        of_ref[rf, :] = o_f
        ob_ref[rb, :] = o_b
    sf_ref[...] = s_f
    sb_ref[...] = s_b


def _select_head(gc, h, d):
    out = gc
    for hh in range(G_HEADS):
        shift = d * G_HEADS + hh
        if shift:
            out = jnp.where(h == hh, pltpu.roll(gc, LANES - shift, 1), out)
    return out


def _gdn(qg, kg, vg, gc, *, b, s, nc):
    t = qg.shape[0]
    tps = s // TM
    n_steps = nc // TM + tps
    nct = nc // TM
    ctx_tile0 = (b * s) // TM

    def fwd(bi, h, i):
        return jnp.where(i < nct, ctx_tile0 + bi * nct + i, bi * tps + (i - nct))

    def bwd(bi, h, i):
        return jnp.where(i < nct, ctx_tile0 + bi * nct + (nct - 1 - i), bi * tps + (tps - 1 - (i - nct)))

    def spec(row_map, per_head):
        if per_head:
            return pl.BlockSpec((TM, LANES), lambda bi, h, i: (row_map(bi, h, i), h))
        return pl.BlockSpec((TM, LANES), lambda bi, h, i: (row_map(bi, h, i), 0))

    in_specs = [spec(fwd, True)] * 3 + [spec(fwd, False)] + [spec(bwd, True)] * 3 + [spec(bwd, False)]
    out_shape = [jax.ShapeDtypeStruct((t, G_V_WIDTH), F32)] * 2
    return pl.pallas_call(
        _gdn_kernel,
        grid=(b, G_HEADS, n_steps),
        in_specs=in_specs,
        out_specs=[spec(fwd, True), spec(bwd, True)],
        out_shape=out_shape,
        scratch_shapes=[pltpu.VMEM((G_KEY_DIM, G_VAL_DIM), F32)] * 2,
        compiler_params=_params("parallel", "parallel", "arbitrary"),
    )(qg, kg, vg, gc, qg, kg, vg, gc)


def _layer_norm(r, g, b):
    mu = jnp.mean(r, axis=-1, keepdims=True)
    rc = r - mu
    var = jnp.mean(rc * rc, axis=-1, keepdims=True)
    return rc * lax.rsqrt(var + EPS) * g + b


def _merge_kernel(x_ref, ya_ref, of_ref, ob_ref, z_ref, gate_ref, mod_ref, wpa_ref, wpb_ref, wo_ref, on_ref,
                  lng_ref, lnb_ref, rw_ref, rb_ref, xo_ref, h2_ref, ei_ref, ew_ref, *, alpha):
    d = x_ref.shape[1]
    o = of_ref[...] + ob_ref[...]
    parts = []
    for h in range(G_HEADS):
        oh = o[:, h * LANES:(h + 1) * LANES]
        parts.append(oh * lax.rsqrt(jnp.mean(oh * oh, axis=-1, keepdims=True) + EPS) * on_ref[...])
    yb = jnp.concatenate(parts, axis=1) * _silu(z_ref[...])
    pa = _dot(ya_ref[...], wpa_ref[...])
    pb = _dot(yb.astype(BF16), wpb_ref[...])
    mix = gate_ref[:, :d] * pa + gate_ref[:, d:] * pb
    y = _dot(mix.astype(BF16), wo_ref[...])
    m = mod_ref[...]
    xn = _layer_norm(alpha * x_ref[...] + m[2:3] * y, lng_ref[...], lnb_ref[...])
    xo_ref[...] = xn
    h2 = (xn * (1.0 + m[4:5]) + m[3:4]).astype(BF16)
    h2_ref[...] = h2

    score = _sigmoid(_dot_nt(rw_ref[...], h2))
    sel = score + rb_ref[...]
    rows = [sel[e:e + 1] for e in range(N_EXPERTS)]
    best = None
    best_val = None
    for gi in range(N_GROUPS):
        a, b, c, dd = rows[gi * EXPERTS_PER_GROUP:(gi + 1) * EXPERTS_PER_GROUP]
        top2 = jnp.maximum(jnp.maximum(jnp.maximum(a + b, a + c), jnp.maximum(a + dd, b + c)),
                           jnp.maximum(b + dd, c + dd))
        if gi == 0:
            best, best_val = jnp.zeros(top2.shape, jnp.int32), top2
        else:
            upd = top2 > best_val
            best = jnp.where(upd, gi, best)
            best_val = jnp.where(upd, top2, best_val)
    in_group = []
    for j in range(EXPERTS_PER_GROUP):
        v = rows[(N_GROUPS - 1) * EXPERTS_PER_GROUP + j]
        for gi in range(N_GROUPS - 2, -1, -1):
            v = jnp.where(best == gi, rows[gi * EXPERTS_PER_GROUP + j], v)
        in_group.append(v)
    l0 = jnp.zeros(best.shape, jnp.int32)
    m0 = in_group[0]
    for j in range(1, EXPERTS_PER_GROUP):
        upd = in_group[j] > m0
        l0 = jnp.where(upd, j, l0)
        m0 = jnp.where(upd, in_group[j], m0)
    l1 = jnp.zeros(best.shape, jnp.int32)
    m1 = jnp.full(m0.shape, -jnp.inf, F32)
    for j in range(EXPERTS_PER_GROUP):
        upd = jnp.logical_and(l0 != j, in_group[j] > m1)
        l1 = jnp.where(upd, j, l1)
        m1 = jnp.where(upd, in_group[j], m1)
    e0 = best * EXPERTS_PER_GROUP + l0
    e1 = best * EXPERTS_PER_GROUP + l1
    w0 = jnp.zeros(m0.shape, F32)
    w1 = jnp.zeros(m0.shape, F32)
    for e in range(N_EXPERTS):
        w0 = jnp.where(e0 == e, score[e:e + 1], w0)
        w1 = jnp.where(e1 == e, score[e:e + 1], w1)
    tot = w0 + w1
    ri = lax.broadcasted_iota(jnp.int32, ei_ref.shape, 0)
    ei_ref[...] = jnp.where(ri == 0, e0, jnp.where(ri == 1, e1, 0))
    ew_ref[...] = jnp.where(ri == 0, w0 / tot, jnp.where(ri == 1, w1 / tot, 0.0))


def _merge(x_all, ya, o_f, o_b, z, gate, modt, wpa, wpb, wo, onorm, lng, lnb, rw_t, rb, *, alpha):
    t, d = x_all.shape
    row = lambda i: (i, 0)
    col = lambda i: (0, i)
    const = lambda i: (0, 0)
    out_shape = [jax.ShapeDtypeStruct((t, d), F32),
                 jax.ShapeDtypeStruct((t, d), BF16),
                 jax.ShapeDtypeStruct((SUBLANES, t), jnp.int32),
                 jax.ShapeDtypeStruct((SUBLANES, t), F32)]
    return pl.pallas_call(
        functools.partial(_merge_kernel, alpha=alpha),
        grid=(t // TM,),
        in_specs=[pl.BlockSpec((TM, d), row),
                  pl.BlockSpec((TM, A_WIDTH), row),
                  pl.BlockSpec((TM, G_V_WIDTH), row),
                  pl.BlockSpec((TM, G_V_WIDTH), row),
                  pl.BlockSpec((TM, G_V_WIDTH), row),
                  pl.BlockSpec((TM, 2 * d), row),
                  pl.BlockSpec((None, 6, d), lambda i: (i, 0, 0)),
                  pl.BlockSpec(wpa.shape, const),
                  pl.BlockSpec(wpb.shape, const),
                  pl.BlockSpec(wo.shape, const),
                  pl.BlockSpec((1, LANES), const),
                  pl.BlockSpec((1, d), const),
                  pl.BlockSpec((1, d), const),
                  pl.BlockSpec(rw_t.shape, const),
                  pl.BlockSpec(rb.shape, const)],
        out_specs=[pl.BlockSpec((TM, d), row), pl.BlockSpec((TM, d), row),
                   pl.BlockSpec((SUBLANES, TM), col), pl.BlockSpec((SUBLANES, TM), col)],
        out_shape=out_shape,
        compiler_params=_params("parallel"),
    )(x_all, ya, o_f, o_b, z, gate, modt, wpa, wpb, wo, onorm, lng, lnb, rw_t, rb)


def _expert_kernel(be_ref, nu_ref, x_ref, w1_ref, w3_ref, w2_ref, o_ref):
    i = pl.program_id(0)

    @pl.when(i < nu_ref[0])
    def _():
        x = x_ref[...]
        hid = _silu(_dot(x, w1_ref[...])) * _dot(x, w3_ref[...])
        o_ref[...] = _dot(hid.astype(BF16), w2_ref[...])

    @pl.when(i >= nu_ref[0])
    def _():
        o_ref[...] = jnp.zeros_like(o_ref)


def _experts(xs, block_e, n_used, w1, w3, w2):
    n_slots, d = xs.shape
    f = w1.shape[2]
    grid_spec = pltpu.PrefetchScalarGridSpec(
        num_scalar_prefetch=2,
        grid=(n_slots // MOE_BLOCK,),
        in_specs=[pl.BlockSpec((MOE_BLOCK, d), lambda i, be, nu: (i, 0)),
                  pl.BlockSpec((None, d, f), lambda i, be, nu: (be[i], 0, 0)),
                  pl.BlockSpec((None, d, f), lambda i, be, nu: (be[i], 0, 0)),
                  pl.BlockSpec((None, f, d), lambda i, be, nu: (be[i], 0, 0))],
        out_specs=pl.BlockSpec((MOE_BLOCK, d), lambda i, be, nu: (i, 0)),
    )
    return pl.pallas_call(
        _expert_kernel,
        grid_spec=grid_spec,
        out_shape=jax.ShapeDtypeStruct((n_slots, d), F32),
        compiler_params=_params("arbitrary"),
    )(block_e, n_used, xs, w1, w3, w2)


def _ln2_kernel(x_ref, y0_ref, y1_ref, ew_ref, mod_ref, g_ref, b_ref, o_ref, *, alpha):
    w = ew_ref[...]
    y = w[:, 0:1] * y0_ref[...] + w[:, 1:2] * y1_ref[...]
    m = mod_ref[...]
    o_ref[...] = _layer_norm(alpha * x_ref[...] + m[5:6] * y, g_ref[...], b_ref[...])


def _ln2(x_all, y0, y1, ew, modt, g, b, *, alpha):
    t, d = x_all.shape
    row = lambda i: (i, 0)
    const = lambda i: (0, 0)
    return pl.pallas_call(
        functools.partial(_ln2_kernel, alpha=alpha),
        grid=(t // TM,),
        in_specs=[pl.BlockSpec((TM, d), row), pl.BlockSpec((TM, d), row), pl.BlockSpec((TM, d), row),
                  pl.BlockSpec((TM, SUBLANES), row),
                  pl.BlockSpec((None, 6, d), lambda i: (i, 0, 0)),
                  pl.BlockSpec((1, d), const), pl.BlockSpec((1, d), const)],
        out_specs=pl.BlockSpec((TM, d), row),
        out_shape=jax.ShapeDtypeStruct((t, d), F32),
        compiler_params=_params("parallel"),
    )(x_all, y0, y1, ew, modt, g, b)


def _rope_table(s, nc):
    n_freq = A_HEAD_DIM // 4
    rows = s // GRID_W
    row = jnp.repeat(jnp.arange(rows, dtype=F32), GRID_W)
    col = jnp.tile(jnp.arange(GRID_W, dtype=F32), rows)
    inv = ROPE_BASE ** (-jnp.arange(n_freq, dtype=F32) / n_freq)
    ang = jnp.stack([row[:, None] * inv, col[:, None] * inv], axis=1)
    cos = jnp.cos(ang).reshape(s, 1, 1, 2, n_freq)
    sin = jnp.sin(ang).reshape(s, 1, 1, 2, n_freq)
    cos = jnp.broadcast_to(cos, (s, 2, 2, 2, n_freq)).reshape(s, LANES)
    sign = jnp.array([-1.0, 1.0], F32).reshape(1, 2, 1, 1, 1)
    sin = jnp.broadcast_to(sin * sign, (s, 2, 2, 2, n_freq)).reshape(s, LANES)
    lat = jnp.concatenate([cos, sin], axis=1)
    ident = jnp.concatenate([jnp.ones((nc, LANES), F32), jnp.zeros((nc, LANES), F32)], axis=1)
    return jnp.concatenate([lat, ident], axis=0)


def _rope_perm(w):
    d = w.shape[0]
    n_freq = A_HEAD_DIM // 4
    return w.reshape(d, A_HEADS, 2, 2, 2, n_freq).transpose(0, 1, 4, 2, 3, 5).reshape(d, A_WIDTH)


def _pack_w_in(w):
    d = w.shape[0]
    sizes = (A_WIDTH, A_WIDTH, A_WIDTH, G_QKV_WIDTH, G_V_WIDTH, 2 * G_HEADS, 2 * G_HEADS)
    offs = [0]
    for sz in sizes:
        offs.append(offs[-1] + sz)
    wq, wk, wv, wg, wz, wa, wb = (w[:, offs[i]:offs[i + 1]] for i in range(len(sizes)))
    wgate = w[:, offs[-1]:]
    wab = jnp.concatenate([wa, wb, jnp.zeros((d, LANES - 4 * G_HEADS), w.dtype)], axis=1)
    cat = jnp.concatenate([_rope_perm(wq) * (A_HEAD_DIM ** -0.5), _rope_perm(wk), wv, wg, wz, wab, wgate], axis=1)
    return cat.astype(BF16)


def _pad_rows(a, rows):
    return jnp.concatenate([a, jnp.zeros((rows - a.shape[0],) + a.shape[1:], a.dtype)], axis=0)


def _pad_lanes(a):
    return jnp.concatenate([a, jnp.zeros(a.shape[:-1] + (LANES - a.shape[-1],), a.dtype)], axis=-1)


def _dispatch(ei, t):
    n_assign = 2 * t
    flat_e = ei[:2].T.reshape(-1)
    onehot = (flat_e[:, None] == jnp.arange(N_EXPERTS, dtype=jnp.int32)[None, :]).astype(jnp.int32)
    csum = jnp.cumsum(onehot, axis=0)
    rank = jnp.sum(csum * onehot, axis=1) - 1
    counts = csum[-1]
    padded = (counts + MOE_BLOCK - 1) // MOE_BLOCK * MOE_BLOCK
    pad_end = jnp.cumsum(padded)
    pad_start = pad_end - padded
    dest = pad_start[flat_e] + rank
    n_blocks = -(-n_assign // MOE_BLOCK) + N_EXPERTS
    tok = jnp.arange(n_assign, dtype=jnp.int32) // 2
    slot_tok = jnp.full((n_blocks * MOE_BLOCK,), t, jnp.int32).at[dest].set(tok)
    block_e = jnp.minimum(jnp.searchsorted(pad_end, jnp.arange(n_blocks, dtype=jnp.int32) * MOE_BLOCK, side='right'),
                          N_EXPERTS - 1).astype(jnp.int32)
    n_used = (pad_end[-1:] // MOE_BLOCK).astype(jnp.int32)
    return dest.astype(jnp.int32), slot_tok, block_e, n_used


def kernel(x, c, ctx, c_ctx, mod_w, mod_b, w_in, conv_w, lam_q1, lam_k1, lam_q2, lam_k2, subln_g, a_log, dt_bias,
           onorm_g, w_pa, w_pb, w_o, ln1_g, ln1_b, router_w, router_b, w_exp1, w_exp3, w_exp2, ln2_g, ln2_b):
    b, s, d = x.shape
    nc = ctx.shape[1]
    depth = mod_w.shape[0]
    assert s % TM == 0 and nc % TM == 0 and s % nc == 0 and s % GRID_W == 0
    t_lat, t_ctx = b * s, b * nc
    t = t_lat + t_ctx
    n_lat_tiles = t_lat // TM
    tiles_per_seq = s // TM
    alpha = (2.0 * depth) ** 0.25

    x_all = jnp.concatenate([x.reshape(t_lat, d), ctx.reshape(t_ctx, d)], axis=0)
    cc = _pad_rows(jnp.concatenate([c, c_ctx[None, :]], axis=0), SUBLANES)
    mods = _mod_tables(cc, mod_w, mod_b)
    tile_row = jnp.concatenate([jnp.arange(n_lat_tiles, dtype=jnp.int32) // tiles_per_seq,
                                jnp.full((t_ctx // TM,), b, jnp.int32)])
    cs_tab = _rope_table(s, nc)
    rw_t = router_w.T.astype(BF16)
    rb = router_b.reshape(N_EXPERTS, 1)

    for l in range(depth):
        lam_init = 0.8 - 0.6 * math.exp(-0.3 * l)
        modt = mods[l].reshape(SUBLANES, 6, d)[tile_row]
        qk, va, g, z, ab, gate = _inproj(x_all, modt, _pack_w_in(w_in[l]), cs_tab, n_lat_tiles, tiles_per_seq)

        lamv = _pad_rows(_pad_lanes(jnp.stack([lam_q1[l], lam_k1[l], lam_q2[l], lam_k2[l]])), SUBLANES)
        sg = subln_g[l].reshape(1, LANES)
        ya = _attention(qk, va, lamv, sg, None, lam_init=lam_init, b=b, s=s, nc=nc, latent=True)
        ya = _attention(qk, va, lamv, sg, ya, lam_init=lam_init, b=b, s=s, nc=nc, latent=False)

        gparams = _pad_rows(_pad_lanes(jnp.stack([a_log[l].reshape(-1), dt_bias[l].reshape(-1)])), SUBLANES)
        qg, kg, vg, gc = _gdn_prep(g, ab, _pad_rows(conv_w[l], SUBLANES), gparams, n_lat_tiles, tiles_per_seq)
        o_f, o_b = _gdn(qg, kg, vg, gc, b=b, s=s, nc=nc)

        x_all, h2, ei, ew = _merge(x_all, ya, o_f, o_b, z, gate, modt, w_pa[l].astype(BF16), w_pb[l].astype(BF16),
                                   w_o[l].astype(BF16), onorm_g[l].reshape(1, LANES), ln1_g[l].reshape(1, d),
                                   ln1_b[l].reshape(1, d), rw_t, rb, alpha=alpha)

        dest, slot_tok, block_e, n_used = _dispatch(ei, t)
        xs = _pad_rows(h2, t + 1)[slot_tok]
        ys = _experts(xs, block_e, n_used, w_exp1[l].astype(BF16), w_exp3[l].astype(BF16), w_exp2[l].astype(BF16))
        x_all = _ln2(x_all, ys[dest[0::2]], ys[dest[1::2]], ew.T, modt, ln2_g[l].reshape(1, d),
                     ln2_b[l].reshape(1, d), alpha=alpha)
    return x_all[:t_lat].reshape(b, s, d)
```

```python
import functools
import math

import jax
import jax.numpy as jnp
from jax import lax
from jax.experimental import pallas as pl
from jax.experimental.pallas import tpu as pltpu

F32 = jnp.float32
BF16 = jnp.bfloat16

GRID_W = 64
A_HEADS = 4
A_HEAD_DIM = 64
A_WIDTH = A_HEADS * 2 * A_HEAD_DIM
G_HEADS = 4
G_KEY_DIM = 128
G_VAL_DIM = 128
G_QK_WIDTH = G_HEADS * G_KEY_DIM
G_V_WIDTH = G_HEADS * G_VAL_DIM
G_QKV_WIDTH = 2 * G_QK_WIDTH + G_V_WIDTH
CONV_K = 5
CHUNK = 64
N_EXPERTS = 16
N_GROUPS = 4
EXPERTS_PER_GROUP = N_EXPERTS // N_GROUPS
MOE_BLOCK = 256
ROPE_BASE = 10000.0
EPS = 1e-6

LANES = 128
SUBLANES = 8
TM = 256
VMEM_LIMIT = 56 * 1024 * 1024

OFF_QK = 0
OFF_VA = 2 * A_WIDTH
OFF_G = OFF_VA + A_WIDTH
OFF_Z = OFF_G + G_QKV_WIDTH
OFF_AB = OFF_Z + G_V_WIDTH
OFF_GATE = OFF_AB + LANES
GC_CUM, GC_BETA, GC_EG, GC_EK, GC_ET = 0, 8, 16, 24, 32


def _sigmoid(x):
    return 1.0 / (1.0 + jnp.exp(-x))


def _silu(x):
    return x * _sigmoid(x)


def _dot(a, b):
    return jnp.dot(a, b, preferred_element_type=F32)


def _dot_nt(a, b):
    return lax.dot_general(a, b, (((1,), (1,)), ((), ())), preferred_element_type=F32)


def _dot_f32(a, b):
    return jnp.dot(a, b, preferred_element_type=F32, precision=lax.Precision.HIGHEST)


def _params(*sem):
    return pltpu.CompilerParams(dimension_semantics=sem, vmem_limit_bytes=VMEM_LIMIT)


def _mod_kernel(cc_ref, w_ref, b_ref, o_ref):
    h = _silu(cc_ref[...])
    o_ref[...] = _dot(h.astype(BF16), w_ref[...].astype(BF16)) + b_ref[...]


def _mod_tables(cc, mod_w, mod_b):
    depth, d, n = mod_w.shape
    tn = n // 6
    return pl.pallas_call(
        _mod_kernel,
        grid=(depth, n // tn),
        in_specs=[pl.BlockSpec((SUBLANES, d), lambda l, j: (0, 0)),
                  pl.BlockSpec((None, d, tn), lambda l, j: (l, 0, j)),
                  pl.BlockSpec((None, 1, tn), lambda l, j: (l, 0, j))],
        out_specs=pl.BlockSpec((None, SUBLANES, tn), lambda l, j: (l, 0, j)),
        out_shape=jax.ShapeDtypeStruct((depth, SUBLANES, n), F32),
        compiler_params=_params("parallel", "parallel"),
    )(cc, mod_w, mod_b.reshape(depth, 1, n))


def _inproj_kernel(x_ref, mod_ref, w_ref, cs_ref, qk_ref, va_ref, g_ref, z_ref, ab_ref, gate_ref):
    m = mod_ref[...]
    h = (x_ref[...] * (1.0 + m[1:2]) + m[0:1]).astype(BF16)
    cs = cs_ref[...]
    cos, sin = cs[:, :LANES], cs[:, LANES:]
    qk = _dot(h, w_ref[:, OFF_QK:OFF_VA])
    for j in range(2 * A_HEADS):
        a = qk[:, j * LANES:(j + 1) * LANES]
        qk_ref[:, j * LANES:(j + 1) * LANES] = (a * cos + pltpu.roll(a, LANES // 2, 1) * sin).astype(BF16)
    va_ref[...] = _dot(h, w_ref[:, OFF_VA:OFF_G]).astype(BF16)
    g_ref[...] = _dot(h, w_ref[:, OFF_G:OFF_Z])
    z_ref[...] = _dot(h, w_ref[:, OFF_Z:OFF_AB])
    ab_ref[...] = _dot(h, w_ref[:, OFF_AB:OFF_GATE])
    gate_ref[...] = _sigmoid(_dot(h, w_ref[:, OFF_GATE:]))


def _inproj(x_all, modt, w_cat, cs_tab, n_lat_tiles, tiles_per_seq):
    t, d = x_all.shape
    n = w_cat.shape[1]
    d_gate = n - OFF_GATE

    def row(i):
        return (i, 0)

    def cs_map(i):
        return (jnp.where(i < n_lat_tiles, i % tiles_per_seq, tiles_per_seq), 0)

    out_shape = [jax.ShapeDtypeStruct((t, 2 * A_WIDTH), BF16),
                 jax.ShapeDtypeStruct((t, A_WIDTH), BF16),
                 jax.ShapeDtypeStruct((t, G_QKV_WIDTH), F32),
                 jax.ShapeDtypeStruct((t, G_V_WIDTH), F32),
                 jax.ShapeDtypeStruct((t, LANES), F32),
                 jax.ShapeDtypeStruct((t, d_gate), F32)]
    return pl.pallas_call(
        _inproj_kernel,
        grid=(t // TM,),
        in_specs=[pl.BlockSpec((TM, d), row),
                  pl.BlockSpec((None, 6, d), lambda i: (i, 0, 0)),
                  pl.BlockSpec((d, n), lambda i: (0, 0)),
                  pl.BlockSpec((TM, 2 * LANES), cs_map)],
        out_specs=[pl.BlockSpec((TM, s.shape[1]), row) for s in out_shape],
        out_shape=out_shape,
        compiler_params=_params("parallel"),
    )(x_all, modt, w_cat, cs_tab)


def _attn_kernel(*refs, lam_init, has_lat):
    if has_lat:
        lam_ref, q_ref, kc_ref, vc_ref, kl_ref, vl_ref, g_ref, o_ref = refs
    else:
        lam_ref, q_ref, kc_ref, vc_ref, g_ref, o_ref = refs
    lv = lam_ref[...]
    lam = (jnp.exp(jnp.sum(lv[0:1] * lv[1:2], axis=-1, keepdims=True))
           - jnp.exp(jnp.sum(lv[2:3] * lv[3:4], axis=-1, keepdims=True)) + lam_init)
    q = q_ref[...]
    lane = lax.broadcasted_iota(jnp.int32, (1, LANES), 1)
    comp1 = ((lane // (A_HEAD_DIM // 2)) % 2) == 1
    outs = []
    for n in range(2):
        qn = jnp.where(comp1 if n else jnp.logical_not(comp1), q, jnp.zeros_like(q))
        sc = _dot_nt(qn, kc_ref[...])
        m = jnp.max(sc, axis=-1, keepdims=True)
        if has_lat:
            sl = _dot_nt(qn, kl_ref[...])
            m = jnp.maximum(m, jnp.max(sl, axis=-1, keepdims=True))
        pc = jnp.exp(sc - m)
        den = jnp.sum(pc, axis=-1, keepdims=True)
        o = _dot(pc.astype(BF16), vc_ref[...])
        if has_lat:
            pl_ = jnp.exp(sl - m)
            den = den + jnp.sum(pl_, axis=-1, keepdims=True)
            o = o + _dot(pl_.astype(BF16), vl_ref[...])
        outs.append(o / den)
    o = outs[0] - lam * outs[1]
    r = o * lax.rsqrt(jnp.mean(o * o, axis=-1, keepdims=True) + EPS)
    o_ref[...] = (r * g_ref[...] * (1.0 - lam_init)).astype(BF16)


def _attention(qk, va, lamv, subln_g, ya, *, lam_init, b, s, nc, latent):
    tq = TM
    t = qk.shape[0]
    ctx0 = (b * s) // nc
    if latent:
        grid = (b, A_HEADS, s // tq)
        q_map = lambda bi, h, i: (bi * (s // tq) + i, h)
    else:
        grid = (b, A_HEADS, nc // tq)
        q_map = lambda bi, h, i: ((b * s) // tq + bi * (nc // tq) + i, h)
    in_specs = [pl.BlockSpec((SUBLANES, LANES), lambda bi, h, i: (0, 0)),
                pl.BlockSpec((tq, LANES), q_map),
                pl.BlockSpec((nc, LANES), lambda bi, h, i: (ctx0 + bi, A_HEADS + h)),
                pl.BlockSpec((nc, LANES), lambda bi, h, i: (ctx0 + bi, h))]
    args = [lamv, qk, qk, va]
    if latent:
        in_specs += [pl.BlockSpec((s, LANES), lambda bi, h, i: (bi, A_HEADS + h)),
                     pl.BlockSpec((s, LANES), lambda bi, h, i: (bi, h))]
        args += [qk, va]
    in_specs += [pl.BlockSpec((1, LANES), lambda bi, h, i: (0, 0))]
    args += [subln_g]
    aliases = {}
    kern = functools.partial(_attn_kernel, lam_init=lam_init, has_lat=latent)
    if not latent:
        in_specs += [pl.BlockSpec(memory_space=pl.ANY)]
        args += [ya]
        aliases = {len(args) - 1: 0}
        kern = functools.partial(_attn_kernel_aliased, lam_init=lam_init)
    return pl.pallas_call(
        kern,
        grid=grid,
        in_specs=in_specs,
        out_specs=pl.BlockSpec((tq, LANES), q_map),
        out_shape=jax.ShapeDtypeStruct((t, A_WIDTH), BF16),
        input_output_aliases=aliases,
        compiler_params=_params("parallel", "parallel", "arbitrary"),
    )(*args)


def _attn_kernel_aliased(*refs, lam_init):
    _attn_kernel(*refs[:-2], refs[-1], lam_init=lam_init, has_lat=False)


def _gdn_prep_kernel(g_ref, prev_ref, next_ref, ab_ref, cw_ref, gp_ref, q_ref, k_ref, v_ref, gc_ref, ext_ref,
                     *, n_lat_tiles, tiles_per_seq):
    i = pl.program_id(0)
    is_lat = i < n_lat_tiles
    first = jnp.logical_or(jnp.logical_not(is_lat), i % tiles_per_seq == 0)
    last = jnp.logical_or(jnp.logical_not(is_lat), i % tiles_per_seq == tiles_per_seq - 1)
    ext_ref[0:SUBLANES, :] = jnp.where(first, 0.0, prev_ref[...])
    ext_ref[SUBLANES:SUBLANES + TM, :] = g_ref[...]
    ext_ref[SUBLANES + TM:, :] = jnp.where(last, 0.0, next_ref[...])
    cw = cw_ref[...]
    acc = None
    for j in range(CONV_K):
        term = ext_ref[pl.ds(SUBLANES - CONV_K // 2 + j, TM), :] * cw[j:j + 1]
        acc = term if acc is None else acc + term
    y = _silu(acc)
    for h in range(G_HEADS):
        for base, ref, scale in ((0, q_ref, G_KEY_DIM ** -0.5), (G_QK_WIDTH, k_ref, 1.0)):
            xh = y[:, base + h * LANES:base + (h + 1) * LANES]
            nrm = xh * lax.rsqrt(jnp.sum(xh * xh, axis=-1, keepdims=True) + EPS)
            ref[:, h * LANES:(h + 1) * LANES] = nrm * scale
    v_ref[...] = y[:, 2 * G_QK_WIDTH:]

    ab = ab_ref[...]
    gp = gp_ref[...]
    lane = lax.broadcasted_iota(jnp.int32, (1, LANES), 1)
    z = ab + gp[1:2]
    softplus = jnp.maximum(z, 0.0) + jnp.log(1.0 + jnp.exp(-jnp.abs(z)))
    g = jnp.where(lane < 2 * G_HEADS, -jnp.exp(gp[0:1]) * softplus, 0.0)
    beta = _sigmoid(ab)
    ri = lax.broadcasted_iota(jnp.int32, (TM, TM), 0)
    ci = lax.broadcasted_iota(jnp.int32, (TM, TM), 1)
    same = (ri // CHUNK) == (ci // CHUNK)
    lower = jnp.where(jnp.logical_and(same, ci <= ri), 1.0, 0.0)
    upper = jnp.where(jnp.logical_and(same, ci >= ri), 1.0, 0.0)
    cum_f = _dot_f32(lower, g)
    cum_b = _dot_f32(upper, g)
    total = cum_f + cum_b - g
    cum = jnp.where(lane < G_HEADS, cum_f, cum_b)
    eg = jnp.exp(cum)
    ek = jnp.exp(total - cum)
    et = jnp.exp(total)
    out = jnp.where(lane < GC_BETA, cum, 0.0)
    out = jnp.where(jnp.logical_and(lane >= GC_BETA, lane < GC_EG), beta, out)
    out = jnp.where(jnp.logical_and(lane >= GC_EG, lane < GC_EK), pltpu.roll(eg, GC_EG, 1), out)
    out = jnp.where(jnp.logical_and(lane >= GC_EK, lane < GC_ET), pltpu.roll(ek, GC_EK, 1), out)
    out = jnp.where(jnp.logical_and(lane >= GC_ET, lane < GC_ET + 8), pltpu.roll(et, GC_ET, 1), out)
    gc_ref[...] = out


def _gdn_prep(g, ab, conv_w8, gparams, n_lat_tiles, tiles_per_seq):
    t, c = g.shape
    nblk8 = t // SUBLANES
    per = TM // SUBLANES
    row = lambda i: (i, 0)
    out_shape = [jax.ShapeDtypeStruct((t, G_QK_WIDTH), F32),
                 jax.ShapeDtypeStruct((t, G_QK_WIDTH), F32),
                 jax.ShapeDtypeStruct((t, G_V_WIDTH), F32),
                 jax.ShapeDtypeStruct((t, LANES), F32)]
    return pl.pallas_call(
        functools.partial(_gdn_prep_kernel, n_lat_tiles=n_lat_tiles, tiles_per_seq=tiles_per_seq),
        grid=(t // TM,),
        in_specs=[pl.BlockSpec((TM, c), row),
                  pl.BlockSpec((SUBLANES, c), lambda i: (jnp.maximum(i * per - 1, 0), 0)),
                  pl.BlockSpec((SUBLANES, c), lambda i: (jnp.minimum((i + 1) * per, nblk8 - 1), 0)),
                  pl.BlockSpec((TM, LANES), row),
                  pl.BlockSpec((SUBLANES, c), lambda i: (0, 0)),
                  pl.BlockSpec((SUBLANES, LANES), lambda i: (0, 0))],
        out_specs=[pl.BlockSpec((TM, s.shape[1]), row) for s in out_shape],
        out_shape=out_shape,
        scratch_shapes=[pltpu.VMEM((TM + 2 * SUBLANES, c), F32)],
        compiler_params=_params("parallel"),
    )(g, g, g, ab, conv_w8, gparams)


def _gdn_masks(c, reverse):
    ii = lax.broadcasted_iota(jnp.int32, (c, 2 * c), 0)
    lane = lax.broadcasted_iota(jnp.int32, (c, 2 * c), 1)
    left = lane < c
    jj = jnp.where(left, lane, lane - c)
    causal = (ii <= jj) if reverse else (ii >= jj)
    strict = (ii < jj) if reverse else (ii > jj)
    return dict(left=left, diag=ii == jj, causal=causal, strict=strict)


def _gdn_setup(q, k, v, gc, gct, col, mk):
    c = q.shape[0]
    gcol = gc[:, GC_CUM + col:GC_CUM + col + 1]
    beta = gc[:, GC_BETA + col:GC_BETA + col + 1]
    eg = gc[:, GC_EG + col:GC_EG + col + 1]
    ek = gc[:, GC_EK + col:GC_EK + col + 1]
    et = gc[0:1, GC_ET + col:GC_ET + col + 1]
    grow = gct[GC_CUM + col:GC_CUM + col + 1, :]
    decay = jnp.exp(jnp.where(mk["causal"], gcol - grow, -1e30))
    kb = k * beta
    k16 = k.astype(BF16)
    both = _dot_nt(jnp.concatenate([kb.astype(BF16), q.astype(BF16)], axis=0),
                   jnp.concatenate([k16, k16], axis=0))
    qk = jnp.where(jnp.logical_and(mk["causal"], mk["left"]), both[c:] * decay, 0.0)
    z = jnp.where(mk["left"], jnp.where(mk["diag"], 1.0, 0.0), jnp.where(mk["strict"], -(both[:c] * decay), 0.0))
    rhs = jnp.concatenate([v * beta, kb * eg], axis=1)
    return dict(z=z, qk=qk.astype(BF16), rhs=rhs, qd=(q * eg).astype(BF16), kdt=(k * ek).T.astype(BF16), et=et)


def _gdn_level(z, mk):
    z16 = z.astype(BF16)
    return jnp.where(mk["left"], z, 0.0) + _dot(z16, jnp.concatenate([jnp.zeros_like(z16), z16], axis=0))


def _gdn_solve(ch, mk):
    y16 = jnp.where(jnp.logical_and(mk["left"], jnp.logical_not(mk["diag"])), ch["z"], 0.0).astype(BF16)
    rhs16 = ch["rhs"].astype(BF16)
    sol = ch["rhs"] + _dot(y16, jnp.concatenate([rhs16, rhs16], axis=0))
    dv = sol.shape[1] // 2
    return sol[:, :dv], jnp.concatenate([sol[:, dv:].astype(BF16), ch["qd"]], axis=0)


def _gdn_step(s, u, wq, qk, kdt, et):
    c = u.shape[0]
    ws = _dot(wq, s.astype(BF16))
    v_new = (u - ws[:c]).astype(BF16)
    o = ws[c:] + _dot(qk, jnp.concatenate([v_new, v_new], axis=0))
    return s * et + _dot(kdt, v_new), o


def _gdn_kernel(qf_ref, kf_ref, vf_ref, gf_ref, qb_ref, kb_ref, vb_ref, gb_ref, of_ref, ob_ref, s_ref):
    @pl.when(pl.program_id(1) == 0)
    def _():
        s_ref[...] = jnp.zeros_like(s_ref)

    n_chunks = qf_ref.shape[0] // CHUNK
    masks = (_gdn_masks(CHUNK, False), _gdn_masks(CHUNK, True))
    dirs = ((qf_ref, kf_ref, vf_ref, gf_ref, of_ref), (qb_ref, kb_ref, vb_ref, gb_ref, ob_ref))
    chains = []
    for n in range(n_chunks):
        for d, (q_ref, k_ref, v_ref, g_ref, o_ref) in enumerate(dirs):
            rows = pl.ds((n_chunks - 1 - n if d else n) * CHUNK, CHUNK)
            gc = g_ref[rows, :]
            gct = jnp.concatenate([gc, gc], axis=0).T
            for h in range(G_HEADS):
                cols = pl.ds(h * LANES, LANES)
                ch = _gdn_setup(q_ref[rows, cols], k_ref[rows, cols], v_ref[rows, cols], gc, gct,
                                d * G_HEADS + h, masks[d])
                ch.update(d=d, h=h, rows=rows, cols=cols, out=o_ref)
                chains.append(ch)
    for _ in range(int(math.log2(CHUNK))):
        for ch in chains:
            ch["z"] = _gdn_level(ch["z"], masks[ch["d"]])
    for ch in chains:
        ch["u"], ch["wq"] = _gdn_solve(ch, masks[ch["d"]])
    state = [s_ref[i] for i in range(2 * G_HEADS)]
    for ch in chains:
        i = ch["d"] * G_HEADS + ch["h"]
        state[i], o = _gdn_step(state[i], ch["u"], ch["wq"], ch["qk"], ch["kdt"], ch["et"])
        ch["out"][ch["rows"], ch["cols"]] = o
    for i in range(2 * G_HEADS):
        s_ref[i] = state[i]


def _gdn(qg, kg, vg, gc, *, b, s, nc):
    t = qg.shape[0]
    tps = s // TM
    nct = nc // TM
    n_steps = nct + tps
    ctx_tile0 = (b * s) // TM

    def fwd(bi, i):
        return (jnp.where(i < nct, ctx_tile0 + bi * nct + i, bi * tps + (i - nct)), 0)

    def bwd(bi, i):
        return (jnp.where(i < nct, ctx_tile0 + bi * nct + (nct - 1 - i), bi * tps + (tps - 1 - (i - nct))), 0)

    def specs(row_map):
        return [pl.BlockSpec((TM, G_QK_WIDTH), row_map), pl.BlockSpec((TM, G_QK_WIDTH), row_map),
                pl.BlockSpec((TM, G_V_WIDTH), row_map), pl.BlockSpec((TM, LANES), row_map)]

    return pl.pallas_call(
        _gdn_kernel,
        grid=(b, n_steps),
        in_specs=specs(fwd) + specs(bwd),
        out_specs=[pl.BlockSpec((TM, G_V_WIDTH), fwd), pl.BlockSpec((TM, G_V_WIDTH), bwd)],
        out_shape=[jax.ShapeDtypeStruct((t, G_V_WIDTH), F32)] * 2,
        scratch_shapes=[pltpu.VMEM((2 * G_HEADS, G_KEY_DIM, G_VAL_DIM), F32)],
        compiler_params=_params("parallel", "arbitrary"),
    )(qg, kg, vg, gc, qg, kg, vg, gc)


def _layer_norm(r, g, b):
    mu = jnp.mean(r, axis=-1, keepdims=True)
    rc = r - mu
    var = jnp.mean(rc * rc, axis=-1, keepdims=True)
    return rc * lax.rsqrt(var + EPS) * g + b


def _merge_kernel(x_ref, ya_ref, of_ref, ob_ref, z_ref, gate_ref, mod_ref, wpa_ref, wpb_ref, wo_ref, on_ref,
                  lng_ref, lnb_ref, rw_ref, rb_ref, xo_ref, h2_ref, ei_ref, ew_ref, *, alpha):
    d = x_ref.shape[1]
    o = of_ref[...] + ob_ref[...]
    parts = []
    for h in range(G_HEADS):
        oh = o[:, h * LANES:(h + 1) * LANES]
        parts.append(oh * lax.rsqrt(jnp.mean(oh * oh, axis=-1, keepdims=True) + EPS) * on_ref[...])
    yb = jnp.concatenate(parts, axis=1) * _silu(z_ref[...])
    pa = _dot(ya_ref[...], wpa_ref[...])
    pb = _dot(yb.astype(BF16), wpb_ref[...])
    mix = gate_ref[:, :d] * pa + gate_ref[:, d:] * pb
    y = _dot(mix.astype(BF16), wo_ref[...])
    m = mod_ref[...]
    xn = _layer_norm(alpha * x_ref[...] + m[2:3] * y, lng_ref[...], lnb_ref[...])
    xo_ref[...] = xn
    h2 = (xn * (1.0 + m[4:5]) + m[3:4]).astype(BF16)
    h2_ref[...] = h2

    score = _sigmoid(_dot_nt(rw_ref[...], h2))
    sel = score + rb_ref[...]
    rows = [sel[e:e + 1] for e in range(N_EXPERTS)]
    best = None
    best_val = None
    for gi in range(N_GROUPS):
        a, b, c, dd = rows[gi * EXPERTS_PER_GROUP:(gi + 1) * EXPERTS_PER_GROUP]
        top2 = jnp.maximum(jnp.maximum(jnp.maximum(a + b, a + c), jnp.maximum(a + dd, b + c)),
                           jnp.maximum(b + dd, c + dd))
        if gi == 0:
            best, best_val = jnp.zeros(top2.shape, jnp.int32), top2
        else:
            upd = top2 > best_val
            best = jnp.where(upd, gi, best)
            best_val = jnp.where(upd, top2, best_val)
    in_group = []
    for j in range(EXPERTS_PER_GROUP):
        v = rows[(N_GROUPS - 1) * EXPERTS_PER_GROUP + j]
        for gi in range(N_GROUPS - 2, -1, -1):
            v = jnp.where(best == gi, rows[gi * EXPERTS_PER_GROUP + j], v)
        in_group.append(v)
    l0 = jnp.zeros(best.shape, jnp.int32)
    m0 = in_group[0]
    for j in range(1, EXPERTS_PER_GROUP):
        upd = in_group[j] > m0
        l0 = jnp.where(upd, j, l0)
        m0 = jnp.where(upd, in_group[j], m0)
    l1 = jnp.zeros(best.shape, jnp.int32)
    m1 = jnp.full(m0.shape, -jnp.inf, F32)
    for j in range(EXPERTS_PER_GROUP):
        upd = jnp.logical_and(l0 != j, in_group[j] > m1)
        l1 = jnp.where(upd, j, l1)
        m1 = jnp.where(upd, in_group[j], m1)
    e0 = best * EXPERTS_PER_GROUP + l0
    e1 = best * EXPERTS_PER_GROUP + l1
    w0 = jnp.zeros(m0.shape, F32)
    w1 = jnp.zeros(m0.shape, F32)
    for e in range(N_EXPERTS):
        w0 = jnp.where(e0 == e, score[e:e + 1], w0)
        w1 = jnp.where(e1 == e, score[e:e + 1], w1)
    tot = w0 + w1
    ri = lax.broadcasted_iota(jnp.int32, ei_ref.shape, 0)
    ei_ref[...] = jnp.where(ri == 0, e0, jnp.where(ri == 1, e1, 0))
    ew_ref[...] = jnp.where(ri == 0, w0 / tot, jnp.where(ri == 1, w1 / tot, 0.0))


def _merge(x_all, ya, o_f, o_b, z, gate, modt, wpa, wpb, wo, onorm, lng, lnb, rw_t, rb, *, alpha):
    t, d = x_all.shape
    row = lambda i: (i, 0)
    col = lambda i: (0, i)
    const = lambda i: (0, 0)
    out_shape = [jax.ShapeDtypeStruct((t, d), F32),
                 jax.ShapeDtypeStruct((t, d), BF16),
                 jax.ShapeDtypeStruct((SUBLANES, t), jnp.int32),
                 jax.ShapeDtypeStruct((SUBLANES, t), F32)]
    return pl.pallas_call(
        functools.partial(_merge_kernel, alpha=alpha),
        grid=(t // TM,),
        in_specs=[pl.BlockSpec((TM, d), row),
                  pl.BlockSpec((TM, A_WIDTH), row),
                  pl.BlockSpec((TM, G_V_WIDTH), row),
                  pl.BlockSpec((TM, G_V_WIDTH), row),
                  pl.BlockSpec((TM, G_V_WIDTH), row),
                  pl.BlockSpec((TM, 2 * d), row),
                  pl.BlockSpec((None, 6, d), lambda i: (i, 0, 0)),
                  pl.BlockSpec(wpa.shape, const),
                  pl.BlockSpec(wpb.shape, const),
                  pl.BlockSpec(wo.shape, const),
                  pl.BlockSpec((1, LANES), const),
                  pl.BlockSpec((1, d), const),
                  pl.BlockSpec((1, d), const),
                  pl.BlockSpec(rw_t.shape, const),
                  pl.BlockSpec(rb.shape, const)],
        out_specs=[pl.BlockSpec((TM, d), row), pl.BlockSpec((TM, d), row),
                   pl.BlockSpec((SUBLANES, TM), col), pl.BlockSpec((SUBLANES, TM), col)],
        out_shape=out_shape,
        compiler_params=_params("parallel"),
    )(x_all, ya, o_f, o_b, z, gate, modt, wpa, wpb, wo, onorm, lng, lnb, rw_t, rb)


def _expert_kernel(be_ref, nu_ref, x_ref, w1_ref, w3_ref, w2_ref, o_ref):
    i = pl.program_id(0)

    @pl.when(i < nu_ref[0])
    def _():
        x = x_ref[...]
        hid = _silu(_dot(x, w1_ref[...])) * _dot(x, w3_ref[...])
        o_ref[...] = _dot(hid.astype(BF16), w2_ref[...])

    @pl.when(i >= nu_ref[0])
    def _():
        o_ref[...] = jnp.zeros_like(o_ref)


def _experts(xs, block_e, n_used, w1, w3, w2):
    n_slots, d = xs.shape
    f = w1.shape[2]
    grid_spec = pltpu.PrefetchScalarGridSpec(
        num_scalar_prefetch=2,
        grid=(n_slots // MOE_BLOCK,),
        in_specs=[pl.BlockSpec((MOE_BLOCK, d), lambda i, be, nu: (i, 0)),
                  pl.BlockSpec((None, d, f), lambda i, be, nu: (be[i], 0, 0)),
                  pl.BlockSpec((None, d, f), lambda i, be, nu: (be[i], 0, 0)),
                  pl.BlockSpec((None, f, d), lambda i, be, nu: (be[i], 0, 0))],
        out_specs=pl.BlockSpec((MOE_BLOCK, d), lambda i, be, nu: (i, 0)),
    )
    return pl.pallas_call(
        _expert_kernel,
        grid_spec=grid_spec,
        out_shape=jax.ShapeDtypeStruct((n_slots, d), F32),
        compiler_params=_params("arbitrary"),
    )(block_e, n_used, xs, w1, w3, w2)


def _ln2_kernel(x_ref, y0_ref, y1_ref, ew_ref, mod_ref, g_ref, b_ref, o_ref, *, alpha):
    w = ew_ref[...]
    y = w[:, 0:1] * y0_ref[...] + w[:, 1:2] * y1_ref[...]
    m = mod_ref[...]
    o_ref[...] = _layer_norm(alpha * x_ref[...] + m[5:6] * y, g_ref[...], b_ref[...])


def _ln2(x_all, y0, y1, ew, modt, g, b, *, alpha):
    t, d = x_all.shape
    row = lambda i: (i, 0)
    const = lambda i: (0, 0)
    return pl.pallas_call(
        functools.partial(_ln2_kernel, alpha=alpha),
        grid=(t // TM,),
        in_specs=[pl.BlockSpec((TM, d), row), pl.BlockSpec((TM, d), row), pl.BlockSpec((TM, d), row),
                  pl.BlockSpec((TM, SUBLANES), row),
                  pl.BlockSpec((None, 6, d), lambda i: (i, 0, 0)),
                  pl.BlockSpec((1, d), const), pl.BlockSpec((1, d), const)],
        out_specs=pl.BlockSpec((TM, d), row),
        out_shape=jax.ShapeDtypeStruct((t, d), F32),
        compiler_params=_params("parallel"),
    )(x_all, y0, y1, ew, modt, g, b)


def _rope_table(s, nc):
    n_freq = A_HEAD_DIM // 4
    rows = s // GRID_W
    row = jnp.repeat(jnp.arange(rows, dtype=F32), GRID_W)
    col = jnp.tile(jnp.arange(GRID_W, dtype=F32), rows)
    inv = ROPE_BASE ** (-jnp.arange(n_freq, dtype=F32) / n_freq)
    ang = jnp.stack([row[:, None] * inv, col[:, None] * inv], axis=1)
    cos = jnp.cos(ang).reshape(s, 1, 1, 2, n_freq)
    sin = jnp.sin(ang).reshape(s, 1, 1, 2, n_freq)
    cos = jnp.broadcast_to(cos, (s, 2, 2, 2, n_freq)).reshape(s, LANES)
    sign = jnp.array([-1.0, 1.0], F32).reshape(1, 2, 1, 1, 1)
    sin = jnp.broadcast_to(sin * sign, (s, 2, 2, 2, n_freq)).reshape(s, LANES)
    lat = jnp.concatenate([cos, sin], axis=1)
    ident = jnp.concatenate([jnp.ones((nc, LANES), F32), jnp.zeros((nc, LANES), F32)], axis=1)
    return jnp.concatenate([lat, ident], axis=0)


def _rope_perm(w):
    d = w.shape[0]
    n_freq = A_HEAD_DIM // 4
    return w.reshape(d, A_HEADS, 2, 2, 2, n_freq).transpose(0, 1, 4, 2, 3, 5).reshape(d, A_WIDTH)


def _pack_w_in(w):
    d = w.shape[0]
    sizes = (A_WIDTH, A_WIDTH, A_WIDTH, G_QKV_WIDTH, G_V_WIDTH, 2 * G_HEADS, 2 * G_HEADS)
    offs = [0]
    for sz in sizes:
        offs.append(offs[-1] + sz)
    wq, wk, wv, wg, wz, wa, wb = (w[:, offs[i]:offs[i + 1]] for i in range(len(sizes)))
    wgate = w[:, offs[-1]:]
    wab = jnp.concatenate([wa, wb, jnp.zeros((d, LANES - 4 * G_HEADS), w.dtype)], axis=1)
    cat = jnp.concatenate([_rope_perm(wq) * (A_HEAD_DIM ** -0.5), _rope_perm(wk), wv, wg, wz, wab, wgate], axis=1)
    return cat.astype(BF16)


def _pad_rows(a, rows):
    return jnp.concatenate([a, jnp.zeros((rows - a.shape[0],) + a.shape[1:], a.dtype)], axis=0)


def _pad_lanes(a):
    return jnp.concatenate([a, jnp.zeros(a.shape[:-1] + (LANES - a.shape[-1],), a.dtype)], axis=-1)


def _dispatch(ei, t):
    n_assign = 2 * t
    flat_e = ei[:2].T.reshape(-1)
    onehot = (flat_e[:, None] == jnp.arange(N_EXPERTS, dtype=jnp.int32)[None, :]).astype(jnp.int32)
    csum = jnp.cumsum(onehot, axis=0)
    rank = jnp.sum(csum * onehot, axis=1) - 1
    counts = csum[-1]
    padded = (counts + MOE_BLOCK - 1) // MOE_BLOCK * MOE_BLOCK
    pad_end = jnp.cumsum(padded)
    pad_start = pad_end - padded
    dest = pad_start[flat_e] + rank
    n_blocks = -(-n_assign // MOE_BLOCK) + N_EXPERTS
    tok = jnp.arange(n_assign, dtype=jnp.int32) // 2
    slot_tok = jnp.full((n_blocks * MOE_BLOCK,), t, jnp.int32).at[dest].set(tok)
    block_e = jnp.minimum(jnp.searchsorted(pad_end, jnp.arange(n_blocks, dtype=jnp.int32) * MOE_BLOCK, side='right'),
                          N_EXPERTS - 1).astype(jnp.int32)
    n_used = (pad_end[-1:] // MOE_BLOCK).astype(jnp.int32)
    return dest.astype(jnp.int32), slot_tok, block_e, n_used


def kernel(x, c, ctx, c_ctx, mod_w, mod_b, w_in, conv_w, lam_q1, lam_k1, lam_q2, lam_k2, subln_g, a_log, dt_bias,
           onorm_g, w_pa, w_pb, w_o, ln1_g, ln1_b, router_w, router_b, w_exp1, w_exp3, w_exp2, ln2_g, ln2_b):
    b, s, d = x.shape
    nc = ctx.shape[1]
    depth = mod_w.shape[0]
    assert s % TM == 0 and nc % TM == 0 and s % nc == 0 and s % GRID_W == 0
    t_lat, t_ctx = b * s, b * nc
    t = t_lat + t_ctx
    n_lat_tiles = t_lat // TM
    tiles_per_seq = s // TM
    alpha = (2.0 * depth) ** 0.25

    x_all = jnp.concatenate([x.reshape(t_lat, d), ctx.reshape(t_ctx, d)], axis=0)
    cc = _pad_rows(jnp.concatenate([c, c_ctx[None, :]], axis=0), SUBLANES)
    mods = _mod_tables(cc, mod_w, mod_b)
    tile_row = jnp.concatenate([jnp.arange(n_lat_tiles, dtype=jnp.int32) // tiles_per_seq,
                                jnp.full((t_ctx // TM,), b, jnp.int32)])
    cs_tab = _rope_table(s, nc)
    rw_t = router_w.T.astype(BF16)
    rb = router_b.reshape(N_EXPERTS, 1)

    for l in range(depth):
        lam_init = 0.8 - 0.6 * math.exp(-0.3 * l)
        modt = mods[l].reshape(SUBLANES, 6, d)[tile_row]
        qk, va, g, z, ab, gate = _inproj(x_all, modt, _pack_w_in(w_in[l]), cs_tab, n_lat_tiles, tiles_per_seq)

        lamv = _pad_rows(_pad_lanes(jnp.stack([lam_q1[l], lam_k1[l], lam_q2[l], lam_k2[l]])), SUBLANES)
        sg = subln_g[l].reshape(1, LANES)
        ya = _attention(qk, va, lamv, sg, None, lam_init=lam_init, b=b, s=s, nc=nc, latent=True)
        ya = _attention(qk, va, lamv, sg, ya, lam_init=lam_init, b=b, s=s, nc=nc, latent=False)

        gparams = _pad_rows(_pad_lanes(jnp.stack([a_log[l].reshape(-1), dt_bias[l].reshape(-1)])), SUBLANES)
        qg, kg, vg, gc = _gdn_prep(g, ab, _pad_rows(conv_w[l], SUBLANES), gparams, n_lat_tiles, tiles_per_seq)
        o_f, o_b = _gdn(qg, kg, vg, gc, b=b, s=s, nc=nc)

        x_all, h2, ei, ew = _merge(x_all, ya, o_f, o_b, z, gate, modt, w_pa[l].astype(BF16), w_pb[l].astype(BF16),
                                   w_o[l].astype(BF16), onorm_g[l].reshape(1, LANES), ln1_g[l].reshape(1, d),
                                   ln1_b[l].reshape(1, d), rw_t, rb, alpha=alpha)

        dest, slot_tok, block_e, n_used = _dispatch(ei, t)
        xs = _pad_rows(h2, t + 1)[slot_tok]
        ys = _experts(xs, block_e, n_used, w_exp1[l].astype(BF16), w_exp3[l].astype(BF16), w_exp2[l].astype(BF16))
        x_all = _ln2(x_all, ys[dest[0::2]], ys[dest[1::2]], ew.T, modt, ln2_g[l].reshape(1, d),
                     ln2_b[l].reshape(1, d), alpha=alpha)
    return x_all[:t_lat].reshape(b, s, d)
```

```python
import functools
import math

import numpy as np

import jax
import jax.numpy as jnp
from jax import lax
from jax.experimental import pallas as pl
from jax.experimental.pallas import tpu as pltpu

F32 = jnp.float32
BF16 = jnp.bfloat16

GRID_W = 64
A_HEADS = 4
A_HEAD_DIM = 64
A_WIDTH = A_HEADS * 2 * A_HEAD_DIM
G_HEADS = 4
G_KEY_DIM = 128
G_VAL_DIM = 128
G_QK_WIDTH = G_HEADS * G_KEY_DIM
G_V_WIDTH = G_HEADS * G_VAL_DIM
G_QKV_WIDTH = 2 * G_QK_WIDTH + G_V_WIDTH
CONV_K = 5
CHUNK = 64
N_EXPERTS = 16
N_GROUPS = 4
EXPERTS_PER_GROUP = N_EXPERTS // N_GROUPS
MOE_BLOCK = 256
ROPE_BASE = 10000.0
EPS = 1e-6

LANES = 128
SUBLANES = 8
TM = 256
VMEM_LIMIT = 56 * 1024 * 1024

OFF_QK = 0
OFF_VA = 2 * A_WIDTH
OFF_G = OFF_VA + A_WIDTH
OFF_Z = OFF_G + G_QKV_WIDTH
OFF_AB = OFF_Z + G_V_WIDTH
OFF_GATE = OFF_AB + LANES
GC_CUM, GC_BETA, GC_EG, GC_EK, GC_ET = 0, 8, 16, 24, 32

PAIRS = tuple((lo, hi) for lo in range(EXPERTS_PER_GROUP) for hi in range(lo + 1, EXPERTS_PER_GROUP))
N_CLASSES = N_GROUPS * len(PAIRS)
CLASS_EA = np.array([g * EXPERTS_PER_GROUP + lo for g in range(N_GROUPS) for lo, _ in PAIRS], np.int32)
CLASS_EB = np.array([g * EXPERTS_PER_GROUP + hi for g in range(N_GROUPS) for _, hi in PAIRS], np.int32)


def _sigmoid(x):
    return 1.0 / (1.0 + jnp.exp(-x))


def _silu(x):
    return x * _sigmoid(x)


def _dot(a, b):
    return jnp.dot(a, b, preferred_element_type=F32)


def _dot_nt(a, b):
    return lax.dot_general(a, b, (((1,), (1,)), ((), ())), preferred_element_type=F32)


def _dot_f32(a, b):
    return jnp.dot(a, b, preferred_element_type=F32, precision=lax.Precision.HIGHEST)


def _params(*sem):
    return pltpu.CompilerParams(dimension_semantics=sem, vmem_limit_bytes=VMEM_LIMIT)


def _to_token_tiles(ref, x):
    rows, width = x.shape
    r = width // LANES
    for c in range(r):
        ref[pl.ds(c, rows, stride=r), :] = x[:, c * LANES:(c + 1) * LANES]


def _from_token_tiles(ref, rows, r):
    return jnp.concatenate([ref[pl.ds(c, rows, stride=r), :] for c in range(r)], axis=-1)


def _mod_kernel(cc_ref, w_ref, b_ref, o_ref):
    h = _silu(cc_ref[...])
    o_ref[...] = _dot(h.astype(BF16), w_ref[...].astype(BF16)) + b_ref[...]


def _mod_tables(cc, mod_w, mod_b):
    depth, d, n = mod_w.shape
    tn = n // 6
    return pl.pallas_call(
        _mod_kernel,
        grid=(depth, n // tn),
        in_specs=[pl.BlockSpec((SUBLANES, d), lambda l, j: (0, 0)),
                  pl.BlockSpec((None, d, tn), lambda l, j: (l, 0, j)),
                  pl.BlockSpec((None, 1, tn), lambda l, j: (l, 0, j))],
        out_specs=pl.BlockSpec((None, SUBLANES, tn), lambda l, j: (l, 0, j)),
        out_shape=jax.ShapeDtypeStruct((depth, SUBLANES, n), F32),
        compiler_params=_params("parallel", "parallel"),
    )(cc, mod_w, mod_b.reshape(depth, 1, n))


def _inproj_kernel(x_ref, mod_ref, w_ref, cs_ref, qk_ref, va_ref, g_ref, z_ref, ab_ref, gate_ref):
    m = mod_ref[...]
    h = (x_ref[...] * (1.0 + m[1:2]) + m[0:1]).astype(BF16)
    cs = cs_ref[...]
    cos, sin = cs[:, :LANES], cs[:, LANES:]
    qk = _dot(h, w_ref[:, OFF_QK:OFF_VA])
    for j in range(2 * A_HEADS):
        a = qk[:, j * LANES:(j + 1) * LANES]
        qk_ref[:, j * LANES:(j + 1) * LANES] = (a * cos + pltpu.roll(a, LANES // 2, 1) * sin).astype(BF16)
    va_ref[...] = _dot(h, w_ref[:, OFF_VA:OFF_G]).astype(BF16)
    g_ref[...] = _dot(h, w_ref[:, OFF_G:OFF_Z])
    z_ref[...] = _dot(h, w_ref[:, OFF_Z:OFF_AB])
    ab_ref[...] = _dot(h, w_ref[:, OFF_AB:OFF_GATE])
    gate_ref[...] = _sigmoid(_dot(h, w_ref[:, OFF_GATE:]))


def _inproj(x_all, modt, w_cat, cs_tab, n_lat_tiles, tiles_per_seq):
    t, d = x_all.shape
    n = w_cat.shape[1]
    d_gate = n - OFF_GATE

    def row(i):
        return (i, 0)

    def cs_map(i):
        return (jnp.where(i < n_lat_tiles, i % tiles_per_seq, tiles_per_seq), 0)

    out_shape = [jax.ShapeDtypeStruct((t, 2 * A_WIDTH), BF16),
                 jax.ShapeDtypeStruct((t, A_WIDTH), BF16),
                 jax.ShapeDtypeStruct((t, G_QKV_WIDTH), F32),
                 jax.ShapeDtypeStruct((t, G_V_WIDTH), F32),
                 jax.ShapeDtypeStruct((t, LANES), F32),
                 jax.ShapeDtypeStruct((t, d_gate), F32)]
    return pl.pallas_call(
        _inproj_kernel,
        grid=(t // TM,),
        in_specs=[pl.BlockSpec((TM, d), row),
                  pl.BlockSpec((None, 6, d), lambda i: (i, 0, 0)),
                  pl.BlockSpec((d, n), lambda i: (0, 0)),
                  pl.BlockSpec((TM, 2 * LANES), cs_map)],
        out_specs=[pl.BlockSpec((TM, s.shape[1]), row) for s in out_shape],
        out_shape=out_shape,
        compiler_params=_params("parallel"),
    )(x_all, modt, w_cat, cs_tab)


def _attn_body(lam_ref, q_ref, kv_refs, g_ref, o_ref, lam_init):
    lv = lam_ref[...]
    lam = (jnp.exp(jnp.sum(lv[0:1] * lv[1:2], axis=-1, keepdims=True))
           - jnp.exp(jnp.sum(lv[2:3] * lv[3:4], axis=-1, keepdims=True)) + lam_init)
    q = q_ref[...]
    lane = lax.broadcasted_iota(jnp.int32, (1, LANES), 1)
    comp1 = ((lane // (A_HEAD_DIM // 2)) % 2) == 1
    outs = []
    for n in range(2):
        qn = jnp.where(comp1 if n else jnp.logical_not(comp1), q, jnp.zeros_like(q))
        scores = [_dot_nt(qn, k_ref[...]) for k_ref, _ in kv_refs]
        m = None
        for sc in scores:
            mc = jnp.max(sc, axis=-1, keepdims=True)
            m = mc if m is None else jnp.maximum(m, mc)
        den, o = None, None
        for sc, (_, v_ref) in zip(scores, kv_refs):
            p = jnp.exp(sc - m)
            dc = jnp.sum(p, axis=-1, keepdims=True)
            oc = _dot(p.astype(BF16), v_ref[...])
            den, o = (dc, oc) if den is None else (den + dc, o + oc)
        outs.append(o / den)
    o = outs[0] - lam * outs[1]
    r = o * lax.rsqrt(jnp.mean(o * o, axis=-1, keepdims=True) + EPS)
    o_ref[...] = (r * g_ref[...] * (1.0 - lam_init)).astype(BF16)


def _attn_kernel(lam_ref, q_ref, kc_ref, vc_ref, kl_ref, vl_ref, g_ref, o_ref, *, lam_init, n_lat_q):
    is_lat = pl.program_id(2) < n_lat_q

    @pl.when(is_lat)
    def _():
        _attn_body(lam_ref, q_ref, ((kc_ref, vc_ref), (kl_ref, vl_ref)), g_ref, o_ref, lam_init)

    @pl.when(jnp.logical_not(is_lat))
    def _():
        _attn_body(lam_ref, q_ref, ((kc_ref, vc_ref),), g_ref, o_ref, lam_init)


def _attention(qk, va, lamv, subln_g, *, lam_init, b, s, nc):
    tq = TM
    t = qk.shape[0]
    n_lat_q, n_ctx_q = s // tq, nc // tq
    ctx0 = (b * s) // nc

    def q_map(bi, h, i):
        return (jnp.where(i < n_lat_q, bi * n_lat_q + i, (b * s) // tq + bi * n_ctx_q + (i - n_lat_q)), h)

    return pl.pallas_call(
        functools.partial(_attn_kernel, lam_init=lam_init, n_lat_q=n_lat_q),
        grid=(b, A_HEADS, n_lat_q + n_ctx_q),
        in_specs=[pl.BlockSpec((SUBLANES, LANES), lambda bi, h, i: (0, 0)),
                  pl.BlockSpec((tq, LANES), q_map),
                  pl.BlockSpec((nc, LANES), lambda bi, h, i: (ctx0 + bi, A_HEADS + h)),
                  pl.BlockSpec((nc, LANES), lambda bi, h, i: (ctx0 + bi, h)),
                  pl.BlockSpec((s, LANES), lambda bi, h, i: (bi, A_HEADS + h)),
                  pl.BlockSpec((s, LANES), lambda bi, h, i: (bi, h)),
                  pl.BlockSpec((1, LANES), lambda bi, h, i: (0, 0))],
        out_specs=pl.BlockSpec((tq, LANES), q_map),
        out_shape=jax.ShapeDtypeStruct((t, A_WIDTH), BF16),
        compiler_params=_params("parallel", "parallel", "arbitrary"),
    )(lamv, qk, qk, va, qk, va, subln_g)


def _gdn_prep_kernel(g_ref, prev_ref, next_ref, ab_ref, cw_ref, gp_ref, q_ref, k_ref, v_ref, gc_ref, ext_ref,
                     *, n_lat_tiles, tiles_per_seq):
    i = pl.program_id(0)
    is_lat = i < n_lat_tiles
    first = jnp.logical_or(jnp.logical_not(is_lat), i % tiles_per_seq == 0)
    last = jnp.logical_or(jnp.logical_not(is_lat), i % tiles_per_seq == tiles_per_seq - 1)
    ext_ref[0:SUBLANES, :] = jnp.where(first, 0.0, prev_ref[...])
    ext_ref[SUBLANES:SUBLANES + TM, :] = g_ref[...]
    ext_ref[SUBLANES + TM:, :] = jnp.where(last, 0.0, next_ref[...])
    cw = cw_ref[...]
    acc = None
    for j in range(CONV_K):
        term = ext_ref[pl.ds(SUBLANES - CONV_K // 2 + j, TM), :] * cw[j:j + 1]
        acc = term if acc is None else acc + term
    y = _silu(acc)
    for h in range(G_HEADS):
        for base, ref, scale in ((0, q_ref, G_KEY_DIM ** -0.5), (G_QK_WIDTH, k_ref, 1.0)):
            xh = y[:, base + h * LANES:base + (h + 1) * LANES]
            nrm = xh * lax.rsqrt(jnp.sum(xh * xh, axis=-1, keepdims=True) + EPS)
            ref[:, h * LANES:(h + 1) * LANES] = nrm * scale
    v_ref[...] = y[:, 2 * G_QK_WIDTH:]

    ab = ab_ref[...]
    gp = gp_ref[...]
    lane = lax.broadcasted_iota(jnp.int32, (1, LANES), 1)
    z = ab + gp[1:2]
    softplus = jnp.maximum(z, 0.0) + jnp.log(1.0 + jnp.exp(-jnp.abs(z)))
    g = jnp.where(lane < 2 * G_HEADS, -jnp.exp(gp[0:1]) * softplus, 0.0)
    beta = _sigmoid(ab)
    ri = lax.broadcasted_iota(jnp.int32, (TM, TM), 0)
    ci = lax.broadcasted_iota(jnp.int32, (TM, TM), 1)
    same = (ri // CHUNK) == (ci // CHUNK)
    lower = jnp.where(jnp.logical_and(same, ci <= ri), 1.0, 0.0)
    upper = jnp.where(jnp.logical_and(same, ci >= ri), 1.0, 0.0)
    cum_f = _dot_f32(lower, g)
    cum_b = _dot_f32(upper, g)
    total = cum_f + cum_b - g
    cum = jnp.where(lane < G_HEADS, cum_f, cum_b)
    eg = jnp.exp(cum)
    ek = jnp.exp(total - cum)
    et = jnp.exp(total)
    out = jnp.where(lane < GC_BETA, cum, 0.0)
    out = jnp.where(jnp.logical_and(lane >= GC_BETA, lane < GC_EG), beta, out)
    out = jnp.where(jnp.logical_and(lane >= GC_EG, lane < GC_EK), pltpu.roll(eg, GC_EG, 1), out)
    out = jnp.where(jnp.logical_and(lane >= GC_EK, lane < GC_ET), pltpu.roll(ek, GC_EK, 1), out)
    out = jnp.where(jnp.logical_and(lane >= GC_ET, lane < GC_ET + 8), pltpu.roll(et, GC_ET, 1), out)
    gc_ref[...] = out


def _gdn_prep(g, ab, conv_w8, gparams, n_lat_tiles, tiles_per_seq):
    t, c = g.shape
    nblk8 = t // SUBLANES
    per = TM // SUBLANES
    row = lambda i: (i, 0)
    out_shape = [jax.ShapeDtypeStruct((t, G_QK_WIDTH), F32),
                 jax.ShapeDtypeStruct((t, G_QK_WIDTH), F32),
                 jax.ShapeDtypeStruct((t, G_V_WIDTH), F32),
                 jax.ShapeDtypeStruct((t, LANES), F32)]
    return pl.pallas_call(
        functools.partial(_gdn_prep_kernel, n_lat_tiles=n_lat_tiles, tiles_per_seq=tiles_per_seq),
        grid=(t // TM,),
        in_specs=[pl.BlockSpec((TM, c), row),
                  pl.BlockSpec((SUBLANES, c), lambda i: (jnp.maximum(i * per - 1, 0), 0)),
                  pl.BlockSpec((SUBLANES, c), lambda i: (jnp.minimum((i + 1) * per, nblk8 - 1), 0)),
                  pl.BlockSpec((TM, LANES), row),
                  pl.BlockSpec((SUBLANES, c), lambda i: (0, 0)),
                  pl.BlockSpec((SUBLANES, LANES), lambda i: (0, 0))],
        out_specs=[pl.BlockSpec((TM, s.shape[1]), row) for s in out_shape],
        out_shape=out_shape,
        scratch_shapes=[pltpu.VMEM((TM + 2 * SUBLANES, c), F32)],
        compiler_params=_params("parallel"),
    )(g, g, g, ab, conv_w8, gparams)


def _gdn_masks(c, reverse):
    ii = lax.broadcasted_iota(jnp.int32, (c, 2 * c), 0)
    lane = lax.broadcasted_iota(jnp.int32, (c, 2 * c), 1)
    left = lane < c
    jj = jnp.where(left, lane, lane - c)
    causal = (ii <= jj) if reverse else (ii >= jj)
    strict = (ii < jj) if reverse else (ii > jj)
    return dict(left=left, diag=ii == jj, causal=causal, strict=strict)


def _gdn_setup(q, k, v, gc, gct, col, mk):
    c = q.shape[0]
    gcol = gc[:, GC_CUM + col:GC_CUM + col + 1]
    beta = gc[:, GC_BETA + col:GC_BETA + col + 1]
    eg = gc[:, GC_EG + col:GC_EG + col + 1]
    ek = gc[:, GC_EK + col:GC_EK + col + 1]
    et = gc[0:1, GC_ET + col:GC_ET + col + 1]
    grow = gct[GC_CUM + col:GC_CUM + col + 1, :]
    decay = jnp.exp(jnp.where(mk["causal"], gcol - grow, -1e30))
    kb = k * beta
    k16 = k.astype(BF16)
    both = _dot_nt(jnp.concatenate([kb.astype(BF16), q.astype(BF16)], axis=0),
                   jnp.concatenate([k16, k16], axis=0))
    qk = jnp.where(jnp.logical_and(mk["causal"], mk["left"]), both[c:] * decay, 0.0)
    z = jnp.where(mk["left"], jnp.where(mk["diag"], 1.0, 0.0), jnp.where(mk["strict"], -(both[:c] * decay), 0.0))
    rhs = jnp.concatenate([v * beta, kb * eg], axis=1)
    return dict(z=z, qk=qk.astype(BF16), rhs=rhs, qd=(q * eg).astype(BF16), kdt=(k * ek).T.astype(BF16), et=et)


def _gdn_level(z, mk):
    z16 = z.astype(BF16)
    return jnp.where(mk["left"], z, 0.0) + _dot(z16, jnp.concatenate([jnp.zeros_like(z16), z16], axis=0))


def _gdn_solve(ch, mk):
    y16 = jnp.where(jnp.logical_and(mk["left"], jnp.logical_not(mk["diag"])), ch["z"], 0.0).astype(BF16)
    rhs16 = ch["rhs"].astype(BF16)
    sol = ch["rhs"] + _dot(y16, jnp.concatenate([rhs16, rhs16], axis=0))
    dv = sol.shape[1] // 2
    return sol[:, :dv], jnp.concatenate([sol[:, dv:].astype(BF16), ch["qd"]], axis=0)


def _gdn_step(s, u, wq, qk, kdt, et):
    c = u.shape[0]
    ws = _dot(wq, s.astype(BF16))
    v_new = (u - ws[:c]).astype(BF16)
    o = ws[c:] + _dot(qk, jnp.concatenate([v_new, v_new], axis=0))
    return s * et + _dot(kdt, v_new), o


def _gdn_kernel(qf_ref, kf_ref, vf_ref, gf_ref, qb_ref, kb_ref, vb_ref, gb_ref, of_ref, ob_ref, s_ref):
    @pl.when(pl.program_id(1) == 0)
    def _():
        s_ref[...] = jnp.zeros_like(s_ref)

    n_chunks = qf_ref.shape[0] // CHUNK
    masks = (_gdn_masks(CHUNK, False), _gdn_masks(CHUNK, True))
    dirs = ((qf_ref, kf_ref, vf_ref, gf_ref, of_ref), (qb_ref, kb_ref, vb_ref, gb_ref, ob_ref))
    chains = []
    for n in range(n_chunks):
        for d, (q_ref, k_ref, v_ref, g_ref, o_ref) in enumerate(dirs):
            rows = pl.ds((n_chunks - 1 - n if d else n) * CHUNK, CHUNK)
            gc = g_ref[rows, :]
            gct = jnp.concatenate([gc, gc], axis=0).T
            for h in range(G_HEADS):
                cols = pl.ds(h * LANES, LANES)
                ch = _gdn_setup(q_ref[rows, cols], k_ref[rows, cols], v_ref[rows, cols], gc, gct,
                                d * G_HEADS + h, masks[d])
                ch.update(d=d, h=h, rows=rows, cols=cols, out=o_ref)
                chains.append(ch)
    for _ in range(int(math.log2(CHUNK))):
        for ch in chains:
            ch["z"] = _gdn_level(ch["z"], masks[ch["d"]])
    for ch in chains:
        ch["u"], ch["wq"] = _gdn_solve(ch, masks[ch["d"]])
    state = [s_ref[i] for i in range(2 * G_HEADS)]
    for ch in chains:
        i = ch["d"] * G_HEADS + ch["h"]
        state[i], o = _gdn_step(state[i], ch["u"], ch["wq"], ch["qk"], ch["kdt"], ch["et"])
        ch["out"][ch["rows"], ch["cols"]] = o
    for i in range(2 * G_HEADS):
        s_ref[i] = state[i]


def _gdn(qg, kg, vg, gc, *, b, s, nc):
    t = qg.shape[0]
    tps = s // TM
    nct = nc // TM
    n_steps = nct + tps
    ctx_tile0 = (b * s) // TM

    def fwd(bi, i):
        return (jnp.where(i < nct, ctx_tile0 + bi * nct + i, bi * tps + (i - nct)), 0)

    def bwd(bi, i):
        return (jnp.where(i < nct, ctx_tile0 + bi * nct + (nct - 1 - i), bi * tps + (tps - 1 - (i - nct))), 0)

    def specs(row_map):
        return [pl.BlockSpec((TM, G_QK_WIDTH), row_map), pl.BlockSpec((TM, G_QK_WIDTH), row_map),
                pl.BlockSpec((TM, G_V_WIDTH), row_map), pl.BlockSpec((TM, LANES), row_map)]

    return pl.pallas_call(
        _gdn_kernel,
        grid=(b, n_steps),
        in_specs=specs(fwd) + specs(bwd),
        out_specs=[pl.BlockSpec((TM, G_V_WIDTH), fwd), pl.BlockSpec((TM, G_V_WIDTH), bwd)],
        out_shape=[jax.ShapeDtypeStruct((t, G_V_WIDTH), F32)] * 2,
        scratch_shapes=[pltpu.VMEM((2 * G_HEADS, G_KEY_DIM, G_VAL_DIM), F32)],
        compiler_params=_params("parallel", "arbitrary"),
    )(qg, kg, vg, gc, qg, kg, vg, gc)


def _layer_norm(r, g, b):
    mu = jnp.mean(r, axis=-1, keepdims=True)
    rc = r - mu
    var = jnp.mean(rc * rc, axis=-1, keepdims=True)
    return rc * lax.rsqrt(var + EPS) * g + b


def _merge_kernel(x_ref, ya_ref, of_ref, ob_ref, z_ref, gate_ref, mod_ref, wpa_ref, wpb_ref, wo_ref, on_ref,
                  lng_ref, lnb_ref, rw_ref, rb_ref, xo_ref, ht_ref, cls_ref, ew_ref, *, alpha):
    d = x_ref.shape[1]
    o = of_ref[...] + ob_ref[...]
    parts = []
    for h in range(G_HEADS):
        oh = o[:, h * LANES:(h + 1) * LANES]
        parts.append(oh * lax.rsqrt(jnp.mean(oh * oh, axis=-1, keepdims=True) + EPS) * on_ref[...])
    yb = jnp.concatenate(parts, axis=1) * _silu(z_ref[...])
    pa = _dot(ya_ref[...], wpa_ref[...])
    pb = _dot(yb.astype(BF16), wpb_ref[...])
    mix = gate_ref[:, :d] * pa + gate_ref[:, d:] * pb
    y = _dot(mix.astype(BF16), wo_ref[...])
    m = mod_ref[...]
    xn = _layer_norm(alpha * x_ref[...] + m[2:3] * y, lng_ref[...], lnb_ref[...])
    xo_ref[...] = xn
    h2 = xn * (1.0 + m[4:5]) + m[3:4]
    _to_token_tiles(ht_ref, h2)

    score = _sigmoid(_dot_nt(rw_ref[...], h2.astype(BF16)))
    sel = score + rb_ref[...]
    rows = [sel[e:e + 1] for e in range(N_EXPERTS)]
    best = None
    best_val = None
    for gi in range(N_GROUPS):
        a, b, c, dd = rows[gi * EXPERTS_PER_GROUP:(gi + 1) * EXPERTS_PER_GROUP]
        top2 = jnp.maximum(jnp.maximum(jnp.maximum(a + b, a + c), jnp.maximum(a + dd, b + c)),
                           jnp.maximum(b + dd, c + dd))
        if gi == 0:
            best, best_val = jnp.zeros(top2.shape, jnp.int32), top2
        else:
            upd = top2 > best_val
            best = jnp.where(upd, gi, best)
            best_val = jnp.where(upd, top2, best_val)
    in_group, in_score = [], []
    for j in range(EXPERTS_PER_GROUP):
        v = rows[(N_GROUPS - 1) * EXPERTS_PER_GROUP + j]
        sv = score[(N_GROUPS - 1) * EXPERTS_PER_GROUP + j:(N_GROUPS - 1) * EXPERTS_PER_GROUP + j + 1]
        for gi in range(N_GROUPS - 2, -1, -1):
            e = gi * EXPERTS_PER_GROUP + j
            v = jnp.where(best == gi, rows[e], v)
            sv = jnp.where(best == gi, score[e:e + 1], sv)
        in_group.append(v)
        in_score.append(sv)
    l0 = jnp.zeros(best.shape, jnp.int32)
    m0 = in_group[0]
    for j in range(1, EXPERTS_PER_GROUP):
        upd = in_group[j] > m0
        l0 = jnp.where(upd, j, l0)
        m0 = jnp.where(upd, in_group[j], m0)
    l1 = jnp.zeros(best.shape, jnp.int32)
    m1 = jnp.full(m0.shape, -jnp.inf, F32)
    for j in range(EXPERTS_PER_GROUP):
        upd = jnp.logical_and(l0 != j, in_group[j] > m1)
        l1 = jnp.where(upd, j, l1)
        m1 = jnp.where(upd, in_group[j], m1)
    lo = jnp.minimum(l0, l1)
    hi = jnp.maximum(l0, l1)
    w_lo = jnp.zeros(m0.shape, F32)
    w_hi = jnp.zeros(m0.shape, F32)
    for j in range(EXPERTS_PER_GROUP):
        w_lo = jnp.where(lo == j, in_score[j], w_lo)
        w_hi = jnp.where(hi == j, in_score[j], w_hi)
    pair = jnp.where(lo == 0, hi - 1, jnp.where(lo == 1, hi + 1, len(PAIRS) - 1))
    tot = w_lo + w_hi
    ri = lax.broadcasted_iota(jnp.int32, cls_ref.shape, 0)
    cls_ref[...] = jnp.where(ri == 0, best * len(PAIRS) + pair, 0)
    ew_ref[...] = jnp.where(ri == 0, w_lo / tot, jnp.where(ri == 1, w_hi / tot, 0.0))


def _merge(x_all, ya, o_f, o_b, z, gate, modt, wpa, wpb, wo, onorm, lng, lnb, rw_t, rb, *, alpha):
    t, d = x_all.shape
    r = d // LANES
    row = lambda i: (i, 0)
    col = lambda i: (0, i)
    const = lambda i: (0, 0)
    out_shape = [jax.ShapeDtypeStruct((t, d), F32),
                 jax.ShapeDtypeStruct((t * r, LANES), F32),
                 jax.ShapeDtypeStruct((SUBLANES, t), jnp.int32),
                 jax.ShapeDtypeStruct((SUBLANES, t), F32)]
    return pl.pallas_call(
        functools.partial(_merge_kernel, alpha=alpha),
        grid=(t // TM,),
        in_specs=[pl.BlockSpec((TM, d), row),
                  pl.BlockSpec((TM, A_WIDTH), row),
                  pl.BlockSpec((TM, G_V_WIDTH), row),
                  pl.BlockSpec((TM, G_V_WIDTH), row),
                  pl.BlockSpec((TM, G_V_WIDTH), row),
                  pl.BlockSpec((TM, 2 * d), row),
                  pl.BlockSpec((None, 6, d), lambda i: (i, 0, 0)),
                  pl.BlockSpec(wpa.shape, const),
                  pl.BlockSpec(wpb.shape, const),
                  pl.BlockSpec(wo.shape, const),
                  pl.BlockSpec((1, LANES), const),
                  pl.BlockSpec((1, d), const),
                  pl.BlockSpec((1, d), const),
                  pl.BlockSpec(rw_t.shape, const),
                  pl.BlockSpec(rb.shape, const)],
        out_specs=[pl.BlockSpec((TM, d), row), pl.BlockSpec((TM * r, LANES), row),
                   pl.BlockSpec((SUBLANES, TM), col), pl.BlockSpec((SUBLANES, TM), col)],
        out_shape=out_shape,
        compiler_params=_params("parallel"),
    )(x_all, ya, o_f, o_b, z, gate, modt, wpa, wpb, wo, onorm, lng, lnb, rw_t, rb)


def _expert_kernel(ea_ref, eb_ref, nu_ref, src_ref, dst_ref, h_hbm, ws_ref, w1a_ref, w3a_ref, w2a_ref,
                   w1b_ref, w3b_ref, w2b_ref, y_hbm, xbuf, ybuf, gsem, ssem, *, r):
    i = pl.program_id(0)
    n_used = nu_ref[0]
    slot = i % 2

    def gather_copy(blk, buf, j):
        tok = src_ref[blk * MOE_BLOCK + j]
        return pltpu.make_async_copy(h_hbm.at[pl.ds(pl.multiple_of(tok * r, r), r), :],
                                     xbuf.at[buf, pl.ds(pl.multiple_of(j * r, r), r), :], gsem.at[buf])

    def scatter(blk, buf, j, start):
        tok = dst_ref[blk * MOE_BLOCK + j]

        @pl.when(tok >= 0)
        def _():
            cp = pltpu.make_async_copy(ybuf.at[buf, pl.ds(pl.multiple_of(j * r, r), r), :],
                                       y_hbm.at[pl.ds(pl.multiple_of(tok * r, r), r), :], ssem.at[buf])
            if start:
                cp.start()
            else:
                cp.wait()

    def for_slots(fn):
        def body(j, carry):
            fn(j)
            return carry
        lax.fori_loop(0, MOE_BLOCK, body, 0, unroll=8)

    @pl.when(i == 0)
    def _():
        for_slots(lambda j: gather_copy(0, 0, j).start())

    @pl.when(i < n_used)
    def _():
        for_slots(lambda j: gather_copy(i, slot, j).wait())

        @pl.when(i + 1 < n_used)
        def _():
            for_slots(lambda j: gather_copy(i + 1, 1 - slot, j).start())

        x = _from_token_tiles(xbuf.at[slot], MOE_BLOCK, r).astype(BF16)
        ws = ws_ref[...]

        def ffn(w1_ref, w3_ref, w2_ref):
            hid = _silu(_dot(x, w1_ref[...])) * _dot(x, w3_ref[...])
            return _dot(hid.astype(BF16), w2_ref[...])

        y = ws[:, 0:1] * ffn(w1a_ref, w3a_ref, w2a_ref) + ws[:, 1:2] * ffn(w1b_ref, w3b_ref, w2b_ref)

        _to_token_tiles(ybuf.at[slot], y)

        @pl.when(i >= 1)
        def _():
            for_slots(lambda j: scatter(i - 1, 1 - slot, j, False))

        for_slots(lambda j: scatter(i, slot, j, True))

        @pl.when(i == n_used - 1)
        def _():
            for_slots(lambda j: scatter(i, slot, j, False))


def _experts(ht, w_slot, block_ea, block_eb, n_used, slot_src, slot_dst, w1, w3, w2, n_out_tokens):
    d, f = w1.shape[1], w1.shape[2]
    r = d // LANES
    n_slots = slot_src.shape[0]

    def wspec(shape, which):
        if which == 0:
            return pl.BlockSpec((None,) + shape, lambda i, ea, eb, nu, src, dst: (ea[i], 0, 0))
        return pl.BlockSpec((None,) + shape, lambda i, ea, eb, nu, src, dst: (eb[i], 0, 0))

    grid_spec = pltpu.PrefetchScalarGridSpec(
        num_scalar_prefetch=5,
        grid=(n_slots // MOE_BLOCK,),
        in_specs=[pl.BlockSpec(memory_space=pl.ANY),
                  pl.BlockSpec((MOE_BLOCK, 2), lambda i, ea, eb, nu, src, dst: (i, 0)),
                  wspec((d, f), 0), wspec((d, f), 0), wspec((f, d), 0),
                  wspec((d, f), 1), wspec((d, f), 1), wspec((f, d), 1)],
        out_specs=pl.BlockSpec(memory_space=pl.ANY),
        scratch_shapes=[pltpu.VMEM((2, MOE_BLOCK * r, LANES), F32),
                        pltpu.VMEM((2, MOE_BLOCK * r, LANES), F32),
                        pltpu.SemaphoreType.DMA((2,)),
                        pltpu.SemaphoreType.DMA((2,))],
    )
    return pl.pallas_call(
        functools.partial(_expert_kernel, r=r),
        grid_spec=grid_spec,
        out_shape=jax.ShapeDtypeStruct((n_out_tokens * r, LANES), F32),
        compiler_params=_params("arbitrary"),
    )(block_ea, block_eb, n_used, slot_src, slot_dst, ht, w_slot, w1, w3, w2, w1, w3, w2)


def _ln2_kernel(x_ref, yt_ref, mod_ref, g_ref, b_ref, o_ref, *, alpha):
    tm, d = x_ref.shape
    y = _from_token_tiles(yt_ref, tm, d // LANES)
    m = mod_ref[...]
    o_ref[...] = _layer_norm(alpha * x_ref[...] + m[5:6] * y, g_ref[...], b_ref[...])


def _ln2(x_all, yt, modt, g, b, *, alpha):
    t, d = x_all.shape
    r = d // LANES
    row = lambda i: (i, 0)
    const = lambda i: (0, 0)
    return pl.pallas_call(
        functools.partial(_ln2_kernel, alpha=alpha),
        grid=(t // TM,),
        in_specs=[pl.BlockSpec((TM, d), row), pl.BlockSpec((TM * r, LANES), row),
                  pl.BlockSpec((None, 6, d), lambda i: (i, 0, 0)),
                  pl.BlockSpec((1, d), const), pl.BlockSpec((1, d), const)],
        out_specs=pl.BlockSpec((TM, d), row),
        out_shape=jax.ShapeDtypeStruct((t, d), F32),
        compiler_params=_params("parallel"),
    )(x_all, yt, modt, g, b)


def _rope_table(s, nc):
    n_freq = A_HEAD_DIM // 4
    rows = s // GRID_W
    row = jnp.repeat(jnp.arange(rows, dtype=F32), GRID_W)
    col = jnp.tile(jnp.arange(GRID_W, dtype=F32), rows)
    inv = ROPE_BASE ** (-jnp.arange(n_freq, dtype=F32) / n_freq)
    ang = jnp.stack([row[:, None] * inv, col[:, None] * inv], axis=1)
    cos = jnp.cos(ang).reshape(s, 1, 1, 2, n_freq)
    sin = jnp.sin(ang).reshape(s, 1, 1, 2, n_freq)
    cos = jnp.broadcast_to(cos, (s, 2, 2, 2, n_freq)).reshape(s, LANES)
    sign = jnp.array([-1.0, 1.0], F32).reshape(1, 2, 1, 1, 1)
    sin = jnp.broadcast_to(sin * sign, (s, 2, 2, 2, n_freq)).reshape(s, LANES)
    lat = jnp.concatenate([cos, sin], axis=1)
    ident = jnp.concatenate([jnp.ones((nc, LANES), F32), jnp.zeros((nc, LANES), F32)], axis=1)
    return jnp.concatenate([lat, ident], axis=0)


def _rope_perm(w):
    d = w.shape[0]
    n_freq = A_HEAD_DIM // 4
    return w.reshape(d, A_HEADS, 2, 2, 2, n_freq).transpose(0, 1, 4, 2, 3, 5).reshape(d, A_WIDTH)


def _pack_w_in(w):
    d = w.shape[0]
    sizes = (A_WIDTH, A_WIDTH, A_WIDTH, G_QKV_WIDTH, G_V_WIDTH, 2 * G_HEADS, 2 * G_HEADS)
    offs = [0]
    for sz in sizes:
        offs.append(offs[-1] + sz)
    wq, wk, wv, wg, wz, wa, wb = (w[:, offs[i]:offs[i + 1]] for i in range(len(sizes)))
    wgate = w[:, offs[-1]:]
    wab = jnp.concatenate([wa, wb, jnp.zeros((d, LANES - 4 * G_HEADS), w.dtype)], axis=1)
    cat = jnp.concatenate([_rope_perm(wq) * (A_HEAD_DIM ** -0.5), _rope_perm(wk), wv, wg, wz, wab, wgate], axis=1)
    return cat.astype(BF16)


def _pad_rows(a, rows):
    return jnp.concatenate([a, jnp.zeros((rows - a.shape[0],) + a.shape[1:], a.dtype)], axis=0)


def _pad_lanes(a):
    return jnp.concatenate([a, jnp.zeros(a.shape[:-1] + (LANES - a.shape[-1],), a.dtype)], axis=-1)


def _dispatch(cls, ew, t):
    onehot = (cls[:, None] == jnp.arange(N_CLASSES, dtype=jnp.int32)[None, :]).astype(jnp.int32)
    csum = jnp.cumsum(onehot, axis=0)
    rank = jnp.sum(csum * onehot, axis=1) - 1
    counts = csum[-1]
    padded = (counts + MOE_BLOCK - 1) // MOE_BLOCK * MOE_BLOCK
    pad_end = jnp.cumsum(padded)
    dest = (pad_end - padded)[cls] + rank
    n_blocks = -(-t // MOE_BLOCK) + N_CLASSES
    n_slots = n_blocks * MOE_BLOCK
    tok = jnp.arange(t, dtype=jnp.int32)
    slot_src = jnp.zeros((n_slots,), jnp.int32).at[dest].set(tok)
    slot_dst = jnp.full((n_slots,), -1, jnp.int32).at[dest].set(tok)
    w_slot = jnp.zeros((n_slots, 2), F32).at[dest].set(ew)
    block_cls = jnp.minimum(jnp.searchsorted(pad_end, jnp.arange(n_blocks, dtype=jnp.int32) * MOE_BLOCK,
                                             side='right'), N_CLASSES - 1)
    n_used = (pad_end[-1:] // MOE_BLOCK).astype(jnp.int32)
    return slot_src, slot_dst, w_slot, jnp.asarray(CLASS_EA)[block_cls], jnp.asarray(CLASS_EB)[block_cls], n_used


def kernel(x, c, ctx, c_ctx, mod_w, mod_b, w_in, conv_w, lam_q1, lam_k1, lam_q2, lam_k2, subln_g, a_log, dt_bias,
           onorm_g, w_pa, w_pb, w_o, ln1_g, ln1_b, router_w, router_b, w_exp1, w_exp3, w_exp2, ln2_g, ln2_b):
    b, s, d = x.shape
    nc = ctx.shape[1]
    depth = mod_w.shape[0]
    assert s % TM == 0 and nc % TM == 0 and s % nc == 0 and s % GRID_W == 0
    assert d == SUBLANES * LANES
    t_lat, t_ctx = b * s, b * nc
    t = t_lat + t_ctx
    n_lat_tiles = t_lat // TM
    tiles_per_seq = s // TM
    alpha = (2.0 * depth) ** 0.25

    x_all = jnp.concatenate([x.reshape(t_lat, d), ctx.reshape(t_ctx, d)], axis=0)
    cc = _pad_rows(jnp.concatenate([c, c_ctx[None, :]], axis=0), SUBLANES)
    mods = _mod_tables(cc, mod_w, mod_b)
    tile_row = jnp.concatenate([jnp.arange(n_lat_tiles, dtype=jnp.int32) // tiles_per_seq,
                                jnp.full((t_ctx // TM,), b, jnp.int32)])
    cs_tab = _rope_table(s, nc)
    rw_t = router_w.T.astype(BF16)
    rb = router_b.reshape(N_EXPERTS, 1)

    for l in range(depth):
        lam_init = 0.8 - 0.6 * math.exp(-0.3 * l)
        modt = mods[l].reshape(SUBLANES, 6, d)[tile_row]
        qk, va, g, z, ab, gate = _inproj(x_all, modt, _pack_w_in(w_in[l]), cs_tab, n_lat_tiles, tiles_per_seq)

        lamv = _pad_rows(_pad_lanes(jnp.stack([lam_q1[l], lam_k1[l], lam_q2[l], lam_k2[l]])), SUBLANES)
        ya = _attention(qk, va, lamv, subln_g[l].reshape(1, LANES), lam_init=lam_init, b=b, s=s, nc=nc)

        gparams = _pad_rows(_pad_lanes(jnp.stack([a_log[l].reshape(-1), dt_bias[l].reshape(-1)])), SUBLANES)
        qg, kg, vg, gc = _gdn_prep(g, ab, _pad_rows(conv_w[l], SUBLANES), gparams, n_lat_tiles, tiles_per_seq)
        o_f, o_b = _gdn(qg, kg, vg, gc, b=b, s=s, nc=nc)

        x_all, ht, cls, ew = _merge(x_all, ya, o_f, o_b, z, gate, modt, w_pa[l].astype(BF16), w_pb[l].astype(BF16),
                                    w_o[l].astype(BF16), onorm_g[l].reshape(1, LANES), ln1_g[l].reshape(1, d),
                                    ln1_b[l].reshape(1, d), rw_t, rb, alpha=alpha)

        slot_src, slot_dst, w_slot, block_ea, block_eb, n_used = _dispatch(cls[0], ew[:2].T, t)
        yt = _experts(ht, w_slot, block_ea, block_eb, n_used, slot_src, slot_dst, w_exp1[l].astype(BF16),
                      w_exp3[l].astype(BF16), w_exp2[l].astype(BF16), t)
        x_all = _ln2(x_all, yt, modt, ln2_g[l].reshape(1, d), ln2_b[l].reshape(1, d), alpha=alpha)
    return x_all[:t_lat].reshape(b, s, d)
```

```python
import functools
import math

import numpy as np

import jax
import jax.numpy as jnp
from jax import lax
from jax.experimental import pallas as pl
from jax.experimental.pallas import tpu as pltpu

F32 = jnp.float32
BF16 = jnp.bfloat16

GRID_W = 64
A_HEADS = 4
A_HEAD_DIM = 64
A_WIDTH = A_HEADS * 2 * A_HEAD_DIM
G_HEADS = 4
G_KEY_DIM = 128
G_VAL_DIM = 128
G_QK_WIDTH = G_HEADS * G_KEY_DIM
G_V_WIDTH = G_HEADS * G_VAL_DIM
G_QKV_WIDTH = 2 * G_QK_WIDTH + G_V_WIDTH
CONV_K = 5
CHUNK = 64
N_EXPERTS = 16
N_GROUPS = 4
EXPERTS_PER_GROUP = N_EXPERTS // N_GROUPS
MOE_BLOCK = 256
ROPE_BASE = 10000.0
EPS = 1e-6

LANES = 128
SUBLANES = 8
TM = 256
DMA_GROUP = 8
ATTN_KEY_CHUNK = 256
VMEM_LIMIT = 56 * 1024 * 1024

OFF_QK = 0
OFF_VA = 2 * A_WIDTH
OFF_G = OFF_VA + A_WIDTH
OFF_Z = OFF_G + G_QKV_WIDTH
OFF_AB = OFF_Z + G_V_WIDTH
OFF_GATE = OFF_AB + LANES
GC_CUM, GC_BETA, GC_EG, GC_EK, GC_ET = 0, 8, 16, 24, 32

PAIRS = tuple((lo, hi) for lo in range(EXPERTS_PER_GROUP) for hi in range(lo + 1, EXPERTS_PER_GROUP))
N_CLASSES = N_GROUPS * len(PAIRS)
CLASS_EA = np.array([g * EXPERTS_PER_GROUP + lo for g in range(N_GROUPS) for lo, _ in PAIRS], np.int32)
CLASS_EB = np.array([g * EXPERTS_PER_GROUP + hi for g in range(N_GROUPS) for _, hi in PAIRS], np.int32)


def _sigmoid(x):
    return 1.0 / (1.0 + jnp.exp(-x))


def _silu(x):
    return x * _sigmoid(x)


def _dot(a, b):
    return jnp.dot(a, b, preferred_element_type=F32)


def _dot_nt(a, b):
    return lax.dot_general(a, b, (((1,), (1,)), ((), ())), preferred_element_type=F32)


def _dot_f32(a, b):
    return jnp.dot(a, b, preferred_element_type=F32, precision=lax.Precision.HIGHEST)


def _params(*sem):
    return pltpu.CompilerParams(dimension_semantics=sem, vmem_limit_bytes=VMEM_LIMIT)


def _to_token_tiles(ref, x):
    rows, width = x.shape
    r = width // LANES
    for c in range(r):
        ref[pl.ds(c, rows, stride=r), :] = x[:, c * LANES:(c + 1) * LANES]


def _from_token_tiles(ref, rows, r):
    return jnp.concatenate([ref[pl.ds(c, rows, stride=r), :] for c in range(r)], axis=-1)


def _mod_kernel(cc_ref, w_ref, b_ref, o_ref):
    h = _silu(cc_ref[...])
    o_ref[...] = _dot(h.astype(BF16), w_ref[...].astype(BF16)) + b_ref[...]


def _mod_tables(cc, mod_w, mod_b):
    depth, d, n = mod_w.shape
    tn = n // 6
    return pl.pallas_call(
        _mod_kernel,
        grid=(depth, n // tn),
        in_specs=[pl.BlockSpec((SUBLANES, d), lambda l, j: (0, 0)),
                  pl.BlockSpec((None, d, tn), lambda l, j: (l, 0, j)),
                  pl.BlockSpec((None, 1, tn), lambda l, j: (l, 0, j))],
        out_specs=pl.BlockSpec((None, SUBLANES, tn), lambda l, j: (l, 0, j)),
        out_shape=jax.ShapeDtypeStruct((depth, SUBLANES, n), F32),
        compiler_params=_params("parallel", "parallel"),
    )(cc, mod_w, mod_b.reshape(depth, 1, n))


def _inproj_kernel(x_ref, mod_ref, w_ref, cs_ref, qk_ref, va_ref, g_ref, z_ref, ab_ref, gate_ref):
    m = mod_ref[...]
    h = (x_ref[...] * (1.0 + m[1:2]) + m[0:1]).astype(BF16)
    cs = cs_ref[...]
    cos, sin = cs[:, :LANES], cs[:, LANES:]
    qk = _dot(h, w_ref[:, OFF_QK:OFF_VA])
    for j in range(2 * A_HEADS):
        a = qk[:, j * LANES:(j + 1) * LANES]
        qk_ref[:, j * LANES:(j + 1) * LANES] = (a * cos + pltpu.roll(a, LANES // 2, 1) * sin).astype(BF16)
    va = _dot(h, w_ref[:, OFF_VA:OFF_G])
    for j in range(A_HEADS):
        va_ref[j * LANES:(j + 1) * LANES, :] = va[:, j * LANES:(j + 1) * LANES].T.astype(BF16)
    g_ref[...] = _dot(h, w_ref[:, OFF_G:OFF_Z])
    z_ref[...] = _dot(h, w_ref[:, OFF_Z:OFF_AB])
    ab_ref[...] = _dot(h, w_ref[:, OFF_AB:OFF_GATE])
    gate_ref[...] = _sigmoid(_dot(h, w_ref[:, OFF_GATE:]))


def _inproj(x_all, modt, w_cat, cs_tab, n_lat_tiles, tiles_per_seq):
    t, d = x_all.shape
    n = w_cat.shape[1]
    d_gate = n - OFF_GATE

    def row(i):
        return (i, 0)

    def cs_map(i):
        return (jnp.where(i < n_lat_tiles, i % tiles_per_seq, tiles_per_seq), 0)

    out_shape = [jax.ShapeDtypeStruct((t, 2 * A_WIDTH), BF16),
                 jax.ShapeDtypeStruct((A_WIDTH, t), BF16),
                 jax.ShapeDtypeStruct((t, G_QKV_WIDTH), F32),
                 jax.ShapeDtypeStruct((t, G_V_WIDTH), F32),
                 jax.ShapeDtypeStruct((t, LANES), F32),
                 jax.ShapeDtypeStruct((t, d_gate), F32)]
    out_specs = [pl.BlockSpec((TM, s.shape[1]), row) for s in out_shape]
    out_specs[1] = pl.BlockSpec((A_WIDTH, TM), lambda i: (0, i))
    return pl.pallas_call(
        _inproj_kernel,
        grid=(t // TM,),
        in_specs=[pl.BlockSpec((TM, d), row),
                  pl.BlockSpec((None, 6, d), lambda i: (i, 0, 0)),
                  pl.BlockSpec((d, n), lambda i: (0, 0)),
                  pl.BlockSpec((TM, 2 * LANES), cs_map)],
        out_specs=out_specs,
        out_shape=out_shape,
        compiler_params=_params("parallel"),
    )(x_all, modt, w_cat, cs_tab)


def _row_fold(x, op):
    rows, n = x.shape
    return op(x.reshape(rows // SUBLANES, SUBLANES, n), axis=0)


def _attn_body(lam_ref, q_ref, kv_refs, g_ref, o_ref, s_refs, lam_init):
    lv = lam_ref[...]
    lam = (jnp.exp(jnp.sum(lv[0:1] * lv[1:2], axis=-1, keepdims=True))
           - jnp.exp(jnp.sum(lv[2:3] * lv[3:4], axis=-1, keepdims=True)) + lam_init)
    q = q_ref[...]
    lane = lax.broadcasted_iota(jnp.int32, (1, LANES), 1)
    comp1 = ((lane // (A_HEAD_DIM // 2)) % 2) == 1
    qn = (jnp.where(comp1, jnp.zeros_like(q), q), jnp.where(comp1, q, jnp.zeros_like(q)))
    m_part, off = [None, None], 0
    for k_ref, _ in kv_refs:
        nk = k_ref.shape[0]
        for n in range(2):
            st = _dot_nt(k_ref[...], qn[n])
            s_refs[n][pl.ds(off, nk), :] = st
            mc = _row_fold(st, jnp.max)
            m_part[n] = mc if m_part[n] is None else jnp.maximum(m_part[n], mc)
        off += nk
    m = [jnp.max(mp, axis=0, keepdims=True) for mp in m_part]
    l_part, ot, off = [None, None], [None, None], 0
    for k_ref, vt_ref in kv_refs:
        for r0 in range(0, k_ref.shape[0], ATTN_KEY_CHUNK):
            for n in range(2):
                p = jnp.exp(s_refs[n][pl.ds(off + r0, ATTN_KEY_CHUNK), :] - m[n])
                lc = _row_fold(p, jnp.sum)
                l_part[n] = lc if l_part[n] is None else l_part[n] + lc
                oc = _dot(vt_ref[:, pl.ds(r0, ATTN_KEY_CHUNK)], p.astype(BF16))
                ot[n] = oc if ot[n] is None else ot[n] + oc
        off += k_ref.shape[0]
    outs = [ot[n] / jnp.sum(l_part[n], axis=0, keepdims=True) for n in range(2)]
    ot = outs[0] - lam * outs[1]
    r = ot * lax.rsqrt(jnp.mean(ot * ot, axis=0, keepdims=True) + EPS)
    o_ref[...] = (r * g_ref[...] * (1.0 - lam_init)).T.astype(BF16)


def _attn_kernel(lam_ref, q_ref, kc_ref, vc_ref, kl_ref, vl_ref, g_ref, o_ref, s0_ref, s1_ref, *, lam_init, n_lat_q):
    is_lat = pl.program_id(2) < n_lat_q
    s_refs = (s0_ref, s1_ref)

    @pl.when(is_lat)
    def _():
        _attn_body(lam_ref, q_ref, ((kc_ref, vc_ref), (kl_ref, vl_ref)), g_ref, o_ref, s_refs, lam_init)

    @pl.when(jnp.logical_not(is_lat))
    def _():
        _attn_body(lam_ref, q_ref, ((kc_ref, vc_ref),), g_ref, o_ref, s_refs, lam_init)


def _attention(qk, vt, lamv, subln_g, *, lam_init, b, s, nc):
    tq = TM
    t = qk.shape[0]
    n_lat_q, n_ctx_q = s // tq, nc // tq
    ctx0 = (b * s) // nc

    def q_map(bi, h, i):
        return (jnp.where(i < n_lat_q, bi * n_lat_q + i, (b * s) // tq + bi * n_ctx_q + (i - n_lat_q)), h)

    return pl.pallas_call(
        functools.partial(_attn_kernel, lam_init=lam_init, n_lat_q=n_lat_q),
        grid=(b, A_HEADS, n_lat_q + n_ctx_q),
        in_specs=[pl.BlockSpec((SUBLANES, LANES), lambda bi, h, i: (0, 0)),
                  pl.BlockSpec((tq, LANES), q_map),
                  pl.BlockSpec((nc, LANES), lambda bi, h, i: (ctx0 + bi, A_HEADS + h)),
                  pl.BlockSpec((LANES, nc), lambda bi, h, i: (h, ctx0 + bi)),
                  pl.BlockSpec((s, LANES), lambda bi, h, i: (bi, A_HEADS + h)),
                  pl.BlockSpec((LANES, s), lambda bi, h, i: (h, bi)),
                  pl.BlockSpec((LANES, 1), lambda bi, h, i: (0, 0))],
        out_specs=pl.BlockSpec((tq, LANES), q_map),
        out_shape=jax.ShapeDtypeStruct((t, A_WIDTH), BF16),
        scratch_shapes=[pltpu.VMEM((nc + s, tq), F32)] * 2,
        compiler_params=_params("parallel", "parallel", "arbitrary"),
    )(lamv, qk, qk, vt, qk, vt, subln_g)


def _gdn_prep_kernel(g_ref, prev_ref, next_ref, ab_ref, cw_ref, gp_ref, q_ref, k_ref, v_ref, gc_ref, ext_ref,
                     *, n_lat_tiles, tiles_per_seq):
    i = pl.program_id(0)
    is_lat = i < n_lat_tiles
    first = jnp.logical_or(jnp.logical_not(is_lat), i % tiles_per_seq == 0)
    last = jnp.logical_or(jnp.logical_not(is_lat), i % tiles_per_seq == tiles_per_seq - 1)
    ext_ref[0:SUBLANES, :] = jnp.where(first, 0.0, prev_ref[...])
    ext_ref[SUBLANES:SUBLANES + TM, :] = g_ref[...]
    ext_ref[SUBLANES + TM:, :] = jnp.where(last, 0.0, next_ref[...])
    cw = cw_ref[...]
    acc = None
    for j in range(CONV_K):
        term = ext_ref[pl.ds(SUBLANES - CONV_K // 2 + j, TM), :] * cw[j:j + 1]
        acc = term if acc is None else acc + term
    y = _silu(acc)
    for h in range(G_HEADS):
        for base, ref, scale in ((0, q_ref, G_KEY_DIM ** -0.5), (G_QK_WIDTH, k_ref, 1.0)):
            xh = y[:, base + h * LANES:base + (h + 1) * LANES]
            nrm = xh * lax.rsqrt(jnp.sum(xh * xh, axis=-1, keepdims=True) + EPS)
            ref[:, h * LANES:(h + 1) * LANES] = nrm * scale
    v_ref[...] = y[:, 2 * G_QK_WIDTH:]

    ab = ab_ref[...]
    gp = gp_ref[...]
    lane = lax.broadcasted_iota(jnp.int32, (1, LANES), 1)
    z = ab + gp[1:2]
    softplus = jnp.maximum(z, 0.0) + jnp.log(1.0 + jnp.exp(-jnp.abs(z)))
    g = jnp.where(lane < 2 * G_HEADS, -jnp.exp(gp[0:1]) * softplus, 0.0)
    beta = _sigmoid(ab)
    ri = lax.broadcasted_iota(jnp.int32, (TM, TM), 0)
    ci = lax.broadcasted_iota(jnp.int32, (TM, TM), 1)
    same = (ri // CHUNK) == (ci // CHUNK)
    lower = jnp.where(jnp.logical_and(same, ci <= ri), 1.0, 0.0)
    upper = jnp.where(jnp.logical_and(same, ci >= ri), 1.0, 0.0)
    cum_f = _dot_f32(lower, g)
    cum_b = _dot_f32(upper, g)
    total = cum_f + cum_b - g
    cum = jnp.where(lane < G_HEADS, cum_f, cum_b)
    eg = jnp.exp(cum)
    ek = jnp.exp(total - cum)
    et = jnp.exp(total)
    out = jnp.where(lane < GC_BETA, cum, 0.0)
    out = jnp.where(jnp.logical_and(lane >= GC_BETA, lane < GC_EG), beta, out)
    out = jnp.where(jnp.logical_and(lane >= GC_EG, lane < GC_EK), pltpu.roll(eg, GC_EG, 1), out)
    out = jnp.where(jnp.logical_and(lane >= GC_EK, lane < GC_ET), pltpu.roll(ek, GC_EK, 1), out)
    out = jnp.where(jnp.logical_and(lane >= GC_ET, lane < GC_ET + 8), pltpu.roll(et, GC_ET, 1), out)
    gc_ref[...] = out


def _gdn_prep(g, ab, conv_w8, gparams, n_lat_tiles, tiles_per_seq):
    t, c = g.shape
    nblk8 = t // SUBLANES
    per = TM // SUBLANES
    row = lambda i: (i, 0)
    out_shape = [jax.ShapeDtypeStruct((t, G_QK_WIDTH), F32),
                 jax.ShapeDtypeStruct((t, G_QK_WIDTH), F32),
                 jax.ShapeDtypeStruct((t, G_V_WIDTH), F32),
                 jax.ShapeDtypeStruct((t, LANES), F32)]
    return pl.pallas_call(
        functools.partial(_gdn_prep_kernel, n_lat_tiles=n_lat_tiles, tiles_per_seq=tiles_per_seq),
        grid=(t // TM,),
        in_specs=[pl.BlockSpec((TM, c), row),
                  pl.BlockSpec((SUBLANES, c), lambda i: (jnp.maximum(i * per - 1, 0), 0)),
                  pl.BlockSpec((SUBLANES, c), lambda i: (jnp.minimum((i + 1) * per, nblk8 - 1), 0)),
                  pl.BlockSpec((TM, LANES), row),
                  pl.BlockSpec((SUBLANES, c), lambda i: (0, 0)),
                  pl.BlockSpec((SUBLANES, LANES), lambda i: (0, 0))],
        out_specs=[pl.BlockSpec((TM, s.shape[1]), row) for s in out_shape],
        out_shape=out_shape,
        scratch_shapes=[pltpu.VMEM((TM + 2 * SUBLANES, c), F32)],
        compiler_params=_params("parallel"),
    )(g, g, g, ab, conv_w8, gparams)


def _gdn_masks(c, reverse):
    ii = lax.broadcasted_iota(jnp.int32, (c, 2 * c), 0)
    lane = lax.broadcasted_iota(jnp.int32, (c, 2 * c), 1)
    left = lane < c
    jj = jnp.where(left, lane, lane - c)
    causal = (ii <= jj) if reverse else (ii >= jj)
    strict = (ii < jj) if reverse else (ii > jj)
    return dict(left=left, diag=ii == jj, causal=causal, strict=strict)


def _gdn_setup(q, k, v, gc, gct, col, mk):
    c = q.shape[0]
    gcol = gc[:, GC_CUM + col:GC_CUM + col + 1]
    beta = gc[:, GC_BETA + col:GC_BETA + col + 1]
    eg = gc[:, GC_EG + col:GC_EG + col + 1]
    ek = gc[:, GC_EK + col:GC_EK + col + 1]
    et = gc[0:1, GC_ET + col:GC_ET + col + 1]
    grow = gct[GC_CUM + col:GC_CUM + col + 1, :]
    decay = jnp.exp(jnp.where(mk["causal"], gcol - grow, -1e30))
    kb = k * beta
    k16 = k.astype(BF16)
    both = _dot_nt(jnp.concatenate([kb.astype(BF16), q.astype(BF16)], axis=0),
                   jnp.concatenate([k16, k16], axis=0))
    qk = jnp.where(jnp.logical_and(mk["causal"], mk["left"]), both[c:] * decay, 0.0)
    z = jnp.where(mk["left"], jnp.where(mk["diag"], 1.0, 0.0), jnp.where(mk["strict"], -(both[:c] * decay), 0.0))
    rhs = jnp.concatenate([v * beta, kb * eg], axis=1)
    return dict(z=z, qk=qk.astype(BF16), rhs=rhs, qd=(q * eg).astype(BF16), kdt=(k * ek).T.astype(BF16), et=et)


def _gdn_level(z, mk):
    z16 = z.astype(BF16)
    return jnp.where(mk["left"], z, 0.0) + _dot(z16, jnp.concatenate([jnp.zeros_like(z16), z16], axis=0))


def _gdn_solve(ch, mk):
    y16 = jnp.where(jnp.logical_and(mk["left"], jnp.logical_not(mk["diag"])), ch["z"], 0.0).astype(BF16)
    rhs16 = ch["rhs"].astype(BF16)
    sol = ch["rhs"] + _dot(y16, jnp.concatenate([rhs16, rhs16], axis=0))
    dv = sol.shape[1] // 2
    return sol[:, :dv], jnp.concatenate([sol[:, dv:].astype(BF16), ch["qd"]], axis=0)


def _gdn_step(s, u, wq, qk, kdt, et):
    c = u.shape[0]
    ws = _dot(wq, s.astype(BF16))
    v_new = (u - ws[:c]).astype(BF16)
    o = ws[c:] + _dot(qk, jnp.concatenate([v_new, v_new], axis=0))
    return s * et + _dot(kdt, v_new), o


def _gdn_kernel(qf_ref, kf_ref, vf_ref, gf_ref, qb_ref, kb_ref, vb_ref, gb_ref, of_ref, ob_ref, s_ref):
    @pl.when(pl.program_id(1) == 0)
    def _():
        s_ref[...] = jnp.zeros_like(s_ref)

    n_chunks = qf_ref.shape[0] // CHUNK
    masks = (_gdn_masks(CHUNK, False), _gdn_masks(CHUNK, True))
    dirs = ((qf_ref, kf_ref, vf_ref, gf_ref, of_ref), (qb_ref, kb_ref, vb_ref, gb_ref, ob_ref))
    chains = []
    for n in range(n_chunks):
        for d, (q_ref, k_ref, v_ref, g_ref, o_ref) in enumerate(dirs):
            rows = pl.ds((n_chunks - 1 - n if d else n) * CHUNK, CHUNK)
            gc = g_ref[rows, :]
            gct = jnp.concatenate([gc, gc], axis=0).T
            for h in range(G_HEADS):
                cols = pl.ds(h * LANES, LANES)
                ch = _gdn_setup(q_ref[rows, cols], k_ref[rows, cols], v_ref[rows, cols], gc, gct,
                                d * G_HEADS + h, masks[d])
                ch.update(d=d, h=h, rows=rows, cols=cols, out=o_ref)
                chains.append(ch)
    for _ in range(int(math.log2(CHUNK))):
        for ch in chains:
            ch["z"] = _gdn_level(ch["z"], masks[ch["d"]])
    for ch in chains:
        ch["u"], ch["wq"] = _gdn_solve(ch, masks[ch["d"]])
    state = [s_ref[i] for i in range(2 * G_HEADS)]
    for ch in chains:
        i = ch["d"] * G_HEADS + ch["h"]
        state[i], o = _gdn_step(state[i], ch["u"], ch["wq"], ch["qk"], ch["kdt"], ch["et"])
        ch["out"][ch["rows"], ch["cols"]] = o
    for i in range(2 * G_HEADS):
        s_ref[i] = state[i]


def _gdn(qg, kg, vg, gc, *, b, s, nc):
    t = qg.shape[0]
    tps = s // TM
    nct = nc // TM
    n_steps = nct + tps
    ctx_tile0 = (b * s) // TM

    def fwd(bi, i):
        return (jnp.where(i < nct, ctx_tile0 + bi * nct + i, bi * tps + (i - nct)), 0)

    def bwd(bi, i):
        return (jnp.where(i < nct, ctx_tile0 + bi * nct + (nct - 1 - i), bi * tps + (tps - 1 - (i - nct))), 0)

    def specs(row_map):
        return [pl.BlockSpec((TM, G_QK_WIDTH), row_map), pl.BlockSpec((TM, G_QK_WIDTH), row_map),
                pl.BlockSpec((TM, G_V_WIDTH), row_map), pl.BlockSpec((TM, LANES), row_map)]

    return pl.pallas_call(
        _gdn_kernel,
        grid=(b, n_steps),
        in_specs=specs(fwd) + specs(bwd),
        out_specs=[pl.BlockSpec((TM, G_V_WIDTH), fwd), pl.BlockSpec((TM, G_V_WIDTH), bwd)],
        out_shape=[jax.ShapeDtypeStruct((t, G_V_WIDTH), F32)] * 2,
        scratch_shapes=[pltpu.VMEM((2 * G_HEADS, G_KEY_DIM, G_VAL_DIM), F32)],
        compiler_params=_params("parallel", "arbitrary"),
    )(qg, kg, vg, gc, qg, kg, vg, gc)


def _layer_norm(r, g, b):
    mu = jnp.mean(r, axis=-1, keepdims=True)
    rc = r - mu
    var = jnp.mean(rc * rc, axis=-1, keepdims=True)
    return rc * lax.rsqrt(var + EPS) * g + b


def _merge_kernel(x_ref, ya_ref, of_ref, ob_ref, z_ref, gate_ref, mod_ref, wpa_ref, wpb_ref, wo_ref, on_ref,
                  lng_ref, lnb_ref, rw_ref, rb_ref, xo_ref, ht_ref, cls_ref, ew_ref, *, alpha):
    d = x_ref.shape[1]
    o = of_ref[...] + ob_ref[...]
    parts = []
    for h in range(G_HEADS):
        oh = o[:, h * LANES:(h + 1) * LANES]
        parts.append(oh * lax.rsqrt(jnp.mean(oh * oh, axis=-1, keepdims=True) + EPS) * on_ref[...])
    yb = jnp.concatenate(parts, axis=1) * _silu(z_ref[...])
    pa = _dot(ya_ref[...], wpa_ref[...])
    pb = _dot(yb.astype(BF16), wpb_ref[...])
    mix = gate_ref[:, :d] * pa + gate_ref[:, d:] * pb
    y = _dot(mix.astype(BF16), wo_ref[...])
    m = mod_ref[...]
    xn = _layer_norm(alpha * x_ref[...] + m[2:3] * y, lng_ref[...], lnb_ref[...])
    xo_ref[...] = xn
    h2 = xn * (1.0 + m[4:5]) + m[3:4]
    _to_token_tiles(ht_ref, h2)

    score = _sigmoid(_dot_nt(rw_ref[...], h2.astype(BF16)))
    sel = score + rb_ref[...]
    rows = [sel[e:e + 1] for e in range(N_EXPERTS)]
    best = None
    best_val = None
    for gi in range(N_GROUPS):
        a, b, c, dd = rows[gi * EXPERTS_PER_GROUP:(gi + 1) * EXPERTS_PER_GROUP]
        top2 = jnp.maximum(jnp.maximum(jnp.maximum(a + b, a + c), jnp.maximum(a + dd, b + c)),
                           jnp.maximum(b + dd, c + dd))
        if gi == 0:
            best, best_val = jnp.zeros(top2.shape, jnp.int32), top2
        else:
            upd = top2 > best_val
            best = jnp.where(upd, gi, best)
            best_val = jnp.where(upd, top2, best_val)
    in_group, in_score = [], []
    for j in range(EXPERTS_PER_GROUP):
        v = rows[(N_GROUPS - 1) * EXPERTS_PER_GROUP + j]
        sv = score[(N_GROUPS - 1) * EXPERTS_PER_GROUP + j:(N_GROUPS - 1) * EXPERTS_PER_GROUP + j + 1]
        for gi in range(N_GROUPS - 2, -1, -1):
            e = gi * EXPERTS_PER_GROUP + j
            v = jnp.where(best == gi, rows[e], v)
            sv = jnp.where(best == gi, score[e:e + 1], sv)
        in_group.append(v)
        in_score.append(sv)
    l0 = jnp.zeros(best.shape, jnp.int32)
    m0 = in_group[0]
    for j in range(1, EXPERTS_PER_GROUP):
        upd = in_group[j] > m0
        l0 = jnp.where(upd, j, l0)
        m0 = jnp.where(upd, in_group[j], m0)
    l1 = jnp.zeros(best.shape, jnp.int32)
    m1 = jnp.full(m0.shape, -jnp.inf, F32)
    for j in range(EXPERTS_PER_GROUP):
        upd = jnp.logical_and(l0 != j, in_group[j] > m1)
        l1 = jnp.where(upd, j, l1)
        m1 = jnp.where(upd, in_group[j], m1)
    lo = jnp.minimum(l0, l1)
    hi = jnp.maximum(l0, l1)
    w_lo = jnp.zeros(m0.shape, F32)
    w_hi = jnp.zeros(m0.shape, F32)
    for j in range(EXPERTS_PER_GROUP):
        w_lo = jnp.where(lo == j, in_score[j], w_lo)
        w_hi = jnp.where(hi == j, in_score[j], w_hi)
    pair = jnp.where(lo == 0, hi - 1, jnp.where(lo == 1, hi + 1, len(PAIRS) - 1))
    tot = w_lo + w_hi
    ri = lax.broadcasted_iota(jnp.int32, cls_ref.shape, 0)
    cls_ref[...] = jnp.where(ri == 0, best * len(PAIRS) + pair, 0)
    ew_ref[...] = jnp.where(ri == 0, w_lo / tot, jnp.where(ri == 1, w_hi / tot, 0.0))


def _merge(x_all, ya, o_f, o_b, z, gate, modt, wpa, wpb, wo, onorm, lng, lnb, rw_t, rb, *, alpha):
    t, d = x_all.shape
    r = d // LANES
    row = lambda i: (i, 0)
    col = lambda i: (0, i)
    const = lambda i: (0, 0)
    out_shape = [jax.ShapeDtypeStruct((t, d), F32),
                 jax.ShapeDtypeStruct((t * r, LANES), F32),
                 jax.ShapeDtypeStruct((SUBLANES, t), jnp.int32),
                 jax.ShapeDtypeStruct((SUBLANES, t), F32)]
    return pl.pallas_call(
        functools.partial(_merge_kernel, alpha=alpha),
        grid=(t // TM,),
        in_specs=[pl.BlockSpec((TM, d), row),
                  pl.BlockSpec((TM, A_WIDTH), row),
                  pl.BlockSpec((TM, G_V_WIDTH), row),
                  pl.BlockSpec((TM, G_V_WIDTH), row),
                  pl.BlockSpec((TM, G_V_WIDTH), row),
                  pl.BlockSpec((TM, 2 * d), row),
                  pl.BlockSpec((None, 6, d), lambda i: (i, 0, 0)),
                  pl.BlockSpec(wpa.shape, const),
                  pl.BlockSpec(wpb.shape, const),
                  pl.BlockSpec(wo.shape, const),
                  pl.BlockSpec((1, LANES), const),
                  pl.BlockSpec((1, d), const),
                  pl.BlockSpec((1, d), const),
                  pl.BlockSpec(rw_t.shape, const),
                  pl.BlockSpec(rb.shape, const)],
        out_specs=[pl.BlockSpec((TM, d), row), pl.BlockSpec((TM * r, LANES), row),
                   pl.BlockSpec((SUBLANES, TM), col), pl.BlockSpec((SUBLANES, TM), col)],
        out_shape=out_shape,
        compiler_params=_params("parallel"),
    )(x_all, ya, o_f, o_b, z, gate, modt, wpa, wpb, wo, onorm, lng, lnb, rw_t, rb)


def _expert_kernel(ea_ref, eb_ref, nu_ref, nv_ref, src_ref, h_hbm, ws_ref, w1a_ref, w3a_ref, w2a_ref,
                   w1b_ref, w3b_ref, w2b_ref, y_hbm, xbuf, ybuf, gsem, ssem, *, r):
    i = pl.program_id(0)
    n_used = nu_ref[0]
    slot = i % 2

    def gather_copy(blk, buf, j):
        tok = src_ref[blk * MOE_BLOCK + j]
        return pltpu.make_async_copy(h_hbm.at[pl.ds(pl.multiple_of(tok * r, r), r), :],
                                     xbuf.at[buf, pl.ds(pl.multiple_of(j * r, r), r), :], gsem.at[buf])

    def scatter_copy(blk, buf, j):
        tok = src_ref[blk * MOE_BLOCK + j]
        return pltpu.make_async_copy(ybuf.at[buf, pl.ds(pl.multiple_of(j * r, r), r), :],
                                     y_hbm.at[pl.ds(pl.multiple_of(tok * r, r), r), :], ssem.at[buf])

    def for_slots(fn):
        def body(j, carry):
            fn(j)
            return carry
        lax.fori_loop(0, MOE_BLOCK, body, 0, unroll=DMA_GROUP)

    def for_token_slots(blk, fn):
        nv = nv_ref[blk]

        def body(g, carry):
            j0 = g * DMA_GROUP

            @pl.when(j0 + DMA_GROUP <= nv)
            def _():
                for k in range(DMA_GROUP):
                    fn(j0 + k)

            @pl.when(jnp.logical_and(j0 < nv, j0 + DMA_GROUP > nv))
            def _():
                for k in range(DMA_GROUP):
                    pl.when(j0 + k < nv)(functools.partial(fn, j0 + k))
            return carry
        lax.fori_loop(0, MOE_BLOCK // DMA_GROUP, body, 0)

    @pl.when(i == 0)
    def _():
        for_slots(lambda j: gather_copy(0, 0, j).start())

    @pl.when(i < n_used)
    def _():
        for_slots(lambda j: gather_copy(i, slot, j).wait())

        @pl.when(i + 1 < n_used)
        def _():
            for_slots(lambda j: gather_copy(i + 1, 1 - slot, j).start())

        x = _from_token_tiles(xbuf.at[slot], MOE_BLOCK, r).astype(BF16)
        ws = ws_ref[...]

        def ffn(w1_ref, w3_ref, w2_ref):
            hid = _silu(_dot(x, w1_ref[...])) * _dot(x, w3_ref[...])
            return _dot(hid.astype(BF16), w2_ref[...])

        y = ws[:, 0:1] * ffn(w1a_ref, w3a_ref, w2a_ref) + ws[:, 1:2] * ffn(w1b_ref, w3b_ref, w2b_ref)

        _to_token_tiles(ybuf.at[slot], y)

        @pl.when(i >= 1)
        def _():
            for_token_slots(i - 1, lambda j: scatter_copy(i - 1, 1 - slot, j).wait())

        for_token_slots(i, lambda j: scatter_copy(i, slot, j).start())

        @pl.when(i == n_used - 1)
        def _():
            for_token_slots(i, lambda j: scatter_copy(i, slot, j).wait())


def _experts(ht, w_slot, block_ea, block_eb, n_used, n_valid, slot_src, w1, w3, w2, n_out_tokens):
    d, f = w1.shape[1], w1.shape[2]
    r = d // LANES
    n_slots = slot_src.shape[0]

    def wspec(shape, which):
        if which == 0:
            return pl.BlockSpec((None,) + shape, lambda i, ea, eb, nu, nv, src: (ea[i], 0, 0))
        return pl.BlockSpec((None,) + shape, lambda i, ea, eb, nu, nv, src: (eb[i], 0, 0))

    grid_spec = pltpu.PrefetchScalarGridSpec(
        num_scalar_prefetch=5,
        grid=(n_slots // MOE_BLOCK,),
        in_specs=[pl.BlockSpec(memory_space=pl.ANY),
                  pl.BlockSpec((MOE_BLOCK, 2), lambda i, ea, eb, nu, nv, src: (i, 0)),
                  wspec((d, f), 0), wspec((d, f), 0), wspec((f, d), 0),
                  wspec((d, f), 1), wspec((d, f), 1), wspec((f, d), 1)],
        out_specs=pl.BlockSpec(memory_space=pl.ANY),
        scratch_shapes=[pltpu.VMEM((2, MOE_BLOCK * r, LANES), F32),
                        pltpu.VMEM((2, MOE_BLOCK * r, LANES), F32),
                        pltpu.SemaphoreType.DMA((2,)),
                        pltpu.SemaphoreType.DMA((2,))],
    )
    return pl.pallas_call(
        functools.partial(_expert_kernel, r=r),
        grid_spec=grid_spec,
        out_shape=jax.ShapeDtypeStruct((n_out_tokens * r, LANES), F32),
        compiler_params=_params("arbitrary"),
    )(block_ea, block_eb, n_used, n_valid, slot_src, ht, w_slot, w1, w3, w2, w1, w3, w2)


def _ln2_kernel(x_ref, yt_ref, mod_ref, g_ref, b_ref, o_ref, *, alpha):
    tm, d = x_ref.shape
    y = _from_token_tiles(yt_ref, tm, d // LANES)
    m = mod_ref[...]
    o_ref[...] = _layer_norm(alpha * x_ref[...] + m[5:6] * y, g_ref[...], b_ref[...])


def _ln2(x_all, yt, modt, g, b, *, alpha):
    t, d = x_all.shape
    r = d // LANES
    row = lambda i: (i, 0)
    const = lambda i: (0, 0)
    return pl.pallas_call(
        functools.partial(_ln2_kernel, alpha=alpha),
        grid=(t // TM,),
        in_specs=[pl.BlockSpec((TM, d), row), pl.BlockSpec((TM * r, LANES), row),
                  pl.BlockSpec((None, 6, d), lambda i: (i, 0, 0)),
                  pl.BlockSpec((1, d), const), pl.BlockSpec((1, d), const)],
        out_specs=pl.BlockSpec((TM, d), row),
        out_shape=jax.ShapeDtypeStruct((t, d), F32),
        compiler_params=_params("parallel"),
    )(x_all, yt, modt, g, b)


def _rope_table(s, nc):
    n_freq = A_HEAD_DIM // 4
    rows = s // GRID_W
    row = jnp.repeat(jnp.arange(rows, dtype=F32), GRID_W)
    col = jnp.tile(jnp.arange(GRID_W, dtype=F32), rows)
    inv = ROPE_BASE ** (-jnp.arange(n_freq, dtype=F32) / n_freq)
    ang = jnp.stack([row[:, None] * inv, col[:, None] * inv], axis=1)
    cos = jnp.cos(ang).reshape(s, 1, 1, 2, n_freq)
    sin = jnp.sin(ang).reshape(s, 1, 1, 2, n_freq)
    cos = jnp.broadcast_to(cos, (s, 2, 2, 2, n_freq)).reshape(s, LANES)
    sign = jnp.array([-1.0, 1.0], F32).reshape(1, 2, 1, 1, 1)
    sin = jnp.broadcast_to(sin * sign, (s, 2, 2, 2, n_freq)).reshape(s, LANES)
    lat = jnp.concatenate([cos, sin], axis=1)
    ident = jnp.concatenate([jnp.ones((nc, LANES), F32), jnp.zeros((nc, LANES), F32)], axis=1)
    return jnp.concatenate([lat, ident], axis=0)


def _rope_perm(w):
    d = w.shape[0]
    n_freq = A_HEAD_DIM // 4
    return w.reshape(d, A_HEADS, 2, 2, 2, n_freq).transpose(0, 1, 4, 2, 3, 5).reshape(d, A_WIDTH)


def _pack_w_in(w):
    d = w.shape[0]
    sizes = (A_WIDTH, A_WIDTH, A_WIDTH, G_QKV_WIDTH, G_V_WIDTH, 2 * G_HEADS, 2 * G_HEADS)
    offs = [0]
    for sz in sizes:
        offs.append(offs[-1] + sz)
    wq, wk, wv, wg, wz, wa, wb = (w[:, offs[i]:offs[i + 1]] for i in range(len(sizes)))
    wgate = w[:, offs[-1]:]
    wab = jnp.concatenate([wa, wb, jnp.zeros((d, LANES - 4 * G_HEADS), w.dtype)], axis=1)
    cat = jnp.concatenate([_rope_perm(wq) * (A_HEAD_DIM ** -0.5), _rope_perm(wk), wv, wg, wz, wab, wgate], axis=1)
    return cat.astype(BF16)


def _pad_rows(a, rows):
    return jnp.concatenate([a, jnp.zeros((rows - a.shape[0],) + a.shape[1:], a.dtype)], axis=0)


def _pad_lanes(a):
    return jnp.concatenate([a, jnp.zeros(a.shape[:-1] + (LANES - a.shape[-1],), a.dtype)], axis=-1)


def _dispatch(cls, ew, t):
    onehot = (cls[:, None] == jnp.arange(N_CLASSES, dtype=jnp.int32)[None, :]).astype(jnp.int32)
    csum = jnp.cumsum(onehot, axis=0)
    rank = jnp.sum(csum * onehot, axis=1) - 1
    counts = csum[-1]
    padded = (counts + MOE_BLOCK - 1) // MOE_BLOCK * MOE_BLOCK
    pad_end = jnp.cumsum(padded)
    pad_start = pad_end - padded
    dest = pad_start[cls] + rank
    n_blocks = -(-t // MOE_BLOCK) + N_CLASSES
    per_tok = jnp.concatenate([jnp.arange(t, dtype=F32)[:, None], ew], axis=1)
    per_slot = jnp.zeros((n_blocks * MOE_BLOCK, per_tok.shape[1]), F32).at[dest].set(per_tok)
    slot_src = per_slot[:, 0].astype(jnp.int32)
    w_slot = per_slot[:, 1:]
    block0 = jnp.arange(n_blocks, dtype=jnp.int32) * MOE_BLOCK
    block_cls = jnp.minimum(jnp.searchsorted(pad_end, block0, side='right'), N_CLASSES - 1)
    n_valid = jnp.clip((pad_start + counts)[block_cls] - block0, 0, MOE_BLOCK).astype(jnp.int32)
    n_used = (pad_end[-1:] // MOE_BLOCK).astype(jnp.int32)
    return slot_src, w_slot, jnp.asarray(CLASS_EA)[block_cls], jnp.asarray(CLASS_EB)[block_cls], n_used, n_valid


def kernel(x, c, ctx, c_ctx, mod_w, mod_b, w_in, conv_w, lam_q1, lam_k1, lam_q2, lam_k2, subln_g, a_log, dt_bias,
           onorm_g, w_pa, w_pb, w_o, ln1_g, ln1_b, router_w, router_b, w_exp1, w_exp3, w_exp2, ln2_g, ln2_b):
    b, s, d = x.shape
    nc = ctx.shape[1]
    depth = mod_w.shape[0]
    assert s % TM == 0 and nc % TM == 0 and s % nc == 0 and s % GRID_W == 0
    assert d == SUBLANES * LANES
    t_lat, t_ctx = b * s, b * nc
    t = t_lat + t_ctx
    n_lat_tiles = t_lat // TM
    tiles_per_seq = s // TM
    alpha = (2.0 * depth) ** 0.25

    x_all = jnp.concatenate([x.reshape(t_lat, d), ctx.reshape(t_ctx, d)], axis=0)
    cc = _pad_rows(jnp.concatenate([c, c_ctx[None, :]], axis=0), SUBLANES)
    mods = _mod_tables(cc, mod_w, mod_b)
    tile_row = jnp.concatenate([jnp.arange(n_lat_tiles, dtype=jnp.int32) // tiles_per_seq,
                                jnp.full((t_ctx // TM,), b, jnp.int32)])
    cs_tab = _rope_table(s, nc)
    rw_t = router_w.T.astype(BF16)
    rb = router_b.reshape(N_EXPERTS, 1)

    for l in range(depth):
        lam_init = 0.8 - 0.6 * math.exp(-0.3 * l)
        modt = mods[l].reshape(SUBLANES, 6, d)[tile_row]
        qk, vt, g, z, ab, gate = _inproj(x_all, modt, _pack_w_in(w_in[l]), cs_tab, n_lat_tiles, tiles_per_seq)

        lamv = _pad_rows(_pad_lanes(jnp.stack([lam_q1[l], lam_k1[l], lam_q2[l], lam_k2[l]])), SUBLANES)
        ya = _attention(qk, vt, lamv, subln_g[l].reshape(LANES, 1), lam_init=lam_init, b=b, s=s, nc=nc)

        gparams = _pad_rows(_pad_lanes(jnp.stack([a_log[l].reshape(-1), dt_bias[l].reshape(-1)])), SUBLANES)
        qg, kg, vg, gc = _gdn_prep(g, ab, _pad_rows(conv_w[l], SUBLANES), gparams, n_lat_tiles, tiles_per_seq)
        o_f, o_b = _gdn(qg, kg, vg, gc, b=b, s=s, nc=nc)

        x_all, ht, cls, ew = _merge(x_all, ya, o_f, o_b, z, gate, modt, w_pa[l].astype(BF16), w_pb[l].astype(BF16),
                                    w_o[l].astype(BF16), onorm_g[l].reshape(1, LANES), ln1_g[l].reshape(1, d),
                                    ln1_b[l].reshape(1, d), rw_t, rb, alpha=alpha)

        slot_src, w_slot, block_ea, block_eb, n_used, n_valid = _dispatch(cls[0], ew[:2].T, t)
        yt = _experts(ht, w_slot, block_ea, block_eb, n_used, n_valid, slot_src, w_exp1[l].astype(BF16),
                      w_exp3[l].astype(BF16), w_exp2[l].astype(BF16), t)
        x_all = _ln2(x_all, yt, modt, ln2_g[l].reshape(1, d), ln2_b[l].reshape(1, d), alpha=alpha)
    return x_all[:t_lat].reshape(b, s, d)
```

```python
import functools
import math

import numpy as np

import jax
import jax.numpy as jnp
from jax import lax
from jax.experimental import pallas as pl
from jax.experimental.pallas import tpu as pltpu

F32 = jnp.float32
BF16 = jnp.bfloat16

GRID_W = 64
A_HEADS = 4
A_HEAD_DIM = 64
A_WIDTH = A_HEADS * 2 * A_HEAD_DIM
G_HEADS = 4
G_KEY_DIM = 128
G_VAL_DIM = 128
G_QK_WIDTH = G_HEADS * G_KEY_DIM
G_V_WIDTH = G_HEADS * G_VAL_DIM
G_QKV_WIDTH = 2 * G_QK_WIDTH + G_V_WIDTH
CONV_K = 5
CHUNK = 64
N_EXPERTS = 16
N_GROUPS = 4
EXPERTS_PER_GROUP = N_EXPERTS // N_GROUPS
MOE_BLOCK = 256
ROPE_BASE = 10000.0
EPS = 1e-6

LANES = 128
SUBLANES = 8
TM = 256
DMA_GROUP = 8
GDN_CHUNKS_PER_BATCH = 2
ATTN_KEY_CHUNK = 256
VMEM_LIMIT = 56 * 1024 * 1024

OFF_QK = 0
OFF_VA = 2 * A_WIDTH
OFF_G = OFF_VA + A_WIDTH
OFF_Z = OFF_G + G_QKV_WIDTH
OFF_AB = OFF_Z + G_V_WIDTH
OFF_GATE = OFF_AB + LANES
GC_CUM, GC_BETA, GC_EG, GC_EK, GC_ET = 0, 8, 16, 24, 32

PAIRS = tuple((lo, hi) for lo in range(EXPERTS_PER_GROUP) for hi in range(lo + 1, EXPERTS_PER_GROUP))
N_CLASSES = N_GROUPS * len(PAIRS)
CLASS_EA = np.array([g * EXPERTS_PER_GROUP + lo for g in range(N_GROUPS) for lo, _ in PAIRS], np.int32)
CLASS_EB = np.array([g * EXPERTS_PER_GROUP + hi for g in range(N_GROUPS) for _, hi in PAIRS], np.int32)


def _sigmoid(x):
    return 1.0 / (1.0 + jnp.exp(-x))


def _silu(x):
    return x * _sigmoid(x)


def _dot(a, b):
    return jnp.dot(a, b, preferred_element_type=F32)


def _dot_nt(a, b):
    return lax.dot_general(a, b, (((1,), (1,)), ((), ())), preferred_element_type=F32)


def _dot_f32(a, b):
    return jnp.dot(a, b, preferred_element_type=F32, precision=lax.Precision.HIGHEST)


def _params(*sem):
    return pltpu.CompilerParams(dimension_semantics=sem, vmem_limit_bytes=VMEM_LIMIT)


def _to_token_tiles(ref, x):
    rows, width = x.shape
    r = width // LANES
    for c in range(r):
        ref[pl.ds(c, rows, stride=r), :] = x[:, c * LANES:(c + 1) * LANES]


def _from_token_tiles(ref, rows, r):
    return jnp.concatenate([ref[pl.ds(c, rows, stride=r), :] for c in range(r)], axis=-1)


def _mod_kernel(cc_ref, w_ref, b_ref, o_ref):
    h = _silu(cc_ref[...])
    o_ref[...] = _dot(h.astype(BF16), w_ref[...].astype(BF16)) + b_ref[...]


def _mod_tables(cc, mod_w, mod_b):
    depth, d, n = mod_w.shape
    tn = n // 6
    return pl.pallas_call(
        _mod_kernel,
        grid=(depth, n // tn),
        in_specs=[pl.BlockSpec((SUBLANES, d), lambda l, j: (0, 0)),
                  pl.BlockSpec((None, d, tn), lambda l, j: (l, 0, j)),
                  pl.BlockSpec((None, 1, tn), lambda l, j: (l, 0, j))],
        out_specs=pl.BlockSpec((None, SUBLANES, tn), lambda l, j: (l, 0, j)),
        out_shape=jax.ShapeDtypeStruct((depth, SUBLANES, n), F32),
        compiler_params=_params("parallel", "parallel"),
    )(cc, mod_w, mod_b.reshape(depth, 1, n))


def _inproj_kernel(x_ref, mod_ref, w_ref, cs_ref, qk_ref, va_ref, g_ref, z_ref, ab_ref, gate_ref):
    m = mod_ref[...]
    h = (x_ref[...] * (1.0 + m[1:2]) + m[0:1]).astype(BF16)
    cs = cs_ref[...]
    cos, sin = cs[:, :LANES], cs[:, LANES:]
    qk = _dot(h, w_ref[:, OFF_QK:OFF_VA])
    for j in range(2 * A_HEADS):
        a = qk[:, j * LANES:(j + 1) * LANES]
        qk_ref[:, j * LANES:(j + 1) * LANES] = (a * cos + pltpu.roll(a, LANES // 2, 1) * sin).astype(BF16)
    va = _dot(h, w_ref[:, OFF_VA:OFF_G])
    for j in range(A_HEADS):
        va_ref[j * LANES:(j + 1) * LANES, :] = va[:, j * LANES:(j + 1) * LANES].T.astype(BF16)
    g_ref[...] = _dot(h, w_ref[:, OFF_G:OFF_Z])
    z_ref[...] = _dot(h, w_ref[:, OFF_Z:OFF_AB])
    ab_ref[...] = _dot(h, w_ref[:, OFF_AB:OFF_GATE])
    gate_ref[...] = _sigmoid(_dot(h, w_ref[:, OFF_GATE:]))


def _inproj(x_all, modt, w_cat, cs_tab, n_lat_tiles, tiles_per_seq):
    t, d = x_all.shape
    n = w_cat.shape[1]
    d_gate = n - OFF_GATE

    def row(i):
        return (i, 0)

    def cs_map(i):
        return (jnp.where(i < n_lat_tiles, i % tiles_per_seq, tiles_per_seq), 0)

    out_shape = [jax.ShapeDtypeStruct((t, 2 * A_WIDTH), BF16),
                 jax.ShapeDtypeStruct((A_WIDTH, t), BF16),
                 jax.ShapeDtypeStruct((t, G_QKV_WIDTH), F32),
                 jax.ShapeDtypeStruct((t, G_V_WIDTH), F32),
                 jax.ShapeDtypeStruct((t, LANES), F32),
                 jax.ShapeDtypeStruct((t, d_gate), F32)]
    out_specs = [pl.BlockSpec((TM, s.shape[1]), row) for s in out_shape]
    out_specs[1] = pl.BlockSpec((A_WIDTH, TM), lambda i: (0, i))
    return pl.pallas_call(
        _inproj_kernel,
        grid=(t // TM,),
        in_specs=[pl.BlockSpec((TM, d), row),
                  pl.BlockSpec((None, 6, d), lambda i: (i, 0, 0)),
                  pl.BlockSpec((d, n), lambda i: (0, 0)),
                  pl.BlockSpec((TM, 2 * LANES), cs_map)],
        out_specs=out_specs,
        out_shape=out_shape,
        compiler_params=_params("parallel"),
    )(x_all, modt, w_cat, cs_tab)


def _row_fold(x, op):
    rows, n = x.shape
    return op(x.reshape(rows // SUBLANES, SUBLANES, n), axis=0)


def _attn_body(lam_ref, q_ref, kv_refs, g_ref, o_ref, s_refs, lam_init):
    lv = lam_ref[...]
    lam = (jnp.exp(jnp.sum(lv[0:1] * lv[1:2], axis=-1, keepdims=True))
           - jnp.exp(jnp.sum(lv[2:3] * lv[3:4], axis=-1, keepdims=True)) + lam_init)
    q = q_ref[...]
    lane = lax.broadcasted_iota(jnp.int32, (1, LANES), 1)
    comp1 = ((lane // (A_HEAD_DIM // 2)) % 2) == 1
    qn = (jnp.where(comp1, jnp.zeros_like(q), q), jnp.where(comp1, q, jnp.zeros_like(q)))
    m_part, off = [None, None], 0
    for k_ref, _ in kv_refs:
        nk = k_ref.shape[0]
        for n in range(2):
            st = _dot_nt(k_ref[...], qn[n])
            s_refs[n][pl.ds(off, nk), :] = st
            mc = _row_fold(st, jnp.max)
            m_part[n] = mc if m_part[n] is None else jnp.maximum(m_part[n], mc)
        off += nk
    m = [jnp.max(mp, axis=0, keepdims=True) for mp in m_part]
    l_part, ot, off = [None, None], [None, None], 0
    for k_ref, vt_ref in kv_refs:
        for r0 in range(0, k_ref.shape[0], ATTN_KEY_CHUNK):
            for n in range(2):
                p = jnp.exp(s_refs[n][pl.ds(off + r0, ATTN_KEY_CHUNK), :] - m[n])
                lc = _row_fold(p, jnp.sum)
                l_part[n] = lc if l_part[n] is None else l_part[n] + lc
                oc = _dot(vt_ref[:, pl.ds(r0, ATTN_KEY_CHUNK)], p.astype(BF16))
                ot[n] = oc if ot[n] is None else ot[n] + oc
        off += k_ref.shape[0]
    outs = [ot[n] / jnp.sum(l_part[n], axis=0, keepdims=True) for n in range(2)]
    ot = outs[0] - lam * outs[1]
    r = ot * lax.rsqrt(jnp.mean(ot * ot, axis=0, keepdims=True) + EPS)
    o_ref[...] = (r * g_ref[...] * (1.0 - lam_init)).T.astype(BF16)


def _attn_kernel(lam_ref, q_ref, kc_ref, vc_ref, kl_ref, vl_ref, g_ref, o_ref, s0_ref, s1_ref, *, lam_init, n_lat_q):
    is_lat = pl.program_id(2) < n_lat_q
    s_refs = (s0_ref, s1_ref)

    @pl.when(is_lat)
    def _():
        _attn_body(lam_ref, q_ref, ((kc_ref, vc_ref), (kl_ref, vl_ref)), g_ref, o_ref, s_refs, lam_init)

    @pl.when(jnp.logical_not(is_lat))
    def _():
        _attn_body(lam_ref, q_ref, ((kc_ref, vc_ref),), g_ref, o_ref, s_refs, lam_init)


def _attention(qk, vt, lamv, subln_g, *, lam_init, b, s, nc):
    tq = TM
    t = qk.shape[0]
    n_lat_q, n_ctx_q = s // tq, nc // tq
    ctx0 = (b * s) // nc

    def q_map(bi, h, i):
        return (jnp.where(i < n_lat_q, bi * n_lat_q + i, (b * s) // tq + bi * n_ctx_q + (i - n_lat_q)), h)

    return pl.pallas_call(
        functools.partial(_attn_kernel, lam_init=lam_init, n_lat_q=n_lat_q),
        grid=(b, A_HEADS, n_lat_q + n_ctx_q),
        in_specs=[pl.BlockSpec((SUBLANES, LANES), lambda bi, h, i: (0, 0)),
                  pl.BlockSpec((tq, LANES), q_map),
                  pl.BlockSpec((nc, LANES), lambda bi, h, i: (ctx0 + bi, A_HEADS + h)),
                  pl.BlockSpec((LANES, nc), lambda bi, h, i: (h, ctx0 + bi)),
                  pl.BlockSpec((s, LANES), lambda bi, h, i: (bi, A_HEADS + h)),
                  pl.BlockSpec((LANES, s), lambda bi, h, i: (h, bi)),
                  pl.BlockSpec((LANES, 1), lambda bi, h, i: (0, 0))],
        out_specs=pl.BlockSpec((tq, LANES), q_map),
        out_shape=jax.ShapeDtypeStruct((t, A_WIDTH), BF16),
        scratch_shapes=[pltpu.VMEM((nc + s, tq), F32)] * 2,
        compiler_params=_params("parallel", "parallel", "arbitrary"),
    )(lamv, qk, qk, vt, qk, vt, subln_g)


def _gdn_prep_kernel(g_ref, prev_ref, next_ref, ab_ref, cw_ref, gp_ref, q_ref, k_ref, v_ref, gc_ref, ext_ref,
                     *, n_lat_tiles, tiles_per_seq):
    i = pl.program_id(0)
    is_lat = i < n_lat_tiles
    first = jnp.logical_or(jnp.logical_not(is_lat), i % tiles_per_seq == 0)
    last = jnp.logical_or(jnp.logical_not(is_lat), i % tiles_per_seq == tiles_per_seq - 1)
    ext_ref[0:SUBLANES, :] = jnp.where(first, 0.0, prev_ref[...])
    ext_ref[SUBLANES:SUBLANES + TM, :] = g_ref[...]
    ext_ref[SUBLANES + TM:, :] = jnp.where(last, 0.0, next_ref[...])
    cw = cw_ref[...]
    acc = None
    for j in range(CONV_K):
        term = ext_ref[pl.ds(SUBLANES - CONV_K // 2 + j, TM), :] * cw[j:j + 1]
        acc = term if acc is None else acc + term
    y = _silu(acc)
    for h in range(G_HEADS):
        for base, ref, scale in ((0, q_ref, G_KEY_DIM ** -0.5), (G_QK_WIDTH, k_ref, 1.0)):
            xh = y[:, base + h * LANES:base + (h + 1) * LANES]
            nrm = xh * lax.rsqrt(jnp.sum(xh * xh, axis=-1, keepdims=True) + EPS)
            ref[:, h * LANES:(h + 1) * LANES] = nrm * scale
    v_ref[...] = y[:, 2 * G_QK_WIDTH:]

    ab = ab_ref[...]
    gp = gp_ref[...]
    lane = lax.broadcasted_iota(jnp.int32, (1, LANES), 1)
    z = ab + gp[1:2]
    softplus = jnp.maximum(z, 0.0) + jnp.log(1.0 + jnp.exp(-jnp.abs(z)))
    g = jnp.where(lane < 2 * G_HEADS, -jnp.exp(gp[0:1]) * softplus, 0.0)
    beta = _sigmoid(ab)
    ri = lax.broadcasted_iota(jnp.int32, (TM, TM), 0)
    ci = lax.broadcasted_iota(jnp.int32, (TM, TM), 1)
    same = (ri // CHUNK) == (ci // CHUNK)
    lower = jnp.where(jnp.logical_and(same, ci <= ri), 1.0, 0.0)
    upper = jnp.where(jnp.logical_and(same, ci >= ri), 1.0, 0.0)
    cum_f = _dot_f32(lower, g)
    cum_b = _dot_f32(upper, g)
    total = cum_f + cum_b - g
    cum = jnp.where(lane < G_HEADS, cum_f, cum_b)
    eg = jnp.exp(cum)
    ek = jnp.exp(total - cum)
    et = jnp.exp(total)
    out = jnp.where(lane < GC_BETA, cum, 0.0)
    out = jnp.where(jnp.logical_and(lane >= GC_BETA, lane < GC_EG), beta, out)
    out = jnp.where(jnp.logical_and(lane >= GC_EG, lane < GC_EK), pltpu.roll(eg, GC_EG, 1), out)
    out = jnp.where(jnp.logical_and(lane >= GC_EK, lane < GC_ET), pltpu.roll(ek, GC_EK, 1), out)
    out = jnp.where(jnp.logical_and(lane >= GC_ET, lane < GC_ET + 8), pltpu.roll(et, GC_ET, 1), out)
    gc_ref[...] = out


def _gdn_prep(g, ab, conv_w8, gparams, n_lat_tiles, tiles_per_seq):
    t, c = g.shape
    nblk8 = t // SUBLANES
    per = TM // SUBLANES
    row = lambda i: (i, 0)
    out_shape = [jax.ShapeDtypeStruct((t, G_QK_WIDTH), F32),
                 jax.ShapeDtypeStruct((t, G_QK_WIDTH), F32),
                 jax.ShapeDtypeStruct((t, G_V_WIDTH), F32),
                 jax.ShapeDtypeStruct((t, LANES), F32)]
    return pl.pallas_call(
        functools.partial(_gdn_prep_kernel, n_lat_tiles=n_lat_tiles, tiles_per_seq=tiles_per_seq),
        grid=(t // TM,),
        in_specs=[pl.BlockSpec((TM, c), row),
                  pl.BlockSpec((SUBLANES, c), lambda i: (jnp.maximum(i * per - 1, 0), 0)),
                  pl.BlockSpec((SUBLANES, c), lambda i: (jnp.minimum((i + 1) * per, nblk8 - 1), 0)),
                  pl.BlockSpec((TM, LANES), row),
                  pl.BlockSpec((SUBLANES, c), lambda i: (0, 0)),
                  pl.BlockSpec((SUBLANES, LANES), lambda i: (0, 0))],
        out_specs=[pl.BlockSpec((TM, s.shape[1]), row) for s in out_shape],
        out_shape=out_shape,
        scratch_shapes=[pltpu.VMEM((TM + 2 * SUBLANES, c), F32)],
        compiler_params=_params("parallel"),
    )(g, g, g, ab, conv_w8, gparams)


def _gdn_masks(c, reverse):
    ii = lax.broadcasted_iota(jnp.int32, (c, 2 * c), 0)
    lane = lax.broadcasted_iota(jnp.int32, (c, 2 * c), 1)
    left = lane < c
    jj = jnp.where(left, lane, lane - c)
    causal = (ii <= jj) if reverse else (ii >= jj)
    strict = (ii < jj) if reverse else (ii > jj)
    return dict(left=left, diag=ii == jj, causal=causal, strict=strict)


def _gdn_setup(q, k, v, gc, gct, col, mk):
    c = q.shape[0]
    gcol = gc[:, GC_CUM + col:GC_CUM + col + 1]
    beta = gc[:, GC_BETA + col:GC_BETA + col + 1]
    eg = gc[:, GC_EG + col:GC_EG + col + 1]
    ek = gc[:, GC_EK + col:GC_EK + col + 1]
    et = gc[0:1, GC_ET + col:GC_ET + col + 1]
    grow = gct[GC_CUM + col:GC_CUM + col + 1, :]
    decay = jnp.exp(jnp.where(mk["causal"], gcol - grow, -1e30))
    kb = k * beta
    k16 = k.astype(BF16)
    both = _dot_nt(jnp.concatenate([kb.astype(BF16), q.astype(BF16)], axis=0),
                   jnp.concatenate([k16, k16], axis=0))
    qk = jnp.where(jnp.logical_and(mk["causal"], mk["left"]), both[c:] * decay, 0.0)
    z = jnp.where(mk["left"], jnp.where(mk["diag"], 1.0, 0.0), jnp.where(mk["strict"], -(both[:c] * decay), 0.0))
    rhs = jnp.concatenate([v * beta, kb * eg], axis=1)
    return dict(z=z, qk=qk.astype(BF16), rhs=rhs, qd=(q * eg).astype(BF16), kdt=(k * ek).T.astype(BF16), et=et)


def _gdn_level(z, mk):
    z16 = z.astype(BF16)
    return jnp.where(mk["left"], z, 0.0) + _dot(z16, jnp.concatenate([jnp.zeros_like(z16), z16], axis=0))


def _gdn_solve(ch, mk):
    y16 = jnp.where(jnp.logical_and(mk["left"], jnp.logical_not(mk["diag"])), ch["z"], 0.0).astype(BF16)
    rhs16 = ch["rhs"].astype(BF16)
    sol = ch["rhs"] + _dot(y16, jnp.concatenate([rhs16, rhs16], axis=0))
    dv = sol.shape[1] // 2
    return sol[:, :dv], jnp.concatenate([sol[:, dv:].astype(BF16), ch["qd"]], axis=0)


def _gdn_kernel(qf_ref, kf_ref, vf_ref, gf_ref, qb_ref, kb_ref, vb_ref, gb_ref, of_ref, ob_ref, s_ref):
    @pl.when(pl.program_id(1) == 0)
    def _():
        s_ref[...] = jnp.zeros_like(s_ref)

    n_chunks = qf_ref.shape[0] // CHUNK
    masks = (_gdn_masks(CHUNK, False), _gdn_masks(CHUNK, True))
    dirs = ((qf_ref, kf_ref, vf_ref, gf_ref, of_ref), (qb_ref, kb_ref, vb_ref, gb_ref, ob_ref))
    state = [s_ref[i] for i in range(2 * G_HEADS)]
    for n0 in range(0, n_chunks, GDN_CHUNKS_PER_BATCH):
        chains = []
        for n in range(n0, n0 + GDN_CHUNKS_PER_BATCH):
            for d, (q_ref, k_ref, v_ref, g_ref, o_ref) in enumerate(dirs):
                rows = pl.ds((n_chunks - 1 - n if d else n) * CHUNK, CHUNK)
                gc = g_ref[rows, :]
                gct = jnp.concatenate([gc, gc], axis=0).T
                for h in range(G_HEADS):
                    cols = pl.ds(h * LANES, LANES)
                    ch = _gdn_setup(q_ref[rows, cols], k_ref[rows, cols], v_ref[rows, cols], gc, gct,
                                    d * G_HEADS + h, masks[d])
                    ch.update(d=d, h=h, rows=rows, cols=cols, out=o_ref)
                    chains.append(ch)
        for _ in range(int(math.log2(CHUNK))):
            for ch in chains:
                ch["z"] = _gdn_level(ch["z"], masks[ch["d"]])
        for ch in chains:
            ch["u"], ch["wq"] = _gdn_solve(ch, masks[ch["d"]])
        per_chunk = 2 * G_HEADS
        for c0 in range(0, len(chains), per_chunk):
            group = chains[c0:c0 + per_chunk]
            ws = [_dot(ch["wq"], state[ch["d"] * G_HEADS + ch["h"]].astype(BF16)) for ch in group]
            for ch, w in zip(group, ws):
                i = ch["d"] * G_HEADS + ch["h"]
                v_new = (ch["u"] - w[:CHUNK]).astype(BF16)
                o = w[CHUNK:] + _dot(ch["qk"], jnp.concatenate([v_new, v_new], axis=0))
                state[i] = state[i] * ch["et"] + _dot(ch["kdt"], v_new)
                ch["out"][ch["rows"], ch["cols"]] = o
    for i in range(2 * G_HEADS):
        s_ref[i] = state[i]


def _gdn(qg, kg, vg, gc, *, b, s, nc):
    t = qg.shape[0]
    tps = s // TM
    nct = nc // TM
    n_steps = nct + tps
    ctx_tile0 = (b * s) // TM

    def fwd(bi, i):
        return (jnp.where(i < nct, ctx_tile0 + bi * nct + i, bi * tps + (i - nct)), 0)

    def bwd(bi, i):
        return (jnp.where(i < nct, ctx_tile0 + bi * nct + (nct - 1 - i), bi * tps + (tps - 1 - (i - nct))), 0)

    def specs(row_map):
        return [pl.BlockSpec((TM, G_QK_WIDTH), row_map), pl.BlockSpec((TM, G_QK_WIDTH), row_map),
                pl.BlockSpec((TM, G_V_WIDTH), row_map), pl.BlockSpec((TM, LANES), row_map)]

    return pl.pallas_call(
        _gdn_kernel,
        grid=(b, n_steps),
        in_specs=specs(fwd) + specs(bwd),
        out_specs=[pl.BlockSpec((TM, G_V_WIDTH), fwd), pl.BlockSpec((TM, G_V_WIDTH), bwd)],
        out_shape=[jax.ShapeDtypeStruct((t, G_V_WIDTH), F32)] * 2,
        scratch_shapes=[pltpu.VMEM((2 * G_HEADS, G_KEY_DIM, G_VAL_DIM), F32)],
        compiler_params=_params("parallel", "arbitrary"),
    )(qg, kg, vg, gc, qg, kg, vg, gc)


def _layer_norm(r, g, b):
    mu = jnp.mean(r, axis=-1, keepdims=True)
    rc = r - mu
    var = jnp.mean(rc * rc, axis=-1, keepdims=True)
    return rc * lax.rsqrt(var + EPS) * g + b


def _merge_kernel(x_ref, ya_ref, of_ref, ob_ref, z_ref, gate_ref, mod_ref, wpa_ref, wpb_ref, wo_ref, on_ref,
                  lng_ref, lnb_ref, rw_ref, rb_ref, xo_ref, ht_ref, cls_ref, ew_ref, *, alpha):
    d = x_ref.shape[1]
    o = of_ref[...] + ob_ref[...]
    parts = []
    for h in range(G_HEADS):
        oh = o[:, h * LANES:(h + 1) * LANES]
        parts.append(oh * lax.rsqrt(jnp.mean(oh * oh, axis=-1, keepdims=True) + EPS) * on_ref[...])
    yb = jnp.concatenate(parts, axis=1) * _silu(z_ref[...])
    pa = _dot(ya_ref[...], wpa_ref[...])
    pb = _dot(yb.astype(BF16), wpb_ref[...])
    mix = gate_ref[:, :d] * pa + gate_ref[:, d:] * pb
    y = _dot(mix.astype(BF16), wo_ref[...])
    m = mod_ref[...]
    xn = _layer_norm(alpha * x_ref[...] + m[2:3] * y, lng_ref[...], lnb_ref[...])
    xo_ref[...] = xn
    h2 = xn * (1.0 + m[4:5]) + m[3:4]
    _to_token_tiles(ht_ref, h2)

    score = _sigmoid(_dot_nt(rw_ref[...], h2.astype(BF16)))
    sel = score + rb_ref[...]
    rows = [sel[e:e + 1] for e in range(N_EXPERTS)]
    best = None
    best_val = None
    for gi in range(N_GROUPS):
        a, b, c, dd = rows[gi * EXPERTS_PER_GROUP:(gi + 1) * EXPERTS_PER_GROUP]
        top2 = jnp.maximum(jnp.maximum(jnp.maximum(a + b, a + c), jnp.maximum(a + dd, b + c)),
                           jnp.maximum(b + dd, c + dd))
        if gi == 0:
            best, best_val = jnp.zeros(top2.shape, jnp.int32), top2
        else:
            upd = top2 > best_val
            best = jnp.where(upd, gi, best)
            best_val = jnp.where(upd, top2, best_val)
    in_group, in_score = [], []
    for j in range(EXPERTS_PER_GROUP):
        v = rows[(N_GROUPS - 1) * EXPERTS_PER_GROUP + j]
        sv = score[(N_GROUPS - 1) * EXPERTS_PER_GROUP + j:(N_GROUPS - 1) * EXPERTS_PER_GROUP + j + 1]
        for gi in range(N_GROUPS - 2, -1, -1):
            e = gi * EXPERTS_PER_GROUP + j
            v = jnp.where(best == gi, rows[e], v)
            sv = jnp.where(best == gi, score[e:e + 1], sv)
        in_group.append(v)
        in_score.append(sv)
    l0 = jnp.zeros(best.shape, jnp.int32)
    m0 = in_group[0]
    for j in range(1, EXPERTS_PER_GROUP):
        upd = in_group[j] > m0
        l0 = jnp.where(upd, j, l0)
        m0 = jnp.where(upd, in_group[j], m0)
    l1 = jnp.zeros(best.shape, jnp.int32)
    m1 = jnp.full(m0.shape, -jnp.inf, F32)
    for j in range(EXPERTS_PER_GROUP):
        upd = jnp.logical_and(l0 != j, in_group[j] > m1)
        l1 = jnp.where(upd, j, l1)
        m1 = jnp.where(upd, in_group[j], m1)
    lo = jnp.minimum(l0, l1)
    hi = jnp.maximum(l0, l1)
    w_lo = jnp.zeros(m0.shape, F32)
    w_hi = jnp.zeros(m0.shape, F32)
    for j in range(EXPERTS_PER_GROUP):
        w_lo = jnp.where(lo == j, in_score[j], w_lo)
        w_hi = jnp.where(hi == j, in_score[j], w_hi)
    pair = jnp.where(lo == 0, hi - 1, jnp.where(lo == 1, hi + 1, len(PAIRS) - 1))
    tot = w_lo + w_hi
    ri = lax.broadcasted_iota(jnp.int32, cls_ref.shape, 0)
    cls_ref[...] = jnp.where(ri == 0, best * len(PAIRS) + pair, 0)
    ew_ref[...] = jnp.where(ri == 0, w_lo / tot, jnp.where(ri == 1, w_hi / tot, 0.0))


def _merge(x_all, ya, o_f, o_b, z, gate, modt, wpa, wpb, wo, onorm, lng, lnb, rw_t, rb, *, alpha):
    t, d = x_all.shape
    r = d // LANES
    row = lambda i: (i, 0)
    col = lambda i: (0, i)
    const = lambda i: (0, 0)
    out_shape = [jax.ShapeDtypeStruct((t, d), F32),
                 jax.ShapeDtypeStruct((t * r, LANES), F32),
                 jax.ShapeDtypeStruct((SUBLANES, t), jnp.int32),
                 jax.ShapeDtypeStruct((SUBLANES, t), F32)]
    return pl.pallas_call(
        functools.partial(_merge_kernel, alpha=alpha),
        grid=(t // TM,),
        in_specs=[pl.BlockSpec((TM, d), row),
                  pl.BlockSpec((TM, A_WIDTH), row),
                  pl.BlockSpec((TM, G_V_WIDTH), row),
                  pl.BlockSpec((TM, G_V_WIDTH), row),
                  pl.BlockSpec((TM, G_V_WIDTH), row),
                  pl.BlockSpec((TM, 2 * d), row),
                  pl.BlockSpec((None, 6, d), lambda i: (i, 0, 0)),
                  pl.BlockSpec(wpa.shape, const),
                  pl.BlockSpec(wpb.shape, const),
                  pl.BlockSpec(wo.shape, const),
                  pl.BlockSpec((1, LANES), const),
                  pl.BlockSpec((1, d), const),
                  pl.BlockSpec((1, d), const),
                  pl.BlockSpec(rw_t.shape, const),
                  pl.BlockSpec(rb.shape, const)],
        out_specs=[pl.BlockSpec((TM, d), row), pl.BlockSpec((TM * r, LANES), row),
                   pl.BlockSpec((SUBLANES, TM), col), pl.BlockSpec((SUBLANES, TM), col)],
        out_shape=out_shape,
        compiler_params=_params("parallel"),
    )(x_all, ya, o_f, o_b, z, gate, modt, wpa, wpb, wo, onorm, lng, lnb, rw_t, rb)


def _expert_kernel(ea_ref, eb_ref, nu_ref, nv_ref, src_ref, h_hbm, ws_ref, w1a_ref, w3a_ref, w2a_ref,
                   w1b_ref, w3b_ref, w2b_ref, y_hbm, xbuf, ybuf, gsem, ssem, *, r):
    i = pl.program_id(0)
    n_used = nu_ref[0]
    slot = i % 2

    def gather_copy(blk, buf, j):
        tok = src_ref[blk * MOE_BLOCK + j]
        return pltpu.make_async_copy(h_hbm.at[pl.ds(pl.multiple_of(tok * r, r), r), :],
                                     xbuf.at[buf, pl.ds(pl.multiple_of(j * r, r), r), :], gsem.at[buf])

    def scatter_copy(blk, buf, j):
        tok = src_ref[blk * MOE_BLOCK + j]
        return pltpu.make_async_copy(ybuf.at[buf, pl.ds(pl.multiple_of(j * r, r), r), :],
                                     y_hbm.at[pl.ds(pl.multiple_of(tok * r, r), r), :], ssem.at[buf])

    def for_slots(fn):
        def body(j, carry):
            fn(j)
            return carry
        lax.fori_loop(0, MOE_BLOCK, body, 0, unroll=DMA_GROUP)

    def for_token_slots(blk, fn):
        nv = nv_ref[blk]

        def body(g, carry):
            j0 = g * DMA_GROUP

            @pl.when(j0 + DMA_GROUP <= nv)
            def _():
                for k in range(DMA_GROUP):
                    fn(j0 + k)

            @pl.when(jnp.logical_and(j0 < nv, j0 + DMA_GROUP > nv))
            def _():
                for k in range(DMA_GROUP):
                    pl.when(j0 + k < nv)(functools.partial(fn, j0 + k))
            return carry
        lax.fori_loop(0, MOE_BLOCK // DMA_GROUP, body, 0)

    @pl.when(i == 0)
    def _():
        for_slots(lambda j: gather_copy(0, 0, j).start())

    @pl.when(i < n_used)
    def _():
        for_slots(lambda j: gather_copy(i, slot, j).wait())

        @pl.when(i + 1 < n_used)
        def _():
            for_slots(lambda j: gather_copy(i + 1, 1 - slot, j).start())

        x = _from_token_tiles(xbuf.at[slot], MOE_BLOCK, r).astype(BF16)
        ws = ws_ref[...]

        ga, ua = _dot(x, w1a_ref[...]), _dot(x, w3a_ref[...])
        gb, ub = _dot(x, w1b_ref[...]), _dot(x, w3b_ref[...])
        ya = _dot((_silu(ga) * ua).astype(BF16), w2a_ref[...])
        yb = _dot((_silu(gb) * ub).astype(BF16), w2b_ref[...])
        y = ws[:, 0:1] * ya + ws[:, 1:2] * yb

        _to_token_tiles(ybuf.at[slot], y)

        @pl.when(i >= 1)
        def _():
            for_token_slots(i - 1, lambda j: scatter_copy(i - 1, 1 - slot, j).wait())

        for_token_slots(i, lambda j: scatter_copy(i, slot, j).start())

        @pl.when(i == n_used - 1)
        def _():
            for_token_slots(i, lambda j: scatter_copy(i, slot, j).wait())


def _experts(ht, w_slot, block_ea, block_eb, n_used, n_valid, slot_src, w1, w3, w2, n_out_tokens):
    d, f = w1.shape[1], w1.shape[2]
    r = d // LANES
    n_slots = slot_src.shape[0]

    def wspec(shape, which):
        if which == 0:
            return pl.BlockSpec((None,) + shape, lambda i, ea, eb, nu, nv, src: (ea[i], 0, 0))
        return pl.BlockSpec((None,) + shape, lambda i, ea, eb, nu, nv, src: (eb[i], 0, 0))

    grid_spec = pltpu.PrefetchScalarGridSpec(
        num_scalar_prefetch=5,
        grid=(n_slots // MOE_BLOCK,),
        in_specs=[pl.BlockSpec(memory_space=pl.ANY),
                  pl.BlockSpec((MOE_BLOCK, 2), lambda i, ea, eb, nu, nv, src: (i, 0)),
                  wspec((d, f), 0), wspec((d, f), 0), wspec((f, d), 0),
                  wspec((d, f), 1), wspec((d, f), 1), wspec((f, d), 1)],
        out_specs=pl.BlockSpec(memory_space=pl.ANY),
        scratch_shapes=[pltpu.VMEM((2, MOE_BLOCK * r, LANES), F32),
                        pltpu.VMEM((2, MOE_BLOCK * r, LANES), F32),
                        pltpu.SemaphoreType.DMA((2,)),
                        pltpu.SemaphoreType.DMA((2,))],
    )
    return pl.pallas_call(
        functools.partial(_expert_kernel, r=r),
        grid_spec=grid_spec,
        out_shape=jax.ShapeDtypeStruct((n_out_tokens * r, LANES), F32),
        compiler_params=_params("arbitrary"),
    )(block_ea, block_eb, n_used, n_valid, slot_src, ht, w_slot, w1, w3, w2, w1, w3, w2)


def _ln2_kernel(x_ref, yt_ref, mod_ref, g_ref, b_ref, o_ref, *, alpha):
    tm, d = x_ref.shape
    y = _from_token_tiles(yt_ref, tm, d // LANES)
    m = mod_ref[...]
    o_ref[...] = _layer_norm(alpha * x_ref[...] + m[5:6] * y, g_ref[...], b_ref[...])


def _ln2(x_all, yt, modt, g, b, *, alpha):
    t, d = x_all.shape
    r = d // LANES
    row = lambda i: (i, 0)
    const = lambda i: (0, 0)
    return pl.pallas_call(
        functools.partial(_ln2_kernel, alpha=alpha),
        grid=(t // TM,),
        in_specs=[pl.BlockSpec((TM, d), row), pl.BlockSpec((TM * r, LANES), row),
                  pl.BlockSpec((None, 6, d), lambda i: (i, 0, 0)),
                  pl.BlockSpec((1, d), const), pl.BlockSpec((1, d), const)],
        out_specs=pl.BlockSpec((TM, d), row),
        out_shape=jax.ShapeDtypeStruct((t, d), F32),
        compiler_params=_params("parallel"),
    )(x_all, yt, modt, g, b)


def _rope_table(s, nc):
    n_freq = A_HEAD_DIM // 4
    rows = s // GRID_W
    row = jnp.repeat(jnp.arange(rows, dtype=F32), GRID_W)
    col = jnp.tile(jnp.arange(GRID_W, dtype=F32), rows)
    inv = ROPE_BASE ** (-jnp.arange(n_freq, dtype=F32) / n_freq)
    ang = jnp.stack([row[:, None] * inv, col[:, None] * inv], axis=1)
    cos = jnp.cos(ang).reshape(s, 1, 1, 2, n_freq)
    sin = jnp.sin(ang).reshape(s, 1, 1, 2, n_freq)
    cos = jnp.broadcast_to(cos, (s, 2, 2, 2, n_freq)).reshape(s, LANES)
    sign = jnp.array([-1.0, 1.0], F32).reshape(1, 2, 1, 1, 1)
    sin = jnp.broadcast_to(sin * sign, (s, 2, 2, 2, n_freq)).reshape(s, LANES)
    lat = jnp.concatenate([cos, sin], axis=1)
    ident = jnp.concatenate([jnp.ones((nc, LANES), F32), jnp.zeros((nc, LANES), F32)], axis=1)
    return jnp.concatenate([lat, ident], axis=0)


def _rope_perm(w):
    d = w.shape[0]
    n_freq = A_HEAD_DIM // 4
    return w.reshape(d, A_HEADS, 2, 2, 2, n_freq).transpose(0, 1, 4, 2, 3, 5).reshape(d, A_WIDTH)


def _pack_w_in(w):
    d = w.shape[0]
    sizes = (A_WIDTH, A_WIDTH, A_WIDTH, G_QKV_WIDTH, G_V_WIDTH, 2 * G_HEADS, 2 * G_HEADS)
    offs = [0]
    for sz in sizes:
        offs.append(offs[-1] + sz)
    wq, wk, wv, wg, wz, wa, wb = (w[:, offs[i]:offs[i + 1]] for i in range(len(sizes)))
    wgate = w[:, offs[-1]:]
    wab = jnp.concatenate([wa, wb, jnp.zeros((d, LANES - 4 * G_HEADS), w.dtype)], axis=1)
    cat = jnp.concatenate([_rope_perm(wq) * (A_HEAD_DIM ** -0.5), _rope_perm(wk), wv, wg, wz, wab, wgate], axis=1)
    return cat.astype(BF16)


def _pad_rows(a, rows):
    return jnp.concatenate([a, jnp.zeros((rows - a.shape[0],) + a.shape[1:], a.dtype)], axis=0)


def _pad_lanes(a):
    return jnp.concatenate([a, jnp.zeros(a.shape[:-1] + (LANES - a.shape[-1],), a.dtype)], axis=-1)


def _dispatch(cls, ew, t):
    onehot = (cls[:, None] == jnp.arange(N_CLASSES, dtype=jnp.int32)[None, :]).astype(jnp.int32)
    csum = jnp.cumsum(onehot, axis=0)
    rank = jnp.sum(csum * onehot, axis=1) - 1
    counts = csum[-1]
    padded = (counts + MOE_BLOCK - 1) // MOE_BLOCK * MOE_BLOCK
    pad_end = jnp.cumsum(padded)
    pad_start = pad_end - padded
    dest = pad_start[cls] + rank
    n_blocks = -(-t // MOE_BLOCK) + N_CLASSES
    per_tok = jnp.concatenate([jnp.arange(t, dtype=F32)[:, None], ew], axis=1)
    per_slot = jnp.zeros((n_blocks * MOE_BLOCK, per_tok.shape[1]), F32).at[dest].set(per_tok)
    slot_src = per_slot[:, 0].astype(jnp.int32)
    w_slot = per_slot[:, 1:]
    block0 = jnp.arange(n_blocks, dtype=jnp.int32) * MOE_BLOCK
    block_cls = jnp.minimum(jnp.searchsorted(pad_end, block0, side='right'), N_CLASSES - 1)
    n_valid = jnp.clip((pad_start + counts)[block_cls] - block0, 0, MOE_BLOCK).astype(jnp.int32)
    n_used = (pad_end[-1:] // MOE_BLOCK).astype(jnp.int32)
    return slot_src, w_slot, jnp.asarray(CLASS_EA)[block_cls], jnp.asarray(CLASS_EB)[block_cls], n_used, n_valid


def kernel(x, c, ctx, c_ctx, mod_w, mod_b, w_in, conv_w, lam_q1, lam_k1, lam_q2, lam_k2, subln_g, a_log, dt_bias,
           onorm_g, w_pa, w_pb, w_o, ln1_g, ln1_b, router_w, router_b, w_exp1, w_exp3, w_exp2, ln2_g, ln2_b):
    b, s, d = x.shape
    nc = ctx.shape[1]
    depth = mod_w.shape[0]
    assert s % TM == 0 and nc % TM == 0 and s % nc == 0 and s % GRID_W == 0
    assert d == SUBLANES * LANES
    t_lat, t_ctx = b * s, b * nc
    t = t_lat + t_ctx
    n_lat_tiles = t_lat // TM
    tiles_per_seq = s // TM
    alpha = (2.0 * depth) ** 0.25

    x_all = jnp.concatenate([x.reshape(t_lat, d), ctx.reshape(t_ctx, d)], axis=0)
    cc = _pad_rows(jnp.concatenate([c, c_ctx[None, :]], axis=0), SUBLANES)
    mods = _mod_tables(cc, mod_w, mod_b)
    tile_row = jnp.concatenate([jnp.arange(n_lat_tiles, dtype=jnp.int32) // tiles_per_seq,
                                jnp.full((t_ctx // TM,), b, jnp.int32)])
    cs_tab = _rope_table(s, nc)
    rw_t = router_w.T.astype(BF16)
    rb = router_b.reshape(N_EXPERTS, 1)

    for l in range(depth):
        lam_init = 0.8 - 0.6 * math.exp(-0.3 * l)
        modt = mods[l].reshape(SUBLANES, 6, d)[tile_row]
        qk, vt, g, z, ab, gate = _inproj(x_all, modt, _pack_w_in(w_in[l]), cs_tab, n_lat_tiles, tiles_per_seq)

        lamv = _pad_rows(_pad_lanes(jnp.stack([lam_q1[l], lam_k1[l], lam_q2[l], lam_k2[l]])), SUBLANES)
        ya = _attention(qk, vt, lamv, subln_g[l].reshape(LANES, 1), lam_init=lam_init, b=b, s=s, nc=nc)

        gparams = _pad_rows(_pad_lanes(jnp.stack([a_log[l].reshape(-1), dt_bias[l].reshape(-1)])), SUBLANES)
        qg, kg, vg, gc = _gdn_prep(g, ab, _pad_rows(conv_w[l], SUBLANES), gparams, n_lat_tiles, tiles_per_seq)
        o_f, o_b = _gdn(qg, kg, vg, gc, b=b, s=s, nc=nc)

        x_all, ht, cls, ew = _merge(x_all, ya, o_f, o_b, z, gate, modt, w_pa[l].astype(BF16), w_pb[l].astype(BF16),
                                    w_o[l].astype(BF16), onorm_g[l].reshape(1, LANES), ln1_g[l].reshape(1, d),
                                    ln1_b[l].reshape(1, d), rw_t, rb, alpha=alpha)

        slot_src, w_slot, block_ea, block_eb, n_used, n_valid = _dispatch(cls[0], ew[:2].T, t)
        yt = _experts(ht, w_slot, block_ea, block_eb, n_used, n_valid, slot_src, w_exp1[l].astype(BF16),
                      w_exp3[l].astype(BF16), w_exp2[l].astype(BF16), t)
        x_all = _ln2(x_all, yt, modt, ln2_g[l].reshape(1, d), ln2_b[l].reshape(1, d), alpha=alpha)
    return x_all[:t_lat].reshape(b, s, d)
```

```python
import functools
import math

import numpy as np

import jax
import jax.numpy as jnp
from jax import lax
from jax.experimental import pallas as pl
from jax.experimental.pallas import tpu as pltpu

F32 = jnp.float32
BF16 = jnp.bfloat16

GRID_W = 64
A_HEADS = 4
A_HEAD_DIM = 64
A_WIDTH = A_HEADS * 2 * A_HEAD_DIM
G_HEADS = 4
G_KEY_DIM = 128
G_VAL_DIM = 128
G_QK_WIDTH = G_HEADS * G_KEY_DIM
G_V_WIDTH = G_HEADS * G_VAL_DIM
G_QKV_WIDTH = 2 * G_QK_WIDTH + G_V_WIDTH
CONV_K = 5
CHUNK = 64
N_EXPERTS = 16
N_GROUPS = 4
EXPERTS_PER_GROUP = N_EXPERTS // N_GROUPS
MOE_BLOCK = 256
ROPE_BASE = 10000.0
EPS = 1e-6
LOG2_E = math.log2(math.e)

LANES = 128
SUBLANES = 8
TM = 256
DMA_GROUP = 8
GDN_CHUNKS_PER_BATCH = 2
ATTN_KEY_CHUNK = 256
VMEM_LIMIT = 56 * 1024 * 1024

OFF_QK = 0
OFF_VA = 2 * A_WIDTH
OFF_G = OFF_VA + A_WIDTH
OFF_Z = OFF_G + G_QKV_WIDTH
OFF_AB = OFF_Z + G_V_WIDTH
OFF_GATE = OFF_AB + LANES
GC_CUM, GC_BETA, GC_EG, GC_EK, GC_ET = 0, 8, 16, 24, 32

PAIRS = tuple((lo, hi) for lo in range(EXPERTS_PER_GROUP) for hi in range(lo + 1, EXPERTS_PER_GROUP))
N_CLASSES = N_GROUPS * len(PAIRS)
CLASS_EA = np.array([g * EXPERTS_PER_GROUP + lo for g in range(N_GROUPS) for lo, _ in PAIRS], np.int32)
CLASS_EB = np.array([g * EXPERTS_PER_GROUP + hi for g in range(N_GROUPS) for _, hi in PAIRS], np.int32)


def _sigmoid(x):
    return 1.0 / (1.0 + jnp.exp(-x))


def _silu(x):
    return x * _sigmoid(x)


def _dot(a, b):
    return jnp.dot(a, b, preferred_element_type=F32)


def _dot_nt(a, b):
    return lax.dot_general(a, b, (((1,), (1,)), ((), ())), preferred_element_type=F32)


def _dot_f32(a, b):
    return jnp.dot(a, b, preferred_element_type=F32, precision=lax.Precision.HIGHEST)


def _params(*sem):
    return pltpu.CompilerParams(dimension_semantics=sem, vmem_limit_bytes=VMEM_LIMIT)


def _to_token_tiles(ref, x):
    rows, width = x.shape
    r = width // LANES
    for c in range(r):
        ref[pl.ds(c, rows, stride=r), :] = x[:, c * LANES:(c + 1) * LANES]


def _from_token_tiles(ref, rows, r):
    return jnp.concatenate([ref[pl.ds(c, rows, stride=r), :] for c in range(r)], axis=-1)


def _mod_kernel(cc_ref, w_ref, b_ref, o_ref):
    h = _silu(cc_ref[...])
    o_ref[...] = _dot(h.astype(BF16), w_ref[...].astype(BF16)) + b_ref[...]


def _mod_tables(cc, mod_w, mod_b):
    depth, d, n = mod_w.shape
    tn = n // 6
    return pl.pallas_call(
        _mod_kernel,
        grid=(depth, n // tn),
        in_specs=[pl.BlockSpec((SUBLANES, d), lambda l, j: (0, 0)),
                  pl.BlockSpec((None, d, tn), lambda l, j: (l, 0, j)),
                  pl.BlockSpec((None, 1, tn), lambda l, j: (l, 0, j))],
        out_specs=pl.BlockSpec((None, SUBLANES, tn), lambda l, j: (l, 0, j)),
        out_shape=jax.ShapeDtypeStruct((depth, SUBLANES, n), F32),
        compiler_params=_params("parallel", "parallel"),
    )(cc, mod_w, mod_b.reshape(depth, 1, n))


def _inproj_kernel(x_ref, mod_ref, w_ref, cs_ref, qk_ref, va_ref, g_ref, z_ref, ab_ref, gate_ref):
    m = mod_ref[...]
    h = (x_ref[...] * (1.0 + m[1:2]) + m[0:1]).astype(BF16)
    cs = cs_ref[...]
    cos, sin = cs[:, :LANES], cs[:, LANES:]
    qk = _dot(h, w_ref[:, OFF_QK:OFF_VA])
    for j in range(2 * A_HEADS):
        a = qk[:, j * LANES:(j + 1) * LANES]
        rot = a * cos + pltpu.roll(a, LANES // 2, 1) * sin
        if j < A_HEADS:
            rot = rot * LOG2_E
        qk_ref[:, j * LANES:(j + 1) * LANES] = rot.astype(BF16)
    va = _dot(h, w_ref[:, OFF_VA:OFF_G])
    for j in range(A_HEADS):
        va_ref[j * LANES:(j + 1) * LANES, :] = va[:, j * LANES:(j + 1) * LANES].T.astype(BF16)
    g_ref[...] = _dot(h, w_ref[:, OFF_G:OFF_Z])
    z_ref[...] = _dot(h, w_ref[:, OFF_Z:OFF_AB])
    ab_ref[...] = _dot(h, w_ref[:, OFF_AB:OFF_GATE])
    gate_ref[...] = _sigmoid(_dot(h, w_ref[:, OFF_GATE:])).astype(BF16)


def _inproj(x_all, modt, w_cat, cs_tab, n_lat_tiles, tiles_per_seq):
    t, d = x_all.shape
    n = w_cat.shape[1]
    d_gate = n - OFF_GATE

    def row(i):
        return (i, 0)

    def cs_map(i):
        return (jnp.where(i < n_lat_tiles, i % tiles_per_seq, tiles_per_seq), 0)

    out_shape = [jax.ShapeDtypeStruct((t, 2 * A_WIDTH), BF16),
                 jax.ShapeDtypeStruct((A_WIDTH, t), BF16),
                 jax.ShapeDtypeStruct((t, G_QKV_WIDTH), F32),
                 jax.ShapeDtypeStruct((t, G_V_WIDTH), F32),
                 jax.ShapeDtypeStruct((t, LANES), F32),
                 jax.ShapeDtypeStruct((t, d_gate), BF16)]
    out_specs = [pl.BlockSpec((TM, s.shape[1]), row) for s in out_shape]
    out_specs[1] = pl.BlockSpec((A_WIDTH, TM), lambda i: (0, i))
    return pl.pallas_call(
        _inproj_kernel,
        grid=(t // TM,),
        in_specs=[pl.BlockSpec((TM, d), row),
                  pl.BlockSpec((None, 6, d), lambda i: (i, 0, 0)),
                  pl.BlockSpec((d, n), lambda i: (0, 0)),
                  pl.BlockSpec((TM, 2 * LANES), cs_map)],
        out_specs=out_specs,
        out_shape=out_shape,
        compiler_params=_params("parallel"),
    )(x_all, modt, w_cat, cs_tab)


def _row_fold(x, op):
    rows, n = x.shape
    return op(x.reshape(rows // SUBLANES, SUBLANES, n), axis=0)


def _attn_body(lam_ref, q_ref, kv_refs, g_ref, o_ref, s_refs, lam_init):
    lv = lam_ref[...]
    lam = (jnp.exp(jnp.sum(lv[0:1] * lv[1:2], axis=-1, keepdims=True))
           - jnp.exp(jnp.sum(lv[2:3] * lv[3:4], axis=-1, keepdims=True)) + lam_init)
    q = q_ref[...]
    lane = lax.broadcasted_iota(jnp.int32, (1, LANES), 1)
    comp1 = ((lane // (A_HEAD_DIM // 2)) % 2) == 1
    qn = (jnp.where(comp1, jnp.zeros_like(q), q), jnp.where(comp1, q, jnp.zeros_like(q)))
    m_part, off = [None, None], 0
    for k_ref, _ in kv_refs:
        nk = k_ref.shape[0]
        for n in range(2):
            st = _dot_nt(k_ref[...], qn[n])
            s_refs[n][pl.ds(off, nk), :] = st
            mc = _row_fold(st, jnp.max)
            m_part[n] = mc if m_part[n] is None else jnp.maximum(m_part[n], mc)
        off += nk
    m = [jnp.max(mp, axis=0, keepdims=True) for mp in m_part]
    l_part, ot, off = [None, None], [None, None], 0
    for k_ref, vt_ref in kv_refs:
        for r0 in range(0, k_ref.shape[0], ATTN_KEY_CHUNK):
            for n in range(2):
                p = jnp.exp2(s_refs[n][pl.ds(off + r0, ATTN_KEY_CHUNK), :] - m[n])
                lc = _row_fold(p, jnp.sum)
                l_part[n] = lc if l_part[n] is None else l_part[n] + lc
                oc = _dot(vt_ref[:, pl.ds(r0, ATTN_KEY_CHUNK)], p.astype(BF16))
                ot[n] = oc if ot[n] is None else ot[n] + oc
        off += k_ref.shape[0]
    outs = [ot[n] / jnp.sum(l_part[n], axis=0, keepdims=True) for n in range(2)]
    ot = outs[0] - lam * outs[1]
    r = ot * lax.rsqrt(jnp.mean(ot * ot, axis=0, keepdims=True) + EPS)
    o_ref[...] = (r * g_ref[...] * (1.0 - lam_init)).T.astype(BF16)


def _attn_kernel(lam_ref, q_ref, kc_ref, vc_ref, kl_ref, vl_ref, g_ref, o_ref, s0_ref, s1_ref, *, lam_init, n_lat_q):
    is_lat = pl.program_id(2) < n_lat_q
    s_refs = (s0_ref, s1_ref)

    @pl.when(is_lat)
    def _():
        _attn_body(lam_ref, q_ref, ((kc_ref, vc_ref), (kl_ref, vl_ref)), g_ref, o_ref, s_refs, lam_init)

    @pl.when(jnp.logical_not(is_lat))
    def _():
        _attn_body(lam_ref, q_ref, ((kc_ref, vc_ref),), g_ref, o_ref, s_refs, lam_init)


def _attention(qk, vt, lamv, subln_g, *, lam_init, b, s, nc):
    tq = TM
    t = qk.shape[0]
    n_lat_q, n_ctx_q = s // tq, nc // tq
    ctx0 = (b * s) // nc

    def q_map(bi, h, i):
        return (jnp.where(i < n_lat_q, bi * n_lat_q + i, (b * s) // tq + bi * n_ctx_q + (i - n_lat_q)), h)

    return pl.pallas_call(
        functools.partial(_attn_kernel, lam_init=lam_init, n_lat_q=n_lat_q),
        grid=(b, A_HEADS, n_lat_q + n_ctx_q),
        in_specs=[pl.BlockSpec((SUBLANES, LANES), lambda bi, h, i: (0, 0)),
                  pl.BlockSpec((tq, LANES), q_map),
                  pl.BlockSpec((nc, LANES), lambda bi, h, i: (ctx0 + bi, A_HEADS + h)),
                  pl.BlockSpec((LANES, nc), lambda bi, h, i: (h, ctx0 + bi)),
                  pl.BlockSpec((s, LANES), lambda bi, h, i: (bi, A_HEADS + h)),
                  pl.BlockSpec((LANES, s), lambda bi, h, i: (h, bi)),
                  pl.BlockSpec((LANES, 1), lambda bi, h, i: (0, 0))],
        out_specs=pl.BlockSpec((tq, LANES), q_map),
        out_shape=jax.ShapeDtypeStruct((t, A_WIDTH), BF16),
        scratch_shapes=[pltpu.VMEM((nc + s, tq), F32)] * 2,
        compiler_params=_params("parallel", "parallel", "arbitrary"),
    )(lamv, qk, qk, vt, qk, vt, subln_g)


def _gdn_prep_kernel(g_ref, prev_ref, next_ref, ab_ref, cw_ref, gp_ref, q_ref, k_ref, v_ref, gc_ref, ext_ref,
                     *, n_lat_tiles, tiles_per_seq):
    i = pl.program_id(0)
    is_lat = i < n_lat_tiles
    first = jnp.logical_or(jnp.logical_not(is_lat), i % tiles_per_seq == 0)
    last = jnp.logical_or(jnp.logical_not(is_lat), i % tiles_per_seq == tiles_per_seq - 1)
    ext_ref[0:SUBLANES, :] = jnp.where(first, 0.0, prev_ref[...])
    ext_ref[SUBLANES:SUBLANES + TM, :] = g_ref[...]
    ext_ref[SUBLANES + TM:, :] = jnp.where(last, 0.0, next_ref[...])
    cw = cw_ref[...]
    acc = None
    for j in range(CONV_K):
        term = ext_ref[pl.ds(SUBLANES - CONV_K // 2 + j, TM), :] * cw[j:j + 1]
        acc = term if acc is None else acc + term
    y = _silu(acc)
    for h in range(G_HEADS):
        for base, ref, scale in ((0, q_ref, G_KEY_DIM ** -0.5), (G_QK_WIDTH, k_ref, 1.0)):
            xh = y[:, base + h * LANES:base + (h + 1) * LANES]
            nrm = xh * lax.rsqrt(jnp.sum(xh * xh, axis=-1, keepdims=True) + EPS)
            ref[:, h * LANES:(h + 1) * LANES] = nrm * scale
    v_ref[...] = y[:, 2 * G_QK_WIDTH:]

    ab = ab_ref[...]
    gp = gp_ref[...]
    lane = lax.broadcasted_iota(jnp.int32, (1, LANES), 1)
    z = ab + gp[1:2]
    softplus = jnp.maximum(z, 0.0) + jnp.log(1.0 + jnp.exp(-jnp.abs(z)))
    g = jnp.where(lane < 2 * G_HEADS, -jnp.exp(gp[0:1]) * softplus, 0.0)
    beta = _sigmoid(ab)
    ri = lax.broadcasted_iota(jnp.int32, (TM, TM), 0)
    ci = lax.broadcasted_iota(jnp.int32, (TM, TM), 1)
    same = (ri // CHUNK) == (ci // CHUNK)
    lower = jnp.where(jnp.logical_and(same, ci <= ri), 1.0, 0.0)
    upper = jnp.where(jnp.logical_and(same, ci >= ri), 1.0, 0.0)
    cum_f = _dot_f32(lower, g)
    cum_b = _dot_f32(upper, g)
    total = cum_f + cum_b - g
    cum = jnp.where(lane < G_HEADS, cum_f, cum_b)
    eg = jnp.exp(cum)
    ek = jnp.exp(total - cum)
    et = jnp.exp(total)
    out = jnp.where(lane < GC_BETA, cum, 0.0)
    out = jnp.where(jnp.logical_and(lane >= GC_BETA, lane < GC_EG), beta, out)
    out = jnp.where(jnp.logical_and(lane >= GC_EG, lane < GC_EK), pltpu.roll(eg, GC_EG, 1), out)
    out = jnp.where(jnp.logical_and(lane >= GC_EK, lane < GC_ET), pltpu.roll(ek, GC_EK, 1), out)
    out = jnp.where(jnp.logical_and(lane >= GC_ET, lane < GC_ET + 8), pltpu.roll(et, GC_ET, 1), out)
    gc_ref[...] = out


def _gdn_prep(g, ab, conv_w8, gparams, n_lat_tiles, tiles_per_seq):
    t, c = g.shape
    nblk8 = t // SUBLANES
    per = TM // SUBLANES
    row = lambda i: (i, 0)
    out_shape = [jax.ShapeDtypeStruct((t, G_QK_WIDTH), F32),
                 jax.ShapeDtypeStruct((t, G_QK_WIDTH), F32),
                 jax.ShapeDtypeStruct((t, G_V_WIDTH), F32),
                 jax.ShapeDtypeStruct((t, LANES), F32)]
    return pl.pallas_call(
        functools.partial(_gdn_prep_kernel, n_lat_tiles=n_lat_tiles, tiles_per_seq=tiles_per_seq),
        grid=(t // TM,),
        in_specs=[pl.BlockSpec((TM, c), row),
                  pl.BlockSpec((SUBLANES, c), lambda i: (jnp.maximum(i * per - 1, 0), 0)),
                  pl.BlockSpec((SUBLANES, c), lambda i: (jnp.minimum((i + 1) * per, nblk8 - 1), 0)),
                  pl.BlockSpec((TM, LANES), row),
                  pl.BlockSpec((SUBLANES, c), lambda i: (0, 0)),
                  pl.BlockSpec((SUBLANES, LANES), lambda i: (0, 0))],
        out_specs=[pl.BlockSpec((TM, s.shape[1]), row) for s in out_shape],
        out_shape=out_shape,
        scratch_shapes=[pltpu.VMEM((TM + 2 * SUBLANES, c), F32)],
        compiler_params=_params("parallel"),
    )(g, g, g, ab, conv_w8, gparams)


def _gdn_masks(c, reverse):
    ii = lax.broadcasted_iota(jnp.int32, (c, 2 * c), 0)
    lane = lax.broadcasted_iota(jnp.int32, (c, 2 * c), 1)
    left = lane < c
    jj = jnp.where(left, lane, lane - c)
    causal = (ii <= jj) if reverse else (ii >= jj)
    strict = (ii < jj) if reverse else (ii > jj)
    return dict(left=left, diag=ii == jj, causal=causal, strict=strict)


def _gdn_setup(q, k, v, gc, gct, col, mk):
    c = q.shape[0]
    gcol = gc[:, GC_CUM + col:GC_CUM + col + 1]
    beta = gc[:, GC_BETA + col:GC_BETA + col + 1]
    eg = gc[:, GC_EG + col:GC_EG + col + 1]
    ek = gc[:, GC_EK + col:GC_EK + col + 1]
    et = gc[0:1, GC_ET + col:GC_ET + col + 1]
    grow = gct[GC_CUM + col:GC_CUM + col + 1, :]
    decay = jnp.exp(jnp.where(mk["causal"], gcol - grow, -1e30))
    kb = k * beta
    k16 = k.astype(BF16)
    both = _dot_nt(jnp.concatenate([kb.astype(BF16), q.astype(BF16)], axis=0),
                   jnp.concatenate([k16, k16], axis=0))
    qk = jnp.where(jnp.logical_and(mk["causal"], mk["left"]), both[c:] * decay, 0.0)
    z = jnp.where(mk["left"], jnp.where(mk["diag"], 1.0, 0.0), jnp.where(mk["strict"], -(both[:c] * decay), 0.0))
    rhs = jnp.concatenate([v * beta, kb * eg], axis=1)
    return dict(z=z, qk=qk.astype(BF16), rhs=rhs, qd=(q * eg).astype(BF16), kdt=(k * ek).T.astype(BF16), et=et)


def _gdn_level(z, mk):
    z16 = z.astype(BF16)
    return jnp.where(mk["left"], z, 0.0) + _dot(z16, jnp.concatenate([jnp.zeros_like(z16), z16], axis=0))


def _gdn_solve(ch, mk):
    y16 = jnp.where(jnp.logical_and(mk["left"], jnp.logical_not(mk["diag"])), ch["z"], 0.0).astype(BF16)
    rhs16 = ch["rhs"].astype(BF16)
    sol = ch["rhs"] + _dot(y16, jnp.concatenate([rhs16, rhs16], axis=0))
    dv = sol.shape[1] // 2
    return sol[:, :dv], jnp.concatenate([sol[:, dv:].astype(BF16), ch["qd"]], axis=0)


def _gdn_kernel(qf_ref, kf_ref, vf_ref, gf_ref, qb_ref, kb_ref, vb_ref, gb_ref, of_ref, ob_ref, s_ref):
    @pl.when(pl.program_id(1) == 0)
    def _():
        s_ref[...] = jnp.zeros_like(s_ref)

    n_chunks = qf_ref.shape[0] // CHUNK
    masks = (_gdn_masks(CHUNK, False), _gdn_masks(CHUNK, True))
    dirs = ((qf_ref, kf_ref, vf_ref, gf_ref, of_ref), (qb_ref, kb_ref, vb_ref, gb_ref, ob_ref))
    state = [s_ref[i] for i in range(2 * G_HEADS)]
    for n0 in range(0, n_chunks, GDN_CHUNKS_PER_BATCH):
        chains = []
        for n in range(n0, n0 + GDN_CHUNKS_PER_BATCH):
            for d, (q_ref, k_ref, v_ref, g_ref, o_ref) in enumerate(dirs):
                rows = pl.ds((n_chunks - 1 - n if d else n) * CHUNK, CHUNK)
                gc = g_ref[rows, :]
                gct = jnp.concatenate([gc, gc], axis=0).T
                for h in range(G_HEADS):
                    cols = pl.ds(h * LANES, LANES)
                    ch = _gdn_setup(q_ref[rows, cols], k_ref[rows, cols], v_ref[rows, cols], gc, gct,
                                    d * G_HEADS + h, masks[d])
                    ch.update(d=d, h=h, rows=rows, cols=cols, out=o_ref)
                    chains.append(ch)
        for _ in range(int(math.log2(CHUNK))):
            for ch in chains:
                ch["z"] = _gdn_level(ch["z"], masks[ch["d"]])
        for ch in chains:
            ch["u"], ch["wq"] = _gdn_solve(ch, masks[ch["d"]])
        per_chunk = 2 * G_HEADS
        for c0 in range(0, len(chains), per_chunk):
            group = chains[c0:c0 + per_chunk]
            ws = [_dot(ch["wq"], state[ch["d"] * G_HEADS + ch["h"]].astype(BF16)) for ch in group]
            for ch, w in zip(group, ws):
                i = ch["d"] * G_HEADS + ch["h"]
                v_new = (ch["u"] - w[:CHUNK]).astype(BF16)
                o = w[CHUNK:] + _dot(ch["qk"], jnp.concatenate([v_new, v_new], axis=0))
                state[i] = state[i] * ch["et"] + _dot(ch["kdt"], v_new)
                ch["out"][ch["rows"], ch["cols"]] = o
    for i in range(2 * G_HEADS):
        s_ref[i] = state[i]


def _gdn(qg, kg, vg, gc, *, b, s, nc):
    t = qg.shape[0]
    tps = s // TM
    nct = nc // TM
    n_steps = nct + tps
    ctx_tile0 = (b * s) // TM

    def fwd(bi, i):
        return (jnp.where(i < nct, ctx_tile0 + bi * nct + i, bi * tps + (i - nct)), 0)

    def bwd(bi, i):
        return (jnp.where(i < nct, ctx_tile0 + bi * nct + (nct - 1 - i), bi * tps + (tps - 1 - (i - nct))), 0)

    def specs(row_map):
        return [pl.BlockSpec((TM, G_QK_WIDTH), row_map), pl.BlockSpec((TM, G_QK_WIDTH), row_map),
                pl.BlockSpec((TM, G_V_WIDTH), row_map), pl.BlockSpec((TM, LANES), row_map)]

    return pl.pallas_call(
        _gdn_kernel,
        grid=(b, n_steps),
        in_specs=specs(fwd) + specs(bwd),
        out_specs=[pl.BlockSpec((TM, G_V_WIDTH), fwd), pl.BlockSpec((TM, G_V_WIDTH), bwd)],
        out_shape=[jax.ShapeDtypeStruct((t, G_V_WIDTH), F32)] * 2,
        scratch_shapes=[pltpu.VMEM((2 * G_HEADS, G_KEY_DIM, G_VAL_DIM), F32)],
        compiler_params=_params("parallel", "arbitrary"),
    )(qg, kg, vg, gc, qg, kg, vg, gc)


def _layer_norm(r, g, b):
    mu = jnp.mean(r, axis=-1, keepdims=True)
    rc = r - mu
    var = jnp.mean(rc * rc, axis=-1, keepdims=True)
    return rc * lax.rsqrt(var + EPS) * g + b


def _merge_kernel(x_ref, ya_ref, of_ref, ob_ref, z_ref, gate_ref, mod_ref, wpa_ref, wpb_ref, wo_ref, on_ref,
                  lng_ref, lnb_ref, rw_ref, rb_ref, xo_ref, ht_ref, cls_ref, ew_ref, *, alpha):
    d = x_ref.shape[1]
    o = of_ref[...] + ob_ref[...]
    parts = []
    for h in range(G_HEADS):
        oh = o[:, h * LANES:(h + 1) * LANES]
        parts.append(oh * lax.rsqrt(jnp.mean(oh * oh, axis=-1, keepdims=True) + EPS) * on_ref[...])
    yb = jnp.concatenate(parts, axis=1) * _silu(z_ref[...])
    pa = _dot(ya_ref[...], wpa_ref[...])
    pb = _dot(yb.astype(BF16), wpb_ref[...])
    mix = gate_ref[:, :d] * pa + gate_ref[:, d:] * pb
    y = _dot(mix.astype(BF16), wo_ref[...])
    m = mod_ref[...]
    xn = _layer_norm(alpha * x_ref[...] + m[2:3] * y, lng_ref[...], lnb_ref[...])
    xo_ref[...] = xn
    h2 = xn * (1.0 + m[4:5]) + m[3:4]
    _to_token_tiles(ht_ref, h2)

    score = _sigmoid(_dot_nt(rw_ref[...], h2.astype(BF16)))
    sel = score + rb_ref[...]
    rows = [sel[e:e + 1] for e in range(N_EXPERTS)]
    best = None
    best_val = None
    for gi in range(N_GROUPS):
        a, b, c, dd = rows[gi * EXPERTS_PER_GROUP:(gi + 1) * EXPERTS_PER_GROUP]
        top2 = jnp.maximum(jnp.maximum(jnp.maximum(a + b, a + c), jnp.maximum(a + dd, b + c)),
                           jnp.maximum(b + dd, c + dd))
        if gi == 0:
            best, best_val = jnp.zeros(top2.shape, jnp.int32), top2
        else:
            upd = top2 > best_val
            best = jnp.where(upd, gi, best)
            best_val = jnp.where(upd, top2, best_val)
    in_group, in_score = [], []
    for j in range(EXPERTS_PER_GROUP):
        v = rows[(N_GROUPS - 1) * EXPERTS_PER_GROUP + j]
        sv = score[(N_GROUPS - 1) * EXPERTS_PER_GROUP + j:(N_GROUPS - 1) * EXPERTS_PER_GROUP + j + 1]
        for gi in range(N_GROUPS - 2, -1, -1):
            e = gi * EXPERTS_PER_GROUP + j
            v = jnp.where(best == gi, rows[e], v)
            sv = jnp.where(best == gi, score[e:e + 1], sv)
        in_group.append(v)
        in_score.append(sv)
    l0 = jnp.zeros(best.shape, jnp.int32)
    m0 = in_group[0]
    for j in range(1, EXPERTS_PER_GROUP):
        upd = in_group[j] > m0
        l0 = jnp.where(upd, j, l0)
        m0 = jnp.where(upd, in_group[j], m0)
    l1 = jnp.zeros(best.shape, jnp.int32)
    m1 = jnp.full(m0.shape, -jnp.inf, F32)
    for j in range(EXPERTS_PER_GROUP):
        upd = jnp.logical_and(l0 != j, in_group[j] > m1)
        l1 = jnp.where(upd, j, l1)
        m1 = jnp.where(upd, in_group[j], m1)
    lo = jnp.minimum(l0, l1)
    hi = jnp.maximum(l0, l1)
    w_lo = jnp.zeros(m0.shape, F32)
    w_hi = jnp.zeros(m0.shape, F32)
    for j in range(EXPERTS_PER_GROUP):
        w_lo = jnp.where(lo == j, in_score[j], w_lo)
        w_hi = jnp.where(hi == j, in_score[j], w_hi)
    pair = jnp.where(lo == 0, hi - 1, jnp.where(lo == 1, hi + 1, len(PAIRS) - 1))
    tot = w_lo + w_hi
    ri = lax.broadcasted_iota(jnp.int32, cls_ref.shape, 0)
    cls_ref[...] = jnp.where(ri == 0, best * len(PAIRS) + pair, 0)
    ew_ref[...] = jnp.where(ri == 0, w_lo / tot, jnp.where(ri == 1, w_hi / tot, 0.0))


def _merge(x_all, ya, o_f, o_b, z, gate, modt, wpa, wpb, wo, onorm, lng, lnb, rw_t, rb, *, alpha, t):
    d = x_all.shape[1]
    r = d // LANES
    row = lambda i: (i, 0)
    col = lambda i: (0, i)
    const = lambda i: (0, 0)
    out_shape = [jax.ShapeDtypeStruct((t, d), F32),
                 jax.ShapeDtypeStruct((t * r, LANES), F32),
                 jax.ShapeDtypeStruct((SUBLANES, t), jnp.int32),
                 jax.ShapeDtypeStruct((SUBLANES, t), F32)]
    return pl.pallas_call(
        functools.partial(_merge_kernel, alpha=alpha),
        grid=(t // TM,),
        in_specs=[pl.BlockSpec((TM, d), row),
                  pl.BlockSpec((TM, A_WIDTH), row),
                  pl.BlockSpec((TM, G_V_WIDTH), row),
                  pl.BlockSpec((TM, G_V_WIDTH), row),
                  pl.BlockSpec((TM, G_V_WIDTH), row),
                  pl.BlockSpec((TM, 2 * d), row),
                  pl.BlockSpec((None, 6, d), lambda i: (i, 0, 0)),
                  pl.BlockSpec(wpa.shape, const),
                  pl.BlockSpec(wpb.shape, const),
                  pl.BlockSpec(wo.shape, const),
                  pl.BlockSpec((1, LANES), const),
                  pl.BlockSpec((1, d), const),
                  pl.BlockSpec((1, d), const),
                  pl.BlockSpec(rw_t.shape, const),
                  pl.BlockSpec(rb.shape, const)],
        out_specs=[pl.BlockSpec((TM, d), row), pl.BlockSpec((TM * r, LANES), row),
                   pl.BlockSpec((SUBLANES, TM), col), pl.BlockSpec((SUBLANES, TM), col)],
        out_shape=out_shape,
        compiler_params=_params("parallel"),
    )(x_all, ya, o_f, o_b, z, gate, modt, wpa, wpb, wo, onorm, lng, lnb, rw_t, rb)


def _expert_kernel(ea_ref, eb_ref, nu_ref, nv_ref, src_ref, h_hbm, ws_ref, w1a_ref, w3a_ref, w2a_ref,
                   w1b_ref, w3b_ref, w2b_ref, y_hbm, xbuf, ybuf, gsem, ssem, *, r):
    i = pl.program_id(0)
    n_used = nu_ref[0]
    slot = i % 2

    def gather_copy(blk, buf, j):
        tok = src_ref[blk * MOE_BLOCK + j]
        return pltpu.make_async_copy(h_hbm.at[pl.ds(pl.multiple_of(tok * r, r), r), :],
                                     xbuf.at[buf, pl.ds(pl.multiple_of(j * r, r), r), :], gsem.at[buf])

    def scatter_copy(blk, buf, j):
        tok = src_ref[blk * MOE_BLOCK + j]
        return pltpu.make_async_copy(ybuf.at[buf, pl.ds(pl.multiple_of(j * r, r), r), :],
                                     y_hbm.at[pl.ds(pl.multiple_of(tok * r, r), r), :], ssem.at[buf])

    def for_slots(fn):
        def body(j, carry):
            fn(j)
            return carry
        lax.fori_loop(0, MOE_BLOCK, body, 0, unroll=DMA_GROUP)

    def for_token_slots(blk, fn):
        nv = nv_ref[blk]

        def body(g, carry):
            j0 = g * DMA_GROUP

            @pl.when(j0 + DMA_GROUP <= nv)
            def _():
                for k in range(DMA_GROUP):
                    fn(j0 + k)

            @pl.when(jnp.logical_and(j0 < nv, j0 + DMA_GROUP > nv))
            def _():
                for k in range(DMA_GROUP):
                    pl.when(j0 + k < nv)(functools.partial(fn, j0 + k))
            return carry
        lax.fori_loop(0, MOE_BLOCK // DMA_GROUP, body, 0)

    @pl.when(i == 0)
    def _():
        for_slots(lambda j: gather_copy(0, 0, j).start())

    @pl.when(i < n_used)
    def _():
        for_slots(lambda j: gather_copy(i, slot, j).wait())

        x = _from_token_tiles(xbuf.at[slot], MOE_BLOCK, r).astype(BF16)
        ws = ws_ref[...]

        nxt = jnp.minimum(i + 1, pl.num_programs(0) - 1)
        for j in range(MOE_BLOCK):
            gather_copy(nxt, 1 - slot, j).start()

        ga, ua = _dot(x, w1a_ref[...]), _dot(x, w3a_ref[...])
        gb, ub = _dot(x, w1b_ref[...]), _dot(x, w3b_ref[...])
        ya = _dot((_silu(ga) * ua).astype(BF16), w2a_ref[...])
        yb = _dot((_silu(gb) * ub).astype(BF16), w2b_ref[...])
        y = ws[:, 0:1] * ya + ws[:, 1:2] * yb

        _to_token_tiles(ybuf.at[slot], y)

        @pl.when(i >= 1)
        def _():
            for_token_slots(i - 1, lambda j: scatter_copy(i - 1, 1 - slot, j).wait())

        for_token_slots(i, lambda j: scatter_copy(i, slot, j).start())

        @pl.when(i == n_used - 1)
        def _():
            for_token_slots(i, lambda j: scatter_copy(i, slot, j).wait())
            for_slots(lambda j: gather_copy(nxt, 1 - slot, j).wait())


def _experts(ht, w_slot, block_ea, block_eb, n_used, n_valid, slot_src, w1, w3, w2, n_out_tokens):
    d, f = w1.shape[1], w1.shape[2]
    r = d // LANES
    n_slots = slot_src.shape[0]

    def wspec(shape, which):
        if which == 0:
            return pl.BlockSpec((None,) + shape, lambda i, ea, eb, nu, nv, src: (ea[i], 0, 0))
        return pl.BlockSpec((None,) + shape, lambda i, ea, eb, nu, nv, src: (eb[i], 0, 0))

    grid_spec = pltpu.PrefetchScalarGridSpec(
        num_scalar_prefetch=5,
        grid=(n_slots // MOE_BLOCK,),
        in_specs=[pl.BlockSpec(memory_space=pl.ANY),
                  pl.BlockSpec((MOE_BLOCK, 2), lambda i, ea, eb, nu, nv, src: (i, 0)),
                  wspec((d, f), 0), wspec((d, f), 0), wspec((f, d), 0),
                  wspec((d, f), 1), wspec((d, f), 1), wspec((f, d), 1)],
        out_specs=pl.BlockSpec(memory_space=pl.ANY),
        scratch_shapes=[pltpu.VMEM((2, MOE_BLOCK * r, LANES), F32),
                        pltpu.VMEM((2, MOE_BLOCK * r, LANES), F32),
                        pltpu.SemaphoreType.DMA((2,)),
                        pltpu.SemaphoreType.DMA((2,))],
    )
    return pl.pallas_call(
        functools.partial(_expert_kernel, r=r),
        grid_spec=grid_spec,
        out_shape=jax.ShapeDtypeStruct((n_out_tokens * r, LANES), F32),
        compiler_params=_params("arbitrary"),
    )(block_ea, block_eb, n_used, n_valid, slot_src, ht, w_slot, w1, w3, w2, w1, w3, w2)


def _ln2_kernel(x_ref, yt_ref, mod_ref, g_ref, b_ref, o_ref, *, alpha):
    tm, d = x_ref.shape
    y = _from_token_tiles(yt_ref, tm, d // LANES)
    m = mod_ref[...]
    o_ref[...] = _layer_norm(alpha * x_ref[...] + m[5:6] * y, g_ref[...], b_ref[...])


def _ln2(x_all, yt, modt, g, b, *, alpha):
    t, d = x_all.shape
    r = d // LANES
    row = lambda i: (i, 0)
    const = lambda i: (0, 0)
    return pl.pallas_call(
        functools.partial(_ln2_kernel, alpha=alpha),
        grid=(t // TM,),
        in_specs=[pl.BlockSpec((TM, d), row), pl.BlockSpec((TM * r, LANES), row),
                  pl.BlockSpec((None, 6, d), lambda i: (i, 0, 0)),
                  pl.BlockSpec((1, d), const), pl.BlockSpec((1, d), const)],
        out_specs=pl.BlockSpec((TM, d), row),
        out_shape=jax.ShapeDtypeStruct((t, d), F32),
        compiler_params=_params("parallel"),
    )(x_all, yt, modt, g, b)


def _rope_table(s, nc):
    n_freq = A_HEAD_DIM // 4
    rows = s // GRID_W
    row = jnp.repeat(jnp.arange(rows, dtype=F32), GRID_W)
    col = jnp.tile(jnp.arange(GRID_W, dtype=F32), rows)
    inv = ROPE_BASE ** (-jnp.arange(n_freq, dtype=F32) / n_freq)
    ang = jnp.stack([row[:, None] * inv, col[:, None] * inv], axis=1)
    cos = jnp.cos(ang).reshape(s, 1, 1, 2, n_freq)
    sin = jnp.sin(ang).reshape(s, 1, 1, 2, n_freq)
    cos = jnp.broadcast_to(cos, (s, 2, 2, 2, n_freq)).reshape(s, LANES)
    sign = jnp.array([-1.0, 1.0], F32).reshape(1, 2, 1, 1, 1)
    sin = jnp.broadcast_to(sin * sign, (s, 2, 2, 2, n_freq)).reshape(s, LANES)
    lat = jnp.concatenate([cos, sin], axis=1)
    ident = jnp.concatenate([jnp.ones((nc, LANES), F32), jnp.zeros((nc, LANES), F32)], axis=1)
    return jnp.concatenate([lat, ident], axis=0)


def _rope_perm(w):
    d = w.shape[0]
    n_freq = A_HEAD_DIM // 4
    return w.reshape(d, A_HEADS, 2, 2, 2, n_freq).transpose(0, 1, 4, 2, 3, 5).reshape(d, A_WIDTH)


def _pack_w_in(w):
    d = w.shape[0]
    sizes = (A_WIDTH, A_WIDTH, A_WIDTH, G_QKV_WIDTH, G_V_WIDTH, 2 * G_HEADS, 2 * G_HEADS)
    offs = [0]
    for sz in sizes:
        offs.append(offs[-1] + sz)
    wq, wk, wv, wg, wz, wa, wb = (w[:, offs[i]:offs[i + 1]] for i in range(len(sizes)))
    wgate = w[:, offs[-1]:]
    wab = jnp.concatenate([wa, wb, jnp.zeros((d, LANES - 4 * G_HEADS), w.dtype)], axis=1)
    cat = jnp.concatenate([_rope_perm(wq) * (A_HEAD_DIM ** -0.5), _rope_perm(wk), wv, wg, wz, wab, wgate], axis=1)
    return cat.astype(BF16)


def _pad_rows(a, rows):
    return jnp.concatenate([a, jnp.zeros((rows - a.shape[0],) + a.shape[1:], a.dtype)], axis=0)


def _pad_lanes(a):
    return jnp.concatenate([a, jnp.zeros(a.shape[:-1] + (LANES - a.shape[-1],), a.dtype)], axis=-1)


def _dispatch(cls, ew, t):
    onehot = (cls[:, None] == jnp.arange(N_CLASSES, dtype=jnp.int32)[None, :]).astype(jnp.int32)
    csum = jnp.cumsum(onehot, axis=0)
    rank = jnp.sum(csum * onehot, axis=1) - 1
    counts = csum[-1]
    padded = (counts + MOE_BLOCK - 1) // MOE_BLOCK * MOE_BLOCK
    pad_end = jnp.cumsum(padded)
    pad_start = pad_end - padded
    dest = pad_start[cls] + rank
    n_blocks = -(-t // MOE_BLOCK) + N_CLASSES
    per_tok = jnp.concatenate([jnp.arange(t, dtype=F32)[:, None], ew], axis=1)
    per_slot = jnp.zeros((n_blocks * MOE_BLOCK, per_tok.shape[1]), F32).at[dest].set(per_tok)
    slot_src = per_slot[:, 0].astype(jnp.int32)
    w_slot = per_slot[:, 1:]
    block0 = jnp.arange(n_blocks, dtype=jnp.int32) * MOE_BLOCK
    block_cls = jnp.minimum(jnp.searchsorted(pad_end, block0, side='right'), N_CLASSES - 1)
    n_valid = jnp.clip((pad_start + counts)[block_cls] - block0, 0, MOE_BLOCK).astype(jnp.int32)
    n_used = (pad_end[-1:] // MOE_BLOCK).astype(jnp.int32)
    return slot_src, w_slot, jnp.asarray(CLASS_EA)[block_cls], jnp.asarray(CLASS_EB)[block_cls], n_used, n_valid


def kernel(x, c, ctx, c_ctx, mod_w, mod_b, w_in, conv_w, lam_q1, lam_k1, lam_q2, lam_k2, subln_g, a_log, dt_bias,
           onorm_g, w_pa, w_pb, w_o, ln1_g, ln1_b, router_w, router_b, w_exp1, w_exp3, w_exp2, ln2_g, ln2_b):
    b, s, d = x.shape
    nc = ctx.shape[1]
    depth = mod_w.shape[0]
    assert s % TM == 0 and nc % TM == 0 and s % nc == 0 and s % GRID_W == 0
    assert d == SUBLANES * LANES
    t_lat, t_ctx = b * s, b * nc
    t = t_lat + t_ctx
    n_lat_tiles = t_lat // TM
    tiles_per_seq = s // TM
    alpha = (2.0 * depth) ** 0.25

    x_all = jnp.concatenate([x.reshape(t_lat, d), ctx.reshape(t_ctx, d)], axis=0)
    cc = _pad_rows(jnp.concatenate([c, c_ctx[None, :]], axis=0), SUBLANES)
    mods = _mod_tables(cc, mod_w, mod_b)
    tile_row = jnp.concatenate([jnp.arange(n_lat_tiles, dtype=jnp.int32) // tiles_per_seq,
                                jnp.full((t_ctx // TM,), b, jnp.int32)])
    cs_tab = _rope_table(s, nc)
    rw_t = router_w.T.astype(BF16)
    rb = router_b.reshape(N_EXPERTS, 1)

    for l in range(depth):
        lam_init = 0.8 - 0.6 * math.exp(-0.3 * l)
        modt = mods[l].reshape(SUBLANES, 6, d)[tile_row]
        qk, vt, g, z, ab, gate = _inproj(x_all, modt, _pack_w_in(w_in[l]), cs_tab, n_lat_tiles, tiles_per_seq)

        lamv = _pad_rows(_pad_lanes(jnp.stack([lam_q1[l], lam_k1[l], lam_q2[l], lam_k2[l]])), SUBLANES)
        ya = _attention(qk, vt, lamv, subln_g[l].reshape(LANES, 1), lam_init=lam_init, b=b, s=s, nc=nc)

        gparams = _pad_rows(_pad_lanes(jnp.stack([a_log[l].reshape(-1), dt_bias[l].reshape(-1)])), SUBLANES)
        qg, kg, vg, gc = _gdn_prep(g, ab, _pad_rows(conv_w[l], SUBLANES), gparams, n_lat_tiles, tiles_per_seq)
        o_f, o_b = _gdn(qg, kg, vg, gc, b=b, s=s, nc=nc)

        t_out = t_lat if l == depth - 1 else t
        x_all, ht, cls, ew = _merge(x_all, ya, o_f, o_b, z, gate, modt, w_pa[l].astype(BF16), w_pb[l].astype(BF16),
                                    w_o[l].astype(BF16), onorm_g[l].reshape(1, LANES), ln1_g[l].reshape(1, d),
                                    ln1_b[l].reshape(1, d), rw_t, rb, alpha=alpha, t=t_out)

        slot_src, w_slot, block_ea, block_eb, n_used, n_valid = _dispatch(cls[0], ew[:2].T, t_out)
        yt = _experts(ht, w_slot, block_ea, block_eb, n_used, n_valid, slot_src, w_exp1[l].astype(BF16),
                      w_exp3[l].astype(BF16), w_exp2[l].astype(BF16), t_out)
        x_all = _ln2(x_all, yt, modt, ln2_g[l].reshape(1, d), ln2_b[l].reshape(1, d), alpha=alpha)
    return x_all.reshape(b, s, d)
```

```python
import functools
import math

import numpy as np

import jax
import jax.numpy as jnp
from jax import lax
from jax.experimental import pallas as pl
from jax.experimental.pallas import tpu as pltpu

F32 = jnp.float32
BF16 = jnp.bfloat16

GRID_W = 64
A_HEADS = 4
A_HEAD_DIM = 64
A_WIDTH = A_HEADS * 2 * A_HEAD_DIM
G_HEADS = 4
G_KEY_DIM = 128
G_VAL_DIM = 128
G_QK_WIDTH = G_HEADS * G_KEY_DIM
G_V_WIDTH = G_HEADS * G_VAL_DIM
G_QKV_WIDTH = 2 * G_QK_WIDTH + G_V_WIDTH
CONV_K = 5
CHUNK = 64
N_EXPERTS = 16
N_GROUPS = 4
EXPERTS_PER_GROUP = N_EXPERTS // N_GROUPS
MOE_BLOCK = 256
ROPE_BASE = 10000.0
EPS = 1e-6

LANES = 128
SUBLANES = 8
TM = 256
DMA_GROUP = 8
GDN_CHUNKS_PER_BATCH = 2
ATTN_KEY_CHUNK = 256
ATTN_Q_SUBTILES = 4
VMEM_LIMIT = 56 * 1024 * 1024

OFF_QK = 0
OFF_VA = 2 * A_WIDTH
OFF_G = OFF_VA + A_WIDTH
OFF_Z = OFF_G + G_QKV_WIDTH
OFF_AB = OFF_Z + G_V_WIDTH
OFF_GATE = OFF_AB + LANES
GC_CUM, GC_BETA, GC_EG, GC_EK, GC_ET = 0, 8, 16, 24, 32

PAIRS = tuple((lo, hi) for lo in range(EXPERTS_PER_GROUP) for hi in range(lo + 1, EXPERTS_PER_GROUP))
N_CLASSES = N_GROUPS * len(PAIRS)
CLASS_EA = np.array([g * EXPERTS_PER_GROUP + lo for g in range(N_GROUPS) for lo, _ in PAIRS], np.int32)
CLASS_EB = np.array([g * EXPERTS_PER_GROUP + hi for g in range(N_GROUPS) for _, hi in PAIRS], np.int32)


def _sigmoid(x):
    return 1.0 / (1.0 + jnp.exp(-x))


def _silu(x):
    return x * _sigmoid(x)


def _dot(a, b):
    return jnp.dot(a, b, preferred_element_type=F32)


def _dot_nt(a, b):
    return lax.dot_general(a, b, (((1,), (1,)), ((), ())), preferred_element_type=F32)


def _dot_f32(a, b):
    return jnp.dot(a, b, preferred_element_type=F32, precision=lax.Precision.HIGHEST)


def _params(*sem):
    return pltpu.CompilerParams(dimension_semantics=sem, vmem_limit_bytes=VMEM_LIMIT)


def _to_token_tiles(ref, x):
    rows, width = x.shape
    r = width // LANES
    for c in range(r):
        ref[pl.ds(c, rows, stride=r), :] = x[:, c * LANES:(c + 1) * LANES]


def _from_token_tiles(ref, rows, r):
    return jnp.concatenate([ref[pl.ds(c, rows, stride=r), :] for c in range(r)], axis=-1)


def _mod_kernel(cc_ref, w_ref, b_ref, o_ref):
    h = _silu(cc_ref[...])
    o_ref[...] = _dot(h.astype(BF16), w_ref[...].astype(BF16)) + b_ref[...]


def _mod_tables(cc, mod_w, mod_b):
    depth, d, n = mod_w.shape
    tn = n // 6
    return pl.pallas_call(
        _mod_kernel,
        grid=(depth, n // tn),
        in_specs=[pl.BlockSpec((SUBLANES, d), lambda l, j: (0, 0)),
                  pl.BlockSpec((None, d, tn), lambda l, j: (l, 0, j)),
                  pl.BlockSpec((None, 1, tn), lambda l, j: (l, 0, j))],
        out_specs=pl.BlockSpec((None, SUBLANES, tn), lambda l, j: (l, 0, j)),
        out_shape=jax.ShapeDtypeStruct((depth, SUBLANES, n), F32),
        compiler_params=_params("parallel", "parallel"),
    )(cc, mod_w, mod_b.reshape(depth, 1, n))


def _inproj_kernel(x_ref, mod_ref, w_ref, cs_ref, qk_ref, va_ref, g_ref, z_ref, ab_ref, gate_ref):
    m = mod_ref[...]
    h = (x_ref[...] * (1.0 + m[1:2]) + m[0:1]).astype(BF16)
    cs = cs_ref[...]
    cos, sin = cs[:, :LANES], cs[:, LANES:]
    qk = _dot(h, w_ref[:, OFF_QK:OFF_VA])
    for j in range(2 * A_HEADS):
        a = qk[:, j * LANES:(j + 1) * LANES]
        qk_ref[:, j * LANES:(j + 1) * LANES] = (a * cos + pltpu.roll(a, LANES // 2, 1) * sin).astype(BF16)
    va = _dot(h, w_ref[:, OFF_VA:OFF_G])
    for j in range(A_HEADS):
        va_ref[j * LANES:(j + 1) * LANES, :] = va[:, j * LANES:(j + 1) * LANES].T.astype(BF16)
    g_ref[...] = _dot(h, w_ref[:, OFF_G:OFF_Z])
    z_ref[...] = _dot(h, w_ref[:, OFF_Z:OFF_AB])
    ab_ref[...] = _dot(h, w_ref[:, OFF_AB:OFF_GATE])
    gate_ref[...] = _sigmoid(_dot(h, w_ref[:, OFF_GATE:])).astype(BF16)


def _inproj(x_all, modt, w_cat, cs_tab, n_lat_tiles, tiles_per_seq):
    t, d = x_all.shape
    n = w_cat.shape[1]
    d_gate = n - OFF_GATE

    def row(i):
        return (i, 0)

    def cs_map(i):
        return (jnp.where(i < n_lat_tiles, i % tiles_per_seq, tiles_per_seq), 0)

    out_shape = [jax.ShapeDtypeStruct((t, 2 * A_WIDTH), BF16),
                 jax.ShapeDtypeStruct((A_WIDTH, t), BF16),
                 jax.ShapeDtypeStruct((t, G_QKV_WIDTH), F32),
                 jax.ShapeDtypeStruct((t, G_V_WIDTH), F32),
                 jax.ShapeDtypeStruct((t, LANES), F32),
                 jax.ShapeDtypeStruct((t, d_gate), BF16)]
    out_specs = [pl.BlockSpec((TM, s.shape[1]), row) for s in out_shape]
    out_specs[1] = pl.BlockSpec((A_WIDTH, TM), lambda i: (0, i))
    return pl.pallas_call(
        _inproj_kernel,
        grid=(t // TM,),
        in_specs=[pl.BlockSpec((TM, d), row),
                  pl.BlockSpec((None, 6, d), lambda i: (i, 0, 0)),
                  pl.BlockSpec((d, n), lambda i: (0, 0)),
                  pl.BlockSpec((TM, 2 * LANES), cs_map)],
        out_specs=out_specs,
        out_shape=out_shape,
        compiler_params=_params("parallel"),
    )(x_all, modt, w_cat, cs_tab)


def _row_fold(x, op):
    rows, n = x.shape
    return op(x.reshape(rows // SUBLANES, SUBLANES, n), axis=0)


def _attn_scores(q, kv_refs, s_refs):
    lane = lax.broadcasted_iota(jnp.int32, (1, LANES), 1)
    comp1 = ((lane // (A_HEAD_DIM // 2)) % 2) == 1
    qn = (jnp.where(comp1, jnp.zeros_like(q), q), jnp.where(comp1, q, jnp.zeros_like(q)))
    m_part, off = [None, None], 0
    for k_ref, _ in kv_refs:
        nk = k_ref.shape[0]
        for n in range(2):
            st = _dot_nt(k_ref[...], qn[n])
            s_refs[n][pl.ds(off, nk), :] = st
            mc = _row_fold(st, jnp.max)
            m_part[n] = mc if m_part[n] is None else jnp.maximum(m_part[n], mc)
        off += nk
    return [jnp.max(mp, axis=0, keepdims=True) for mp in m_part]


def _attn_values(lam_ref, kv_refs, g_ref, s_refs, m, lam_init):
    lv = lam_ref[...]
    lam = (jnp.exp(jnp.sum(lv[0:1] * lv[1:2], axis=-1, keepdims=True))
           - jnp.exp(jnp.sum(lv[2:3] * lv[3:4], axis=-1, keepdims=True)) + lam_init)
    l_part, ot, off = [None, None], [None, None], 0
    for k_ref, vt_ref in kv_refs:
        for r0 in range(0, k_ref.shape[0], ATTN_KEY_CHUNK):
            for n in range(2):
                p = jnp.exp(s_refs[n][pl.ds(off + r0, ATTN_KEY_CHUNK), :] - m[n])
                lc = _row_fold(p, jnp.sum)
                l_part[n] = lc if l_part[n] is None else l_part[n] + lc
                oc = _dot(vt_ref[:, pl.ds(r0, ATTN_KEY_CHUNK)], p.astype(BF16))
                ot[n] = oc if ot[n] is None else ot[n] + oc
        off += k_ref.shape[0]
    outs = [ot[n] / jnp.sum(l_part[n], axis=0, keepdims=True) for n in range(2)]
    ot = outs[0] - lam * outs[1]
    r = ot * lax.rsqrt(jnp.mean(ot * ot, axis=0, keepdims=True) + EPS)
    return (r * g_ref[...] * (1.0 - lam_init)).T.astype(BF16)


def _attn_kernel(lam_ref, q_ref, *refs, lam_init, n_kv, n_sub):
    kv_refs = tuple((refs[2 * j], refs[2 * j + 1]) for j in range(n_kv))
    g_ref, o_ref = refs[2 * n_kv], refs[2 * n_kv + 1]
    s_all = refs[2 * n_kv + 2:]
    tq = q_ref.shape[0] // n_sub
    def scores(j):
        s_refs = (s_all[2 * j], s_all[2 * j + 1])
        return s_refs, _attn_scores(q_ref[pl.ds(j * tq, tq), :], kv_refs, s_refs)

    nxt = scores(0)
    for j in range(n_sub):
        s_refs, m = nxt
        if j + 1 < n_sub:
            nxt = scores(j + 1)
        o_ref[pl.ds(j * tq, tq), :] = _attn_values(lam_ref, kv_refs, g_ref, s_refs, m, lam_init)


def _attention(qk, vt, lamv, subln_g, *, lam_init, b, s, nc, latent):
    n_sub = ATTN_Q_SUBTILES if latent else 1
    rows = TM * n_sub
    ctx0 = (b * s) // nc
    kv_specs = [pl.BlockSpec((nc, LANES), lambda bi, h, i: (ctx0 + bi, A_HEADS + h)),
                pl.BlockSpec((LANES, nc), lambda bi, h, i: (h, ctx0 + bi))]
    kv_args = [qk, vt]
    if latent:
        n_q = s // rows
        q_map = lambda bi, h, i: (bi * n_q + i, h)
        kv_specs += [pl.BlockSpec((s, LANES), lambda bi, h, i: (bi, A_HEADS + h)),
                     pl.BlockSpec((LANES, s), lambda bi, h, i: (h, bi))]
        kv_args += [qk, vt]
        n_keys, t_out = nc + s, b * s
    else:
        n_q = nc // rows
        q_map = lambda bi, h, i: ((b * s) // rows + bi * n_q + i, h)
        n_keys, t_out = nc, b * nc
    return pl.pallas_call(
        functools.partial(_attn_kernel, lam_init=lam_init, n_kv=len(kv_args) // 2, n_sub=n_sub),
        grid=(b, A_HEADS, n_q),
        in_specs=[pl.BlockSpec((SUBLANES, LANES), lambda bi, h, i: (0, 0)),
                  pl.BlockSpec((rows, LANES), q_map)] + kv_specs
                 + [pl.BlockSpec((LANES, 1), lambda bi, h, i: (0, 0))],
        out_specs=pl.BlockSpec((rows, LANES), lambda bi, h, i: (bi * n_q + i, h)),
        out_shape=jax.ShapeDtypeStruct((t_out, A_WIDTH), BF16),
        scratch_shapes=[pltpu.VMEM((n_keys, TM), F32)] * (2 * n_sub),
        compiler_params=_params("parallel", "parallel", "arbitrary"),
    )(lamv, qk, *kv_args, subln_g)


def _gdn_prep_kernel(g_ref, prev_ref, next_ref, ab_ref, cw_ref, gp_ref, q_ref, k_ref, v_ref, gc_ref, ext_ref,
                     *, n_lat_tiles, tiles_per_seq):
    i = pl.program_id(0)
    is_lat = i < n_lat_tiles
    first = jnp.logical_or(jnp.logical_not(is_lat), i % tiles_per_seq == 0)
    last = jnp.logical_or(jnp.logical_not(is_lat), i % tiles_per_seq == tiles_per_seq - 1)
    ext_ref[0:SUBLANES, :] = jnp.where(first, 0.0, prev_ref[...])
    ext_ref[SUBLANES:SUBLANES + TM, :] = g_ref[...]
    ext_ref[SUBLANES + TM:, :] = jnp.where(last, 0.0, next_ref[...])
    cw = cw_ref[...]
    acc = None
    for j in range(CONV_K):
        term = ext_ref[pl.ds(SUBLANES - CONV_K // 2 + j, TM), :] * cw[j:j + 1]
        acc = term if acc is None else acc + term
    y = _silu(acc)
    for h in range(G_HEADS):
        for base, ref, scale in ((0, q_ref, G_KEY_DIM ** -0.5), (G_QK_WIDTH, k_ref, 1.0)):
            xh = y[:, base + h * LANES:base + (h + 1) * LANES]
            nrm = xh * lax.rsqrt(jnp.sum(xh * xh, axis=-1, keepdims=True) + EPS)
            ref[:, h * LANES:(h + 1) * LANES] = nrm * scale
    v_ref[...] = y[:, 2 * G_QK_WIDTH:]

    ab = ab_ref[...]
    gp = gp_ref[...]
    lane = lax.broadcasted_iota(jnp.int32, (1, LANES), 1)
    z = ab + gp[1:2]
    softplus = jnp.maximum(z, 0.0) + jnp.log(1.0 + jnp.exp(-jnp.abs(z)))
    g = jnp.where(lane < 2 * G_HEADS, -jnp.exp(gp[0:1]) * softplus, 0.0)
    beta = _sigmoid(ab)
    ri = lax.broadcasted_iota(jnp.int32, (TM, TM), 0)
    ci = lax.broadcasted_iota(jnp.int32, (TM, TM), 1)
    same = (ri // CHUNK) == (ci // CHUNK)
    lower = jnp.where(jnp.logical_and(same, ci <= ri), 1.0, 0.0)
    upper = jnp.where(jnp.logical_and(same, ci >= ri), 1.0, 0.0)
    cum_f = _dot_f32(lower, g)
    cum_b = _dot_f32(upper, g)
    total = cum_f + cum_b - g
    cum = jnp.where(lane < G_HEADS, cum_f, cum_b)
    eg = jnp.exp(cum)
    ek = jnp.exp(total - cum)
    et = jnp.exp(total)
    out = jnp.where(lane < GC_BETA, cum, 0.0)
    out = jnp.where(jnp.logical_and(lane >= GC_BETA, lane < GC_EG), beta, out)
    out = jnp.where(jnp.logical_and(lane >= GC_EG, lane < GC_EK), pltpu.roll(eg, GC_EG, 1), out)
    out = jnp.where(jnp.logical_and(lane >= GC_EK, lane < GC_ET), pltpu.roll(ek, GC_EK, 1), out)
    out = jnp.where(jnp.logical_and(lane >= GC_ET, lane < GC_ET + 8), pltpu.roll(et, GC_ET, 1), out)
    gc_ref[...] = out


def _gdn_prep(g, ab, conv_w8, gparams, n_lat_tiles, tiles_per_seq):
    t, c = g.shape
    nblk8 = t // SUBLANES
    per = TM // SUBLANES
    row = lambda i: (i, 0)
    out_shape = [jax.ShapeDtypeStruct((t, G_QK_WIDTH), F32),
                 jax.ShapeDtypeStruct((t, G_QK_WIDTH), F32),
                 jax.ShapeDtypeStruct((t, G_V_WIDTH), F32),
                 jax.ShapeDtypeStruct((t, LANES), F32)]
    return pl.pallas_call(
        functools.partial(_gdn_prep_kernel, n_lat_tiles=n_lat_tiles, tiles_per_seq=tiles_per_seq),
        grid=(t // TM,),
        in_specs=[pl.BlockSpec((TM, c), row),
                  pl.BlockSpec((SUBLANES, c), lambda i: (jnp.maximum(i * per - 1, 0), 0)),
                  pl.BlockSpec((SUBLANES, c), lambda i: (jnp.minimum((i + 1) * per, nblk8 - 1), 0)),
                  pl.BlockSpec((TM, LANES), row),
                  pl.BlockSpec((SUBLANES, c), lambda i: (0, 0)),
                  pl.BlockSpec((SUBLANES, LANES), lambda i: (0, 0))],
        out_specs=[pl.BlockSpec((TM, s.shape[1]), row) for s in out_shape],
        out_shape=out_shape,
        scratch_shapes=[pltpu.VMEM((TM + 2 * SUBLANES, c), F32)],
        compiler_params=_params("parallel"),
    )(g, g, g, ab, conv_w8, gparams)


def _gdn_masks(c, reverse):
    ii = lax.broadcasted_iota(jnp.int32, (c, 2 * c), 0)
    lane = lax.broadcasted_iota(jnp.int32, (c, 2 * c), 1)
    left = lane < c
    jj = jnp.where(left, lane, lane - c)
    causal = (ii <= jj) if reverse else (ii >= jj)
    strict = (ii < jj) if reverse else (ii > jj)
    return dict(left=left, diag=ii == jj, causal=causal, strict=strict)


def _gdn_setup(q, k, v, gc, gct, col, mk):
    c = q.shape[0]
    gcol = gc[:, GC_CUM + col:GC_CUM + col + 1]
    beta = gc[:, GC_BETA + col:GC_BETA + col + 1]
    eg = gc[:, GC_EG + col:GC_EG + col + 1]
    ek = gc[:, GC_EK + col:GC_EK + col + 1]
    et = gc[0:1, GC_ET + col:GC_ET + col + 1]
    grow = gct[GC_CUM + col:GC_CUM + col + 1, :]
    decay = jnp.exp(jnp.where(mk["causal"], gcol - grow, -1e30))
    kb = k * beta
    k16 = k.astype(BF16)
    both = _dot_nt(jnp.concatenate([kb.astype(BF16), q.astype(BF16)], axis=0),
                   jnp.concatenate([k16, k16], axis=0))
    qk = jnp.where(jnp.logical_and(mk["causal"], mk["left"]), both[c:] * decay, 0.0)
    z = jnp.where(mk["left"], jnp.where(mk["diag"], 1.0, 0.0), jnp.where(mk["strict"], -(both[:c] * decay), 0.0))
    rhs = jnp.concatenate([v * beta, kb * eg], axis=1)
    return dict(z=z, qk=qk.astype(BF16), rhs=rhs, qd=(q * eg).astype(BF16), kdt=(k * ek).T.astype(BF16), et=et)


def _gdn_level(z, mk):
    z16 = z.astype(BF16)
    return jnp.where(mk["left"], z, 0.0) + _dot(z16, jnp.concatenate([jnp.zeros_like(z16), z16], axis=0))


def _gdn_solve(ch, mk):
    y16 = jnp.where(jnp.logical_and(mk["left"], jnp.logical_not(mk["diag"])), ch["z"], 0.0).astype(BF16)
    rhs16 = ch["rhs"].astype(BF16)
    sol = ch["rhs"] + _dot(y16, jnp.concatenate([rhs16, rhs16], axis=0))
    dv = sol.shape[1] // 2
    return sol[:, :dv], jnp.concatenate([sol[:, dv:].astype(BF16), ch["qd"]], axis=0)


def _gdn_kernel(qf_ref, kf_ref, vf_ref, gf_ref, qb_ref, kb_ref, vb_ref, gb_ref, of_ref, ob_ref, s_ref):
    @pl.when(pl.program_id(1) == 0)
    def _():
        s_ref[...] = jnp.zeros_like(s_ref)

    n_chunks = qf_ref.shape[0] // CHUNK
    masks = (_gdn_masks(CHUNK, False), _gdn_masks(CHUNK, True))
    dirs = ((qf_ref, kf_ref, vf_ref, gf_ref, of_ref), (qb_ref, kb_ref, vb_ref, gb_ref, ob_ref))
    state = [s_ref[i] for i in range(2 * G_HEADS)]
    for n0 in range(0, n_chunks, GDN_CHUNKS_PER_BATCH):
        chains = []
        for n in range(n0, n0 + GDN_CHUNKS_PER_BATCH):
            for d, (q_ref, k_ref, v_ref, g_ref, o_ref) in enumerate(dirs):
                rows = pl.ds((n_chunks - 1 - n if d else n) * CHUNK, CHUNK)
                gc = g_ref[rows, :]
                gct = jnp.concatenate([gc, gc], axis=0).T
                for h in range(G_HEADS):
                    cols = pl.ds(h * LANES, LANES)
                    ch = _gdn_setup(q_ref[rows, cols], k_ref[rows, cols], v_ref[rows, cols], gc, gct,
                                    d * G_HEADS + h, masks[d])
                    ch.update(d=d, h=h, rows=rows, cols=cols, out=o_ref)
                    chains.append(ch)
        for _ in range(int(math.log2(CHUNK))):
            for ch in chains:
                ch["z"] = _gdn_level(ch["z"], masks[ch["d"]])
        for ch in chains:
            ch["u"], ch["wq"] = _gdn_solve(ch, masks[ch["d"]])
        per_chunk = 2 * G_HEADS
        for c0 in range(0, len(chains), per_chunk):
            group = chains[c0:c0 + per_chunk]
            ws = [_dot(ch["wq"], state[ch["d"] * G_HEADS + ch["h"]].astype(BF16)) for ch in group]
            for ch, w in zip(group, ws):
                i = ch["d"] * G_HEADS + ch["h"]
                v_new = (ch["u"] - w[:CHUNK]).astype(BF16)
                o = w[CHUNK:] + _dot(ch["qk"], jnp.concatenate([v_new, v_new], axis=0))
                state[i] = state[i] * ch["et"] + _dot(ch["kdt"], v_new)
                ch["out"][ch["rows"], ch["cols"]] = o
    for i in range(2 * G_HEADS):
        s_ref[i] = state[i]


def _gdn(qg, kg, vg, gc, *, b, s, nc):
    t = qg.shape[0]
    tps = s // TM
    nct = nc // TM
    n_steps = nct + tps
    ctx_tile0 = (b * s) // TM

    def fwd(bi, i):
        return (jnp.where(i < nct, ctx_tile0 + bi * nct + i, bi * tps + (i - nct)), 0)

    def bwd(bi, i):
        return (jnp.where(i < nct, ctx_tile0 + bi * nct + (nct - 1 - i), bi * tps + (tps - 1 - (i - nct))), 0)

    def specs(row_map):
        return [pl.BlockSpec((TM, G_QK_WIDTH), row_map), pl.BlockSpec((TM, G_QK_WIDTH), row_map),
                pl.BlockSpec((TM, G_V_WIDTH), row_map), pl.BlockSpec((TM, LANES), row_map)]

    return pl.pallas_call(
        _gdn_kernel,
        grid=(b, n_steps),
        in_specs=specs(fwd) + specs(bwd),
        out_specs=[pl.BlockSpec((TM, G_V_WIDTH), fwd), pl.BlockSpec((TM, G_V_WIDTH), bwd)],
        out_shape=[jax.ShapeDtypeStruct((t, G_V_WIDTH), F32)] * 2,
        scratch_shapes=[pltpu.VMEM((2 * G_HEADS, G_KEY_DIM, G_VAL_DIM), F32)],
        compiler_params=_params("parallel", "arbitrary"),
    )(qg, kg, vg, gc, qg, kg, vg, gc)


def _layer_norm(r, g, b):
    mu = jnp.mean(r, axis=-1, keepdims=True)
    rc = r - mu
    var = jnp.mean(rc * rc, axis=-1, keepdims=True)
    return rc * lax.rsqrt(var + EPS) * g + b


def _merge_kernel(x_ref, yal_ref, yac_ref, of_ref, ob_ref, z_ref, gate_ref, mod_ref, wpa_ref, wpb_ref, wo_ref, on_ref,
                  lng_ref, lnb_ref, rw_ref, rb_ref, xo_ref, ht_ref, cls_ref, ew_ref, *, alpha, n_lat_tiles):
    d = x_ref.shape[1]
    o = of_ref[...] + ob_ref[...]
    parts = []
    for h in range(G_HEADS):
        oh = o[:, h * LANES:(h + 1) * LANES]
        parts.append(oh * lax.rsqrt(jnp.mean(oh * oh, axis=-1, keepdims=True) + EPS) * on_ref[...])
    yb = jnp.concatenate(parts, axis=1) * _silu(z_ref[...])
    ya = jnp.where(pl.program_id(0) < n_lat_tiles, yal_ref[...], yac_ref[...])
    pa = _dot(ya, wpa_ref[...])
    pb = _dot(yb.astype(BF16), wpb_ref[...])
    mix = gate_ref[:, :d] * pa + gate_ref[:, d:] * pb
    y = _dot(mix.astype(BF16), wo_ref[...])
    m = mod_ref[...]
    xn = _layer_norm(alpha * x_ref[...] + m[2:3] * y, lng_ref[...], lnb_ref[...])
    xo_ref[...] = xn
    h2 = xn * (1.0 + m[4:5]) + m[3:4]
    _to_token_tiles(ht_ref, h2)

    score = _sigmoid(_dot_nt(rw_ref[...], h2.astype(BF16)))
    sel = score + rb_ref[...]
    rows = [sel[e:e + 1] for e in range(N_EXPERTS)]
    best = None
    best_val = None
    for gi in range(N_GROUPS):
        a, b, c, dd = rows[gi * EXPERTS_PER_GROUP:(gi + 1) * EXPERTS_PER_GROUP]
        top2 = jnp.maximum(jnp.maximum(jnp.maximum(a + b, a + c), jnp.maximum(a + dd, b + c)),
                           jnp.maximum(b + dd, c + dd))
        if gi == 0:
            best, best_val = jnp.zeros(top2.shape, jnp.int32), top2
        else:
            upd = top2 > best_val
            best = jnp.where(upd, gi, best)
            best_val = jnp.where(upd, top2, best_val)
    in_group, in_score = [], []
    for j in range(EXPERTS_PER_GROUP):
        v = rows[(N_GROUPS - 1) * EXPERTS_PER_GROUP + j]
        sv = score[(N_GROUPS - 1) * EXPERTS_PER_GROUP + j:(N_GROUPS - 1) * EXPERTS_PER_GROUP + j + 1]
        for gi in range(N_GROUPS - 2, -1, -1):
            e = gi * EXPERTS_PER_GROUP + j
            v = jnp.where(best == gi, rows[e], v)
            sv = jnp.where(best == gi, score[e:e + 1], sv)
        in_group.append(v)
        in_score.append(sv)
    l0 = jnp.zeros(best.shape, jnp.int32)
    m0 = in_group[0]
    for j in range(1, EXPERTS_PER_GROUP):
        upd = in_group[j] > m0
        l0 = jnp.where(upd, j, l0)
        m0 = jnp.where(upd, in_group[j], m0)
    l1 = jnp.zeros(best.shape, jnp.int32)
    m1 = jnp.full(m0.shape, -jnp.inf, F32)
    for j in range(EXPERTS_PER_GROUP):
        upd = jnp.logical_and(l0 != j, in_group[j] > m1)
        l1 = jnp.where(upd, j, l1)
        m1 = jnp.where(upd, in_group[j], m1)
    lo = jnp.minimum(l0, l1)
    hi = jnp.maximum(l0, l1)
    w_lo = jnp.zeros(m0.shape, F32)
    w_hi = jnp.zeros(m0.shape, F32)
    for j in range(EXPERTS_PER_GROUP):
        w_lo = jnp.where(lo == j, in_score[j], w_lo)
        w_hi = jnp.where(hi == j, in_score[j], w_hi)
    pair = jnp.where(lo == 0, hi - 1, jnp.where(lo == 1, hi + 1, len(PAIRS) - 1))
    tot = w_lo + w_hi
    ri = lax.broadcasted_iota(jnp.int32, cls_ref.shape, 0)
    cls_ref[...] = jnp.where(ri == 0, best * len(PAIRS) + pair, 0)
    ew_ref[...] = jnp.where(ri == 0, w_lo / tot, jnp.where(ri == 1, w_hi / tot, 0.0))


def _merge(x_all, ya_lat, ya_ctx, o_f, o_b, z, gate, modt, wpa, wpb, wo, onorm, lng, lnb, rw_t, rb, *, alpha, t):
    d = x_all.shape[1]
    r = d // LANES
    n_lat_tiles = ya_lat.shape[0] // TM
    row = lambda i: (i, 0)
    col = lambda i: (0, i)
    const = lambda i: (0, 0)
    out_shape = [jax.ShapeDtypeStruct((t, d), F32),
                 jax.ShapeDtypeStruct((t * r, LANES), F32),
                 jax.ShapeDtypeStruct((SUBLANES, t), jnp.int32),
                 jax.ShapeDtypeStruct((SUBLANES, t), F32)]
    return pl.pallas_call(
        functools.partial(_merge_kernel, alpha=alpha, n_lat_tiles=n_lat_tiles),
        grid=(t // TM,),
        in_specs=[pl.BlockSpec((TM, d), row),
                  pl.BlockSpec((TM, A_WIDTH), lambda i: (jnp.minimum(i, n_lat_tiles - 1), 0)),
                  pl.BlockSpec((TM, A_WIDTH), lambda i: (jnp.maximum(i - n_lat_tiles, 0), 0)),
                  pl.BlockSpec((TM, G_V_WIDTH), row),
                  pl.BlockSpec((TM, G_V_WIDTH), row),
                  pl.BlockSpec((TM, G_V_WIDTH), row),
                  pl.BlockSpec((TM, 2 * d), row),
                  pl.BlockSpec((None, 6, d), lambda i: (i, 0, 0)),
                  pl.BlockSpec(wpa.shape, const),
                  pl.BlockSpec(wpb.shape, const),
                  pl.BlockSpec(wo.shape, const),
                  pl.BlockSpec((1, LANES), const),
                  pl.BlockSpec((1, d), const),
                  pl.BlockSpec((1, d), const),
                  pl.BlockSpec(rw_t.shape, const),
                  pl.BlockSpec(rb.shape, const)],
        out_specs=[pl.BlockSpec((TM, d), row), pl.BlockSpec((TM * r, LANES), row),
                   pl.BlockSpec((SUBLANES, TM), col), pl.BlockSpec((SUBLANES, TM), col)],
        out_shape=out_shape,
        compiler_params=_params("parallel"),
    )(x_all, ya_lat, ya_ctx, o_f, o_b, z, gate, modt, wpa, wpb, wo, onorm, lng, lnb, rw_t, rb)


def _expert_kernel(ea_ref, eb_ref, nu_ref, nv_ref, src_ref, h_hbm, ws_ref, w1a_ref, w3a_ref, w2a_ref,
                   w1b_ref, w3b_ref, w2b_ref, y_hbm, xbuf, ybuf, gsem, ssem, *, r):
    i = pl.program_id(0)
    n_used = nu_ref[0]
    slot = i % 2

    def gather_copy(blk, buf, j):
        tok = src_ref[blk * MOE_BLOCK + j]
        return pltpu.make_async_copy(h_hbm.at[pl.ds(pl.multiple_of(tok * r, r), r), :],
                                     xbuf.at[buf, pl.ds(pl.multiple_of(j * r, r), r), :], gsem.at[buf])

    def scatter_copy(blk, buf, j):
        tok = src_ref[blk * MOE_BLOCK + j]
        return pltpu.make_async_copy(ybuf.at[buf, pl.ds(pl.multiple_of(j * r, r), r), :],
                                     y_hbm.at[pl.ds(pl.multiple_of(tok * r, r), r), :], ssem.at[buf])

    def for_slots(fn):
        def body(j, carry):
            fn(j)
            return carry
        lax.fori_loop(0, MOE_BLOCK, body, 0, unroll=DMA_GROUP)

    def for_token_slots(blk, fn):
        nv = nv_ref[blk]

        def body(g, carry):
            j0 = g * DMA_GROUP

            @pl.when(j0 + DMA_GROUP <= nv)
            def _():
                for k in range(DMA_GROUP):
                    fn(j0 + k)

            @pl.when(jnp.logical_and(j0 < nv, j0 + DMA_GROUP > nv))
            def _():
                for k in range(DMA_GROUP):
                    pl.when(j0 + k < nv)(functools.partial(fn, j0 + k))
            return carry
        lax.fori_loop(0, MOE_BLOCK // DMA_GROUP, body, 0)

    @pl.when(i == 0)
    def _():
        for_slots(lambda j: gather_copy(0, 0, j).start())

    @pl.when(i < n_used)
    def _():
        for_slots(lambda j: gather_copy(i, slot, j).wait())

        @pl.when(i + 1 < n_used)
        def _():
            for_slots(lambda j: gather_copy(i + 1, 1 - slot, j).start())

        x = _from_token_tiles(xbuf.at[slot], MOE_BLOCK, r).astype(BF16)
        ws = ws_ref[...]

        ga, ua = _dot(x, w1a_ref[...]), _dot(x, w3a_ref[...])
        gb, ub = _dot(x, w1b_ref[...]), _dot(x, w3b_ref[...])
        ya = _dot((_silu(ga) * ua).astype(BF16), w2a_ref[...])
        yb = _dot((_silu(gb) * ub).astype(BF16), w2b_ref[...])
        y = ws[:, 0:1] * ya + ws[:, 1:2] * yb

        _to_token_tiles(ybuf.at[slot], y)

        @pl.when(i >= 1)
        def _():
            for_token_slots(i - 1, lambda j: scatter_copy(i - 1, 1 - slot, j).wait())

        for_token_slots(i, lambda j: scatter_copy(i, slot, j).start())

        @pl.when(i == n_used - 1)
        def _():
            for_token_slots(i, lambda j: scatter_copy(i, slot, j).wait())


def _experts(ht, w_slot, block_ea, block_eb, n_used, n_valid, slot_src, w1, w3, w2, n_out_tokens):
    d, f = w1.shape[1], w1.shape[2]
    r = d // LANES
    n_slots = slot_src.shape[0]

    def wspec(shape, which):
        if which == 0:
            return pl.BlockSpec((None,) + shape, lambda i, ea, eb, nu, nv, src: (ea[i], 0, 0))
        return pl.BlockSpec((None,) + shape, lambda i, ea, eb, nu, nv, src: (eb[i], 0, 0))

    grid_spec = pltpu.PrefetchScalarGridSpec(
        num_scalar_prefetch=5,
        grid=(n_slots // MOE_BLOCK,),
        in_specs=[pl.BlockSpec(memory_space=pl.ANY),
                  pl.BlockSpec((MOE_BLOCK, 2), lambda i, ea, eb, nu, nv, src: (i, 0)),
                  wspec((d, f), 0), wspec((d, f), 0), wspec((f, d), 0),
                  wspec((d, f), 1), wspec((d, f), 1), wspec((f, d), 1)],
        out_specs=pl.BlockSpec(memory_space=pl.ANY),
        scratch_shapes=[pltpu.VMEM((2, MOE_BLOCK * r, LANES), F32),
                        pltpu.VMEM((2, MOE_BLOCK * r, LANES), F32),
                        pltpu.SemaphoreType.DMA((2,)),
                        pltpu.SemaphoreType.DMA((2,))],
    )
    return pl.pallas_call(
        functools.partial(_expert_kernel, r=r),
        grid_spec=grid_spec,
        out_shape=jax.ShapeDtypeStruct((n_out_tokens * r, LANES), F32),
        compiler_params=_params("arbitrary"),
    )(block_ea, block_eb, n_used, n_valid, slot_src, ht, w_slot, w1, w3, w2, w1, w3, w2)


def _ln2_kernel(x_ref, yt_ref, mod_ref, g_ref, b_ref, o_ref, *, alpha):
    tm, d = x_ref.shape
    y = _from_token_tiles(yt_ref, tm, d // LANES)
    m = mod_ref[...]
    o_ref[...] = _layer_norm(alpha * x_ref[...] + m[5:6] * y, g_ref[...], b_ref[...])


def _ln2(x_all, yt, modt, g, b, *, alpha):
    t, d = x_all.shape
    r = d // LANES
    row = lambda i: (i, 0)
    const = lambda i: (0, 0)
    return pl.pallas_call(
        functools.partial(_ln2_kernel, alpha=alpha),
        grid=(t // TM,),
        in_specs=[pl.BlockSpec((TM, d), row), pl.BlockSpec((TM * r, LANES), row),
                  pl.BlockSpec((None, 6, d), lambda i: (i, 0, 0)),
                  pl.BlockSpec((1, d), const), pl.BlockSpec((1, d), const)],
        out_specs=pl.BlockSpec((TM, d), row),
        out_shape=jax.ShapeDtypeStruct((t, d), F32),
        compiler_params=_params("parallel"),
    )(x_all, yt, modt, g, b)


def _rope_table(s, nc):
    n_freq = A_HEAD_DIM // 4
    rows = s // GRID_W
    row = jnp.repeat(jnp.arange(rows, dtype=F32), GRID_W)
    col = jnp.tile(jnp.arange(GRID_W, dtype=F32), rows)
    inv = ROPE_BASE ** (-jnp.arange(n_freq, dtype=F32) / n_freq)
    ang = jnp.stack([row[:, None] * inv, col[:, None] * inv], axis=1)
    cos = jnp.cos(ang).reshape(s, 1, 1, 2, n_freq)
    sin = jnp.sin(ang).reshape(s, 1, 1, 2, n_freq)
    cos = jnp.broadcast_to(cos, (s, 2, 2, 2, n_freq)).reshape(s, LANES)
    sign = jnp.array([-1.0, 1.0], F32).reshape(1, 2, 1, 1, 1)
    sin = jnp.broadcast_to(sin * sign, (s, 2, 2, 2, n_freq)).reshape(s, LANES)
    lat = jnp.concatenate([cos, sin], axis=1)
    ident = jnp.concatenate([jnp.ones((nc, LANES), F32), jnp.zeros((nc, LANES), F32)], axis=1)
    return jnp.concatenate([lat, ident], axis=0)


def _rope_perm(w):
    d = w.shape[0]
    n_freq = A_HEAD_DIM // 4
    return w.reshape(d, A_HEADS, 2, 2, 2, n_freq).transpose(0, 1, 4, 2, 3, 5).reshape(d, A_WIDTH)


def _pack_w_in(w):
    d = w.shape[0]
    sizes = (A_WIDTH, A_WIDTH, A_WIDTH, G_QKV_WIDTH, G_V_WIDTH, 2 * G_HEADS, 2 * G_HEADS)
    offs = [0]
    for sz in sizes:
        offs.append(offs[-1] + sz)
    wq, wk, wv, wg, wz, wa, wb = (w[:, offs[i]:offs[i + 1]] for i in range(len(sizes)))
    wgate = w[:, offs[-1]:]
    wab = jnp.concatenate([wa, wb, jnp.zeros((d, LANES - 4 * G_HEADS), w.dtype)], axis=1)
    cat = jnp.concatenate([_rope_perm(wq) * (A_HEAD_DIM ** -0.5), _rope_perm(wk), wv, wg, wz, wab, wgate], axis=1)
    return cat.astype(BF16)


def _pad_rows(a, rows):
    return jnp.concatenate([a, jnp.zeros((rows - a.shape[0],) + a.shape[1:], a.dtype)], axis=0)


def _pad_lanes(a):
    return jnp.concatenate([a, jnp.zeros(a.shape[:-1] + (LANES - a.shape[-1],), a.dtype)], axis=-1)


def _dispatch(cls, ew, t):
    onehot = (cls[:, None] == jnp.arange(N_CLASSES, dtype=jnp.int32)[None, :]).astype(jnp.int32)
    csum = jnp.cumsum(onehot, axis=0)
    rank = jnp.sum(csum * onehot, axis=1) - 1
    counts = csum[-1]
    padded = (counts + MOE_BLOCK - 1) // MOE_BLOCK * MOE_BLOCK
    pad_end = jnp.cumsum(padded)
    pad_start = pad_end - padded
    dest = pad_start[cls] + rank
    n_blocks = -(-t // MOE_BLOCK) + N_CLASSES
    per_tok = jnp.concatenate([jnp.arange(t, dtype=F32)[:, None], ew], axis=1)
    per_slot = jnp.zeros((n_blocks * MOE_BLOCK, per_tok.shape[1]), F32).at[dest].set(per_tok)
    slot_src = per_slot[:, 0].astype(jnp.int32)
    w_slot = per_slot[:, 1:]
    block0 = jnp.arange(n_blocks, dtype=jnp.int32) * MOE_BLOCK
    block_cls = jnp.minimum(jnp.searchsorted(pad_end, block0, side='right'), N_CLASSES - 1)
    n_valid = jnp.clip((pad_start + counts)[block_cls] - block0, 0, MOE_BLOCK).astype(jnp.int32)
    n_used = (pad_end[-1:] // MOE_BLOCK).astype(jnp.int32)
    return slot_src, w_slot, jnp.asarray(CLASS_EA)[block_cls], jnp.asarray(CLASS_EB)[block_cls], n_used, n_valid


def kernel(x, c, ctx, c_ctx, mod_w, mod_b, w_in, conv_w, lam_q1, lam_k1, lam_q2, lam_k2, subln_g, a_log, dt_bias,
           onorm_g, w_pa, w_pb, w_o, ln1_g, ln1_b, router_w, router_b, w_exp1, w_exp3, w_exp2, ln2_g, ln2_b):
    b, s, d = x.shape
    nc = ctx.shape[1]
    depth = mod_w.shape[0]
    assert s % TM == 0 and nc % TM == 0 and s % nc == 0 and s % GRID_W == 0
    assert d == SUBLANES * LANES
    t_lat, t_ctx = b * s, b * nc
    t = t_lat + t_ctx
    n_lat_tiles = t_lat // TM
    tiles_per_seq = s // TM
    alpha = (2.0 * depth) ** 0.25

    x_all = jnp.concatenate([x.reshape(t_lat, d), ctx.reshape(t_ctx, d)], axis=0)
    cc = _pad_rows(jnp.concatenate([c, c_ctx[None, :]], axis=0), SUBLANES)
    mods = _mod_tables(cc, mod_w, mod_b)
    tile_row = jnp.concatenate([jnp.arange(n_lat_tiles, dtype=jnp.int32) // tiles_per_seq,
                                jnp.full((t_ctx // TM,), b, jnp.int32)])
    cs_tab = _rope_table(s, nc)
    rw_t = router_w.T.astype(BF16)
    rb = router_b.reshape(N_EXPERTS, 1)

    for l in range(depth):
        lam_init = 0.8 - 0.6 * math.exp(-0.3 * l)
        modt = mods[l].reshape(SUBLANES, 6, d)[tile_row]
        qk, vt, g, z, ab, gate = _inproj(x_all, modt, _pack_w_in(w_in[l]), cs_tab, n_lat_tiles, tiles_per_seq)

        lamv = _pad_rows(_pad_lanes(jnp.stack([lam_q1[l], lam_k1[l], lam_q2[l], lam_k2[l]])), SUBLANES)
        sg = subln_g[l].reshape(LANES, 1)
        ya_lat = _attention(qk, vt, lamv, sg, lam_init=lam_init, b=b, s=s, nc=nc, latent=True)
        ya_ctx = _attention(qk, vt, lamv, sg, lam_init=lam_init, b=b, s=s, nc=nc, latent=False)

        gparams = _pad_rows(_pad_lanes(jnp.stack([a_log[l].reshape(-1), dt_bias[l].reshape(-1)])), SUBLANES)
        qg, kg, vg, gc = _gdn_prep(g, ab, _pad_rows(conv_w[l], SUBLANES), gparams, n_lat_tiles, tiles_per_seq)
        o_f, o_b = _gdn(qg, kg, vg, gc, b=b, s=s, nc=nc)

        t_out = t_lat if l == depth - 1 else t
        x_all, ht, cls, ew = _merge(x_all, ya_lat, ya_ctx, o_f, o_b, z, gate, modt, w_pa[l].astype(BF16), w_pb[l].astype(BF16),
                                    w_o[l].astype(BF16), onorm_g[l].reshape(1, LANES), ln1_g[l].reshape(1, d),
                                    ln1_b[l].reshape(1, d), rw_t, rb, alpha=alpha, t=t_out)

        slot_src, w_slot, block_ea, block_eb, n_used, n_valid = _dispatch(cls[0], ew[:2].T, t_out)
        yt = _experts(ht, w_slot, block_ea, block_eb, n_used, n_valid, slot_src, w_exp1[l].astype(BF16),
                      w_exp3[l].astype(BF16), w_exp2[l].astype(BF16), t_out)
        x_all = _ln2(x_all, yt, modt, ln2_g[l].reshape(1, d), ln2_b[l].reshape(1, d), alpha=alpha)
    return x_all.reshape(b, s, d)
```

```python
import functools
import math

import numpy as np

import jax
import jax.numpy as jnp
from jax import lax
from jax.experimental import pallas as pl
from jax.experimental.pallas import tpu as pltpu

F32 = jnp.float32
BF16 = jnp.bfloat16

GRID_W = 64
A_HEADS = 4
A_HEAD_DIM = 64
A_WIDTH = A_HEADS * 2 * A_HEAD_DIM
G_HEADS = 4
G_KEY_DIM = 128
G_VAL_DIM = 128
G_QK_WIDTH = G_HEADS * G_KEY_DIM
G_V_WIDTH = G_HEADS * G_VAL_DIM
G_QKV_WIDTH = 2 * G_QK_WIDTH + G_V_WIDTH
CONV_K = 5
CHUNK = 64
N_EXPERTS = 16
N_GROUPS = 4
EXPERTS_PER_GROUP = N_EXPERTS // N_GROUPS
MOE_BLOCK = 256
ROPE_BASE = 10000.0
EPS = 1e-6

LANES = 128
SUBLANES = 8
TM = 256
DMA_GROUP = 8
GDN_CHUNKS_PER_BATCH = 4
ATTN_KEY_CHUNK = 256
ATTN_Q_SUBTILES = 4
VMEM_LIMIT = 56 * 1024 * 1024

OFF_QK = 0
OFF_VA = 2 * A_WIDTH
OFF_G = OFF_VA + A_WIDTH
OFF_Z = OFF_G + G_QKV_WIDTH
OFF_AB = OFF_Z + G_V_WIDTH
OFF_GATE = OFF_AB + LANES
GC_CUM, GC_BETA, GC_EG, GC_EK, GC_ET = 0, 8, 16, 24, 32

PAIRS = tuple((lo, hi) for lo in range(EXPERTS_PER_GROUP) for hi in range(lo + 1, EXPERTS_PER_GROUP))
N_CLASSES = N_GROUPS * len(PAIRS)
CLASS_EA = np.array([g * EXPERTS_PER_GROUP + lo for g in range(N_GROUPS) for lo, _ in PAIRS], np.int32)
CLASS_EB = np.array([g * EXPERTS_PER_GROUP + hi for g in range(N_GROUPS) for _, hi in PAIRS], np.int32)


def _sigmoid(x):
    return 1.0 / (1.0 + jnp.exp(-x))


def _silu(x):
    return x * _sigmoid(x)


def _dot(a, b):
    return jnp.dot(a, b, preferred_element_type=F32)


def _dot_nt(a, b):
    return lax.dot_general(a, b, (((1,), (1,)), ((), ())), preferred_element_type=F32)


def _dot_f32(a, b):
    return jnp.dot(a, b, preferred_element_type=F32, precision=lax.Precision.HIGHEST)


def _params(*sem):
    return pltpu.CompilerParams(dimension_semantics=sem, vmem_limit_bytes=VMEM_LIMIT)


def _to_token_tiles(ref, x):
    rows, width = x.shape
    r = width // LANES
    for c in range(r):
        ref[pl.ds(c, rows, stride=r), :] = x[:, c * LANES:(c + 1) * LANES]


def _from_token_tiles(ref, rows, r):
    return jnp.concatenate([ref[pl.ds(c, rows, stride=r), :] for c in range(r)], axis=-1)


def _mod_kernel(cc_ref, w_ref, b_ref, o_ref):
    h = _silu(cc_ref[...])
    o_ref[...] = _dot(h.astype(BF16), w_ref[...].astype(BF16)) + b_ref[...]


def _mod_tables(cc, mod_w, mod_b):
    depth, d, n = mod_w.shape
    tn = n // 6
    return pl.pallas_call(
        _mod_kernel,
        grid=(depth, n // tn),
        in_specs=[pl.BlockSpec((SUBLANES, d), lambda l, j: (0, 0)),
                  pl.BlockSpec((None, d, tn), lambda l, j: (l, 0, j)),
                  pl.BlockSpec((None, 1, tn), lambda l, j: (l, 0, j))],
        out_specs=pl.BlockSpec((None, SUBLANES, tn), lambda l, j: (l, 0, j)),
        out_shape=jax.ShapeDtypeStruct((depth, SUBLANES, n), F32),
        compiler_params=_params("parallel", "parallel"),
    )(cc, mod_w, mod_b.reshape(depth, 1, n))


def _inproj_kernel(x_ref, mod_ref, w_ref, cs_ref, qk_ref, va_ref, g_ref, z_ref, ab_ref, gate_ref):
    m = mod_ref[...]
    h = (x_ref[...] * (1.0 + m[1:2]) + m[0:1]).astype(BF16)
    cs = cs_ref[...]
    cos, sin = cs[:, :LANES], cs[:, LANES:]
    qk = _dot(h, w_ref[:, OFF_QK:OFF_VA])
    for j in range(2 * A_HEADS):
        a = qk[:, j * LANES:(j + 1) * LANES]
        qk_ref[:, j * LANES:(j + 1) * LANES] = (a * cos + pltpu.roll(a, LANES // 2, 1) * sin).astype(BF16)
    va = _dot(h, w_ref[:, OFF_VA:OFF_G])
    for j in range(A_HEADS):
        va_ref[j * LANES:(j + 1) * LANES, :] = va[:, j * LANES:(j + 1) * LANES].T.astype(BF16)
    g_ref[...] = _dot(h, w_ref[:, OFF_G:OFF_Z])
    z_ref[...] = _dot(h, w_ref[:, OFF_Z:OFF_AB])
    ab_ref[...] = _dot(h, w_ref[:, OFF_AB:OFF_GATE])
    gate_ref[...] = _sigmoid(_dot(h, w_ref[:, OFF_GATE:])).astype(BF16)


def _inproj(x_all, modt, w_cat, cs_tab, n_lat_tiles, tiles_per_seq):
    t, d = x_all.shape
    n = w_cat.shape[1]
    d_gate = n - OFF_GATE

    def row(i):
        return (i, 0)

    def cs_map(i):
        return (jnp.where(i < n_lat_tiles, i % tiles_per_seq, tiles_per_seq), 0)

    out_shape = [jax.ShapeDtypeStruct((t, 2 * A_WIDTH), BF16),
                 jax.ShapeDtypeStruct((A_WIDTH, t), BF16),
                 jax.ShapeDtypeStruct((t, G_QKV_WIDTH), F32),
                 jax.ShapeDtypeStruct((t, G_V_WIDTH), F32),
                 jax.ShapeDtypeStruct((t, LANES), F32),
                 jax.ShapeDtypeStruct((t, d_gate), BF16)]
    out_specs = [pl.BlockSpec((TM, s.shape[1]), row) for s in out_shape]
    out_specs[1] = pl.BlockSpec((A_WIDTH, TM), lambda i: (0, i))
    return pl.pallas_call(
        _inproj_kernel,
        grid=(t // TM,),
        in_specs=[pl.BlockSpec((TM, d), row),
                  pl.BlockSpec((None, 6, d), lambda i: (i, 0, 0)),
                  pl.BlockSpec((d, n), lambda i: (0, 0)),
                  pl.BlockSpec((TM, 2 * LANES), cs_map)],
        out_specs=out_specs,
        out_shape=out_shape,
        compiler_params=_params("parallel"),
    )(x_all, modt, w_cat, cs_tab)


def _row_fold(x, op):
    rows, n = x.shape
    return op(x.reshape(rows // SUBLANES, SUBLANES, n), axis=0)


def _attn_scores(q, kv_refs, s_refs):
    lane = lax.broadcasted_iota(jnp.int32, (1, LANES), 1)
    comp1 = ((lane // (A_HEAD_DIM // 2)) % 2) == 1
    qn = (jnp.where(comp1, jnp.zeros_like(q), q), jnp.where(comp1, q, jnp.zeros_like(q)))
    m_part, off = [None, None], 0
    for k_ref, _ in kv_refs:
        nk = k_ref.shape[0]
        for n in range(2):
            st = _dot_nt(k_ref[...], qn[n])
            s_refs[n][pl.ds(off, nk), :] = st
            mc = _row_fold(st, jnp.max)
            m_part[n] = mc if m_part[n] is None else jnp.maximum(m_part[n], mc)
        off += nk
    return [jnp.max(mp, axis=0, keepdims=True) for mp in m_part]


def _attn_values(lam_ref, kv_refs, g_ref, s_refs, m, lam_init):
    lv = lam_ref[...]
    lam = (jnp.exp(jnp.sum(lv[0:1] * lv[1:2], axis=-1, keepdims=True))
           - jnp.exp(jnp.sum(lv[2:3] * lv[3:4], axis=-1, keepdims=True)) + lam_init)
    l_part, ot, off = [None, None], [None, None], 0
    for k_ref, vt_ref in kv_refs:
        for r0 in range(0, k_ref.shape[0], ATTN_KEY_CHUNK):
            for n in range(2):
                p = jnp.exp(s_refs[n][pl.ds(off + r0, ATTN_KEY_CHUNK), :] - m[n])
                lc = _row_fold(p, jnp.sum)
                l_part[n] = lc if l_part[n] is None else l_part[n] + lc
                oc = _dot(vt_ref[:, pl.ds(r0, ATTN_KEY_CHUNK)], p.astype(BF16))
                ot[n] = oc if ot[n] is None else ot[n] + oc
        off += k_ref.shape[0]
    outs = [ot[n] / jnp.sum(l_part[n], axis=0, keepdims=True) for n in range(2)]
    ot = outs[0] - lam * outs[1]
    r = ot * lax.rsqrt(jnp.mean(ot * ot, axis=0, keepdims=True) + EPS)
    return (r * g_ref[...] * (1.0 - lam_init)).T.astype(BF16)


def _attn_kernel(lam_ref, q_ref, *refs, lam_init, n_kv, n_sub):
    kv_refs = tuple((refs[2 * j], refs[2 * j + 1]) for j in range(n_kv))
    g_ref, o_ref = refs[2 * n_kv], refs[2 * n_kv + 1]
    s_all = refs[2 * n_kv + 2:]
    tq = q_ref.shape[0] // n_sub
    def scores(j):
        s_refs = (s_all[2 * j], s_all[2 * j + 1])
        return s_refs, _attn_scores(q_ref[pl.ds(j * tq, tq), :], kv_refs, s_refs)

    nxt = scores(0)
    for j in range(n_sub):
        s_refs, m = nxt
        if j + 1 < n_sub:
            nxt = scores(j + 1)
        o_ref[pl.ds(j * tq, tq), :] = _attn_values(lam_ref, kv_refs, g_ref, s_refs, m, lam_init)


def _attention(qk, vt, lamv, subln_g, *, lam_init, b, s, nc, latent):
    n_sub = ATTN_Q_SUBTILES if latent else 1
    rows = TM * n_sub
    ctx0 = (b * s) // nc
    kv_specs = [pl.BlockSpec((nc, LANES), lambda bi, h, i: (ctx0 + bi, A_HEADS + h)),
                pl.BlockSpec((LANES, nc), lambda bi, h, i: (h, ctx0 + bi))]
    kv_args = [qk, vt]
    if latent:
        n_q = s // rows
        q_map = lambda bi, h, i: (bi * n_q + i, h)
        kv_specs += [pl.BlockSpec((s, LANES), lambda bi, h, i: (bi, A_HEADS + h)),
                     pl.BlockSpec((LANES, s), lambda bi, h, i: (h, bi))]
        kv_args += [qk, vt]
        n_keys, t_out = nc + s, b * s
    else:
        n_q = nc // rows
        q_map = lambda bi, h, i: ((b * s) // rows + bi * n_q + i, h)
        n_keys, t_out = nc, b * nc
    return pl.pallas_call(
        functools.partial(_attn_kernel, lam_init=lam_init, n_kv=len(kv_args) // 2, n_sub=n_sub),
        grid=(b, A_HEADS, n_q),
        in_specs=[pl.BlockSpec((SUBLANES, LANES), lambda bi, h, i: (0, 0)),
                  pl.BlockSpec((rows, LANES), q_map)] + kv_specs
                 + [pl.BlockSpec((LANES, 1), lambda bi, h, i: (0, 0))],
        out_specs=pl.BlockSpec((rows, LANES), lambda bi, h, i: (bi * n_q + i, h)),
        out_shape=jax.ShapeDtypeStruct((t_out, A_WIDTH), BF16),
        scratch_shapes=[pltpu.VMEM((n_keys, TM), F32)] * (2 * n_sub),
        compiler_params=_params("parallel", "parallel", "arbitrary"),
    )(lamv, qk, *kv_args, subln_g)


def _gdn_prep_kernel(g_ref, prev_ref, next_ref, ab_ref, cw_ref, gp_ref, q_ref, k_ref, v_ref, gc_ref, ext_ref,
                     *, n_lat_tiles, tiles_per_seq):
    i = pl.program_id(0)
    is_lat = i < n_lat_tiles
    first = jnp.logical_or(jnp.logical_not(is_lat), i % tiles_per_seq == 0)
    last = jnp.logical_or(jnp.logical_not(is_lat), i % tiles_per_seq == tiles_per_seq - 1)
    ext_ref[0:SUBLANES, :] = jnp.where(first, 0.0, prev_ref[...])
    ext_ref[SUBLANES:SUBLANES + TM, :] = g_ref[...]
    ext_ref[SUBLANES + TM:, :] = jnp.where(last, 0.0, next_ref[...])
    cw = cw_ref[...]
    acc = None
    for j in range(CONV_K):
        term = ext_ref[pl.ds(SUBLANES - CONV_K // 2 + j, TM), :] * cw[j:j + 1]
        acc = term if acc is None else acc + term
    y = _silu(acc)
    for h in range(G_HEADS):
        for base, ref, scale in ((0, q_ref, G_KEY_DIM ** -0.5), (G_QK_WIDTH, k_ref, 1.0)):
            xh = y[:, base + h * LANES:base + (h + 1) * LANES]
            nrm = xh * lax.rsqrt(jnp.sum(xh * xh, axis=-1, keepdims=True) + EPS)
            ref[:, h * LANES:(h + 1) * LANES] = nrm * scale
    v_ref[...] = y[:, 2 * G_QK_WIDTH:]

    ab = ab_ref[...]
    gp = gp_ref[...]
    lane = lax.broadcasted_iota(jnp.int32, (1, LANES), 1)
    z = ab + gp[1:2]
    softplus = jnp.maximum(z, 0.0) + jnp.log(1.0 + jnp.exp(-jnp.abs(z)))
    g = jnp.where(lane < 2 * G_HEADS, -jnp.exp(gp[0:1]) * softplus, 0.0)
    beta = _sigmoid(ab)
    ri = lax.broadcasted_iota(jnp.int32, (TM, TM), 0)
    ci = lax.broadcasted_iota(jnp.int32, (TM, TM), 1)
    same = (ri // CHUNK) == (ci // CHUNK)
    lower = jnp.where(jnp.logical_and(same, ci <= ri), 1.0, 0.0)
    upper = jnp.where(jnp.logical_and(same, ci >= ri), 1.0, 0.0)
    cum_f = _dot_f32(lower, g)
    cum_b = _dot_f32(upper, g)
    total = cum_f + cum_b - g
    cum = jnp.where(lane < G_HEADS, cum_f, cum_b)
    eg = jnp.exp(cum)
    ek = jnp.exp(total - cum)
    et = jnp.exp(total)
    out = jnp.where(lane < GC_BETA, cum, 0.0)
    out = jnp.where(jnp.logical_and(lane >= GC_BETA, lane < GC_EG), beta, out)
    out = jnp.where(jnp.logical_and(lane >= GC_EG, lane < GC_EK), pltpu.roll(eg, GC_EG, 1), out)
    out = jnp.where(jnp.logical_and(lane >= GC_EK, lane < GC_ET), pltpu.roll(ek, GC_EK, 1), out)
    out = jnp.where(jnp.logical_and(lane >= GC_ET, lane < GC_ET + 8), pltpu.roll(et, GC_ET, 1), out)
    gc_ref[...] = out


def _gdn_prep(g, ab, conv_w8, gparams, n_lat_tiles, tiles_per_seq):
    t, c = g.shape
    nblk8 = t // SUBLANES
    per = TM // SUBLANES
    row = lambda i: (i, 0)
    out_shape = [jax.ShapeDtypeStruct((t, G_QK_WIDTH), F32),
                 jax.ShapeDtypeStruct((t, G_QK_WIDTH), F32),
                 jax.ShapeDtypeStruct((t, G_V_WIDTH), F32),
                 jax.ShapeDtypeStruct((t, LANES), F32)]
    return pl.pallas_call(
        functools.partial(_gdn_prep_kernel, n_lat_tiles=n_lat_tiles, tiles_per_seq=tiles_per_seq),
        grid=(t // TM,),
        in_specs=[pl.BlockSpec((TM, c), row),
                  pl.BlockSpec((SUBLANES, c), lambda i: (jnp.maximum(i * per - 1, 0), 0)),
                  pl.BlockSpec((SUBLANES, c), lambda i: (jnp.minimum((i + 1) * per, nblk8 - 1), 0)),
                  pl.BlockSpec((TM, LANES), row),
                  pl.BlockSpec((SUBLANES, c), lambda i: (0, 0)),
                  pl.BlockSpec((SUBLANES, LANES), lambda i: (0, 0))],
        out_specs=[pl.BlockSpec((TM, s.shape[1]), row) for s in out_shape],
        out_shape=out_shape,
        scratch_shapes=[pltpu.VMEM((TM + 2 * SUBLANES, c), F32)],
        compiler_params=_params("parallel"),
    )(g, g, g, ab, conv_w8, gparams)


def _gdn_masks(c, reverse):
    ii = lax.broadcasted_iota(jnp.int32, (c, 2 * c), 0)
    lane = lax.broadcasted_iota(jnp.int32, (c, 2 * c), 1)
    left = lane < c
    jj = jnp.where(left, lane, lane - c)
    causal = (ii <= jj) if reverse else (ii >= jj)
    strict = (ii < jj) if reverse else (ii > jj)
    return dict(left=left, diag=ii == jj, causal=causal, strict=strict)


def _gdn_setup(q, k, v, gc, gct, col, mk):
    c = q.shape[0]
    gcol = gc[:, GC_CUM + col:GC_CUM + col + 1]
    beta = gc[:, GC_BETA + col:GC_BETA + col + 1]
    eg = gc[:, GC_EG + col:GC_EG + col + 1]
    ek = gc[:, GC_EK + col:GC_EK + col + 1]
    et = gc[0:1, GC_ET + col:GC_ET + col + 1]
    grow = gct[GC_CUM + col:GC_CUM + col + 1, :]
    decay = jnp.exp(jnp.where(mk["causal"], gcol - grow, -1e30))
    kb = k * beta
    k16 = k.astype(BF16)
    both = _dot_nt(jnp.concatenate([kb.astype(BF16), q.astype(BF16)], axis=0),
                   jnp.concatenate([k16, k16], axis=0))
    qk = jnp.where(jnp.logical_and(mk["causal"], mk["left"]), both[c:] * decay, 0.0)
    z = jnp.where(mk["left"], jnp.where(mk["diag"], 1.0, 0.0), jnp.where(mk["strict"], -(both[:c] * decay), 0.0))
    rhs = jnp.concatenate([v * beta, kb * eg], axis=1)
    return dict(z=z, qk=qk.astype(BF16), rhs=rhs, qd=(q * eg).astype(BF16), kdt=(k * ek).T.astype(BF16), et=et)


def _gdn_level(z, mk):
    z16 = z.astype(BF16)
    return jnp.where(mk["left"], z, 0.0) + _dot(z16, jnp.concatenate([jnp.zeros_like(z16), z16], axis=0))


def _gdn_solve(ch, mk):
    y16 = jnp.where(jnp.logical_and(mk["left"], jnp.logical_not(mk["diag"])), ch["z"], 0.0).astype(BF16)
    rhs16 = ch["rhs"].astype(BF16)
    sol = ch["rhs"] + _dot(y16, jnp.concatenate([rhs16, rhs16], axis=0))
    dv = sol.shape[1] // 2
    return sol[:, :dv], jnp.concatenate([sol[:, dv:].astype(BF16), ch["qd"]], axis=0)


def _gdn_kernel(qf_ref, kf_ref, vf_ref, gf_ref, qb_ref, kb_ref, vb_ref, gb_ref, of_ref, ob_ref, s_ref):
    @pl.when(pl.program_id(1) == 0)
    def _():
        s_ref[...] = jnp.zeros_like(s_ref)

    n_chunks = qf_ref.shape[0] // CHUNK
    masks = (_gdn_masks(CHUNK, False), _gdn_masks(CHUNK, True))
    dirs = ((qf_ref, kf_ref, vf_ref, gf_ref, of_ref), (qb_ref, kb_ref, vb_ref, gb_ref, ob_ref))
    state = [s_ref[i] for i in range(2 * G_HEADS)]
    for n0 in range(0, n_chunks, GDN_CHUNKS_PER_BATCH):
        chains = []
        for n in range(n0, n0 + GDN_CHUNKS_PER_BATCH):
            for d, (q_ref, k_ref, v_ref, g_ref, o_ref) in enumerate(dirs):
                rows = pl.ds((n_chunks - 1 - n if d else n) * CHUNK, CHUNK)
                gc = g_ref[rows, :]
                gct = jnp.concatenate([gc, gc], axis=0).T
                for h in range(G_HEADS):
                    cols = pl.ds(h * LANES, LANES)
                    ch = _gdn_setup(q_ref[rows, cols], k_ref[rows, cols], v_ref[rows, cols], gc, gct,
                                    d * G_HEADS + h, masks[d])
                    ch.update(d=d, h=h, rows=rows, cols=cols, out=o_ref)
                    chains.append(ch)
        for _ in range(int(math.log2(CHUNK))):
            for ch in chains:
                ch["z"] = _gdn_level(ch["z"], masks[ch["d"]])
        for ch in chains:
            ch["u"], ch["wq"] = _gdn_solve(ch, masks[ch["d"]])
        per_chunk = 2 * G_HEADS
        for c0 in range(0, len(chains), per_chunk):
            group = chains[c0:c0 + per_chunk]
            ws = [_dot(ch["wq"], state[ch["d"] * G_HEADS + ch["h"]].astype(BF16)) for ch in group]
            for ch, w in zip(group, ws):
                i = ch["d"] * G_HEADS + ch["h"]
                v_new = (ch["u"] - w[:CHUNK]).astype(BF16)
                o = w[CHUNK:] + _dot(ch["qk"], jnp.concatenate([v_new, v_new], axis=0))
                state[i] = state[i] * ch["et"] + _dot(ch["kdt"], v_new)
                ch["out"][ch["rows"], ch["cols"]] = o
    for i in range(2 * G_HEADS):
        s_ref[i] = state[i]


def _gdn(qg, kg, vg, gc, *, b, s, nc):
    t = qg.shape[0]
    tps = s // TM
    nct = nc // TM
    n_steps = nct + tps
    ctx_tile0 = (b * s) // TM

    def fwd(bi, i):
        return (jnp.where(i < nct, ctx_tile0 + bi * nct + i, bi * tps + (i - nct)), 0)

    def bwd(bi, i):
        return (jnp.where(i < nct, ctx_tile0 + bi * nct + (nct - 1 - i), bi * tps + (tps - 1 - (i - nct))), 0)

    def specs(row_map):
        return [pl.BlockSpec((TM, G_QK_WIDTH), row_map), pl.BlockSpec((TM, G_QK_WIDTH), row_map),
                pl.BlockSpec((TM, G_V_WIDTH), row_map), pl.BlockSpec((TM, LANES), row_map)]

    return pl.pallas_call(
        _gdn_kernel,
        grid=(b, n_steps),
        in_specs=specs(fwd) + specs(bwd),
        out_specs=[pl.BlockSpec((TM, G_V_WIDTH), fwd), pl.BlockSpec((TM, G_V_WIDTH), bwd)],
        out_shape=[jax.ShapeDtypeStruct((t, G_V_WIDTH), F32)] * 2,
        scratch_shapes=[pltpu.VMEM((2 * G_HEADS, G_KEY_DIM, G_VAL_DIM), F32)],
        compiler_params=_params("parallel", "arbitrary"),
    )(qg, kg, vg, gc, qg, kg, vg, gc)


def _layer_norm(r, g, b):
    mu = jnp.mean(r, axis=-1, keepdims=True)
    rc = r - mu
    var = jnp.mean(rc * rc, axis=-1, keepdims=True)
    return rc * lax.rsqrt(var + EPS) * g + b


def _merge_kernel(x_ref, yal_ref, yac_ref, of_ref, ob_ref, z_ref, gate_ref, mod_ref, wpa_ref, wpb_ref, wo_ref, on_ref,
                  lng_ref, lnb_ref, rw_ref, rb_ref, xo_ref, ht_ref, cls_ref, ew_ref, *, alpha, n_lat_tiles):
    d = x_ref.shape[1]
    o = of_ref[...] + ob_ref[...]
    parts = []
    for h in range(G_HEADS):
        oh = o[:, h * LANES:(h + 1) * LANES]
        parts.append(oh * lax.rsqrt(jnp.mean(oh * oh, axis=-1, keepdims=True) + EPS) * on_ref[...])
    yb = jnp.concatenate(parts, axis=1) * _silu(z_ref[...])
    ya = jnp.where(pl.program_id(0) < n_lat_tiles, yal_ref[...], yac_ref[...])
    pa = _dot(ya, wpa_ref[...])
    pb = _dot(yb.astype(BF16), wpb_ref[...])
    mix = gate_ref[:, :d] * pa + gate_ref[:, d:] * pb
    y = _dot(mix.astype(BF16), wo_ref[...])
    m = mod_ref[...]
    xn = _layer_norm(alpha * x_ref[...] + m[2:3] * y, lng_ref[...], lnb_ref[...])
    xo_ref[...] = xn
    h2 = xn * (1.0 + m[4:5]) + m[3:4]
    _to_token_tiles(ht_ref, h2)

    score = _sigmoid(_dot_nt(rw_ref[...], h2.astype(BF16)))
    sel = score + rb_ref[...]
    rows = [sel[e:e + 1] for e in range(N_EXPERTS)]
    best = None
    best_val = None
    for gi in range(N_GROUPS):
        a, b, c, dd = rows[gi * EXPERTS_PER_GROUP:(gi + 1) * EXPERTS_PER_GROUP]
        top2 = jnp.maximum(jnp.maximum(jnp.maximum(a + b, a + c), jnp.maximum(a + dd, b + c)),
                           jnp.maximum(b + dd, c + dd))
        if gi == 0:
            best, best_val = jnp.zeros(top2.shape, jnp.int32), top2
        else:
            upd = top2 > best_val
            best = jnp.where(upd, gi, best)
            best_val = jnp.where(upd, top2, best_val)
    in_group, in_score = [], []
    for j in range(EXPERTS_PER_GROUP):
        v = rows[(N_GROUPS - 1) * EXPERTS_PER_GROUP + j]
        sv = score[(N_GROUPS - 1) * EXPERTS_PER_GROUP + j:(N_GROUPS - 1) * EXPERTS_PER_GROUP + j + 1]
        for gi in range(N_GROUPS - 2, -1, -1):
            e = gi * EXPERTS_PER_GROUP + j
            v = jnp.where(best == gi, rows[e], v)
            sv = jnp.where(best == gi, score[e:e + 1], sv)
        in_group.append(v)
        in_score.append(sv)
    l0 = jnp.zeros(best.shape, jnp.int32)
    m0 = in_group[0]
    for j in range(1, EXPERTS_PER_GROUP):
        upd = in_group[j] > m0
        l0 = jnp.where(upd, j, l0)
        m0 = jnp.where(upd, in_group[j], m0)
    l1 = jnp.zeros(best.shape, jnp.int32)
    m1 = jnp.full(m0.shape, -jnp.inf, F32)
    for j in range(EXPERTS_PER_GROUP):
        upd = jnp.logical_and(l0 != j, in_group[j] > m1)
        l1 = jnp.where(upd, j, l1)
        m1 = jnp.where(upd, in_group[j], m1)
    lo = jnp.minimum(l0, l1)
    hi = jnp.maximum(l0, l1)
    w_lo = jnp.zeros(m0.shape, F32)
    w_hi = jnp.zeros(m0.shape, F32)
    for j in range(EXPERTS_PER_GROUP):
        w_lo = jnp.where(lo == j, in_score[j], w_lo)
        w_hi = jnp.where(hi == j, in_score[j], w_hi)
    pair = jnp.where(lo == 0, hi - 1, jnp.where(lo == 1, hi + 1, len(PAIRS) - 1))
    tot = w_lo + w_hi
    ri = lax.broadcasted_iota(jnp.int32, cls_ref.shape, 0)
    cls_ref[...] = jnp.where(ri == 0, best * len(PAIRS) + pair, 0)
    ew_ref[...] = jnp.where(ri == 0, w_lo / tot, jnp.where(ri == 1, w_hi / tot, 0.0))


def _merge(x_all, ya_lat, ya_ctx, o_f, o_b, z, gate, modt, wpa, wpb, wo, onorm, lng, lnb, rw_t, rb, *, alpha, t):
    d = x_all.shape[1]
    r = d // LANES
    n_lat_tiles = ya_lat.shape[0] // TM
    row = lambda i: (i, 0)
    col = lambda i: (0, i)
    const = lambda i: (0, 0)
    out_shape = [jax.ShapeDtypeStruct((t, d), F32),
                 jax.ShapeDtypeStruct((t * r, LANES), F32),
                 jax.ShapeDtypeStruct((SUBLANES, t), jnp.int32),
                 jax.ShapeDtypeStruct((SUBLANES, t), F32)]
    return pl.pallas_call(
        functools.partial(_merge_kernel, alpha=alpha, n_lat_tiles=n_lat_tiles),
        grid=(t // TM,),
        in_specs=[pl.BlockSpec((TM, d), row),
                  pl.BlockSpec((TM, A_WIDTH), lambda i: (jnp.minimum(i, n_lat_tiles - 1), 0)),
                  pl.BlockSpec((TM, A_WIDTH), lambda i: (jnp.maximum(i - n_lat_tiles, 0), 0)),
                  pl.BlockSpec((TM, G_V_WIDTH), row),
                  pl.BlockSpec((TM, G_V_WIDTH), row),
                  pl.BlockSpec((TM, G_V_WIDTH), row),
                  pl.BlockSpec((TM, 2 * d), row),
                  pl.BlockSpec((None, 6, d), lambda i: (i, 0, 0)),
                  pl.BlockSpec(wpa.shape, const),
                  pl.BlockSpec(wpb.shape, const),
                  pl.BlockSpec(wo.shape, const),
                  pl.BlockSpec((1, LANES), const),
                  pl.BlockSpec((1, d), const),
                  pl.BlockSpec((1, d), const),
                  pl.BlockSpec(rw_t.shape, const),
                  pl.BlockSpec(rb.shape, const)],
        out_specs=[pl.BlockSpec((TM, d), row), pl.BlockSpec((TM * r, LANES), row),
                   pl.BlockSpec((SUBLANES, TM), col), pl.BlockSpec((SUBLANES, TM), col)],
        out_shape=out_shape,
        compiler_params=_params("parallel"),
    )(x_all, ya_lat, ya_ctx, o_f, o_b, z, gate, modt, wpa, wpb, wo, onorm, lng, lnb, rw_t, rb)


def _expert_kernel(ea_ref, eb_ref, nu_ref, nv_ref, src_ref, h_hbm, ws_ref, w1a_ref, w3a_ref, w2a_ref,
                   w1b_ref, w3b_ref, w2b_ref, y_hbm, xbuf, ybuf, gsem, ssem, *, r):
    i = pl.program_id(0)
    n_used = nu_ref[0]
    slot = i % 2

    def gather_copy(blk, buf, j):
        tok = src_ref[blk * MOE_BLOCK + j]
        return pltpu.make_async_copy(h_hbm.at[pl.ds(pl.multiple_of(tok * r, r), r), :],
                                     xbuf.at[buf, pl.ds(pl.multiple_of(j * r, r), r), :], gsem.at[buf])

    def scatter_copy(blk, buf, j):
        tok = src_ref[blk * MOE_BLOCK + j]
        return pltpu.make_async_copy(ybuf.at[buf, pl.ds(pl.multiple_of(j * r, r), r), :],
                                     y_hbm.at[pl.ds(pl.multiple_of(tok * r, r), r), :], ssem.at[buf])

    def for_slots(fn):
        def body(j, carry):
            fn(j)
            return carry
        lax.fori_loop(0, MOE_BLOCK, body, 0, unroll=DMA_GROUP)

    def for_token_slots(blk, fn):
        nv = nv_ref[blk]

        def body(g, carry):
            j0 = g * DMA_GROUP

            @pl.when(j0 + DMA_GROUP <= nv)
            def _():
                for k in range(DMA_GROUP):
                    fn(j0 + k)

            @pl.when(jnp.logical_and(j0 < nv, j0 + DMA_GROUP > nv))
            def _():
                for k in range(DMA_GROUP):
                    pl.when(j0 + k < nv)(functools.partial(fn, j0 + k))
            return carry
        lax.fori_loop(0, MOE_BLOCK // DMA_GROUP, body, 0)

    @pl.when(i == 0)
    def _():
        for_slots(lambda j: gather_copy(0, 0, j).start())

    @pl.when(i < n_used)
    def _():
        for_slots(lambda j: gather_copy(i, slot, j).wait())

        @pl.when(i + 1 < n_used)
        def _():
            for_slots(lambda j: gather_copy(i + 1, 1 - slot, j).start())

        x = _from_token_tiles(xbuf.at[slot], MOE_BLOCK, r).astype(BF16)
        ws = ws_ref[...]

        ga, ua = _dot(x, w1a_ref[...]), _dot(x, w3a_ref[...])
        gb, ub = _dot(x, w1b_ref[...]), _dot(x, w3b_ref[...])
        ya = _dot((_silu(ga) * ua).astype(BF16), w2a_ref[...])
        yb = _dot((_silu(gb) * ub).astype(BF16), w2b_ref[...])
        y = ws[:, 0:1] * ya + ws[:, 1:2] * yb

        _to_token_tiles(ybuf.at[slot], y)

        @pl.when(i >= 1)
        def _():
            for_token_slots(i - 1, lambda j: scatter_copy(i - 1, 1 - slot, j).wait())

        for_token_slots(i, lambda j: scatter_copy(i, slot, j).start())

        @pl.when(i == n_used - 1)
        def _():
            for_token_slots(i, lambda j: scatter_copy(i, slot, j).wait())


def _experts(ht, w_slot, block_ea, block_eb, n_used, n_valid, slot_src, w1, w3, w2, n_out_tokens):
    d, f = w1.shape[1], w1.shape[2]
    r = d // LANES
    n_slots = slot_src.shape[0]

    def wspec(shape, which):
        if which == 0:
            return pl.BlockSpec((None,) + shape, lambda i, ea, eb, nu, nv, src: (ea[i], 0, 0))
        return pl.BlockSpec((None,) + shape, lambda i, ea, eb, nu, nv, src: (eb[i], 0, 0))

    grid_spec = pltpu.PrefetchScalarGridSpec(
        num_scalar_prefetch=5,
        grid=(n_slots // MOE_BLOCK,),
        in_specs=[pl.BlockSpec(memory_space=pl.ANY),
                  pl.BlockSpec((MOE_BLOCK, 2), lambda i, ea, eb, nu, nv, src: (i, 0)),
                  wspec((d, f), 0), wspec((d, f), 0), wspec((f, d), 0),
                  wspec((d, f), 1), wspec((d, f), 1), wspec((f, d), 1)],
        out_specs=pl.BlockSpec(memory_space=pl.ANY),
        scratch_shapes=[pltpu.VMEM((2, MOE_BLOCK * r, LANES), F32),
                        pltpu.VMEM((2, MOE_BLOCK * r, LANES), F32),
                        pltpu.SemaphoreType.DMA((2,)),
                        pltpu.SemaphoreType.DMA((2,))],
    )
    return pl.pallas_call(
        functools.partial(_expert_kernel, r=r),
        grid_spec=grid_spec,
        out_shape=jax.ShapeDtypeStruct((n_out_tokens * r, LANES), F32),
        compiler_params=_params("arbitrary"),
    )(block_ea, block_eb, n_used, n_valid, slot_src, ht, w_slot, w1, w3, w2, w1, w3, w2)


def _ln2_kernel(x_ref, yt_ref, mod_ref, g_ref, b_ref, o_ref, *, alpha):
    tm, d = x_ref.shape
    y = _from_token_tiles(yt_ref, tm, d // LANES)
    m = mod_ref[...]
    o_ref[...] = _layer_norm(alpha * x_ref[...] + m[5:6] * y, g_ref[...], b_ref[...])


def _ln2(x_all, yt, modt, g, b, *, alpha):
    t, d = x_all.shape
    r = d // LANES
    row = lambda i: (i, 0)
    const = lambda i: (0, 0)
    return pl.pallas_call(
        functools.partial(_ln2_kernel, alpha=alpha),
        grid=(t // TM,),
        in_specs=[pl.BlockSpec((TM, d), row), pl.BlockSpec((TM * r, LANES), row),
                  pl.BlockSpec((None, 6, d), lambda i: (i, 0, 0)),
                  pl.BlockSpec((1, d), const), pl.BlockSpec((1, d), const)],
        out_specs=pl.BlockSpec((TM, d), row),
        out_shape=jax.ShapeDtypeStruct((t, d), F32),
        compiler_params=_params("parallel"),
    )(x_all, yt, modt, g, b)


def _rope_table(s, nc):
    n_freq = A_HEAD_DIM // 4
    rows = s // GRID_W
    row = jnp.repeat(jnp.arange(rows, dtype=F32), GRID_W)
    col = jnp.tile(jnp.arange(GRID_W, dtype=F32), rows)
    inv = ROPE_BASE ** (-jnp.arange(n_freq, dtype=F32) / n_freq)
    ang = jnp.stack([row[:, None] * inv, col[:, None] * inv], axis=1)
    cos = jnp.cos(ang).reshape(s, 1, 1, 2, n_freq)
    sin = jnp.sin(ang).reshape(s, 1, 1, 2, n_freq)
    cos = jnp.broadcast_to(cos, (s, 2, 2, 2, n_freq)).reshape(s, LANES)
    sign = jnp.array([-1.0, 1.0], F32).reshape(1, 2, 1, 1, 1)
    sin = jnp.broadcast_to(sin * sign, (s, 2, 2, 2, n_freq)).reshape(s, LANES)
    lat = jnp.concatenate([cos, sin], axis=1)
    ident = jnp.concatenate([jnp.ones((nc, LANES), F32), jnp.zeros((nc, LANES), F32)], axis=1)
    return jnp.concatenate([lat, ident], axis=0)


def _rope_perm(w):
    d = w.shape[0]
    n_freq = A_HEAD_DIM // 4
    return w.reshape(d, A_HEADS, 2, 2, 2, n_freq).transpose(0, 1, 4, 2, 3, 5).reshape(d, A_WIDTH)


def _pack_w_in(w):
    d = w.shape[0]
    sizes = (A_WIDTH, A_WIDTH, A_WIDTH, G_QKV_WIDTH, G_V_WIDTH, 2 * G_HEADS, 2 * G_HEADS)
    offs = [0]
    for sz in sizes:
        offs.append(offs[-1] + sz)
    wq, wk, wv, wg, wz, wa, wb = (w[:, offs[i]:offs[i + 1]] for i in range(len(sizes)))
    wgate = w[:, offs[-1]:]
    wab = jnp.concatenate([wa, wb, jnp.zeros((d, LANES - 4 * G_HEADS), w.dtype)], axis=1)
    cat = jnp.concatenate([_rope_perm(wq) * (A_HEAD_DIM ** -0.5), _rope_perm(wk), wv, wg, wz, wab, wgate], axis=1)
    return cat.astype(BF16)


def _pad_rows(a, rows):
    return jnp.concatenate([a, jnp.zeros((rows - a.shape[0],) + a.shape[1:], a.dtype)], axis=0)


def _pad_lanes(a):
    return jnp.concatenate([a, jnp.zeros(a.shape[:-1] + (LANES - a.shape[-1],), a.dtype)], axis=-1)


def _dispatch(cls, ew, t):
    onehot = (cls[:, None] == jnp.arange(N_CLASSES, dtype=jnp.int32)[None, :]).astype(jnp.int32)
    csum = jnp.cumsum(onehot, axis=0)
    rank = jnp.sum(csum * onehot, axis=1) - 1
    counts = csum[-1]
    padded = (counts + MOE_BLOCK - 1) // MOE_BLOCK * MOE_BLOCK
    pad_end = jnp.cumsum(padded)
    pad_start = pad_end - padded
    dest = pad_start[cls] + rank
    n_blocks = -(-t // MOE_BLOCK) + N_CLASSES
    per_tok = jnp.concatenate([jnp.arange(t, dtype=F32)[:, None], ew], axis=1)
    per_slot = jnp.zeros((n_blocks * MOE_BLOCK, per_tok.shape[1]), F32).at[dest].set(per_tok)
    slot_src = per_slot[:, 0].astype(jnp.int32)
    w_slot = per_slot[:, 1:]
    block0 = jnp.arange(n_blocks, dtype=jnp.int32) * MOE_BLOCK
    block_cls = jnp.minimum(jnp.searchsorted(pad_end, block0, side='right'), N_CLASSES - 1)
    n_valid = jnp.clip((pad_start + counts)[block_cls] - block0, 0, MOE_BLOCK).astype(jnp.int32)
    n_used = (pad_end[-1:] // MOE_BLOCK).astype(jnp.int32)
    return slot_src, w_slot, jnp.asarray(CLASS_EA)[block_cls], jnp.asarray(CLASS_EB)[block_cls], n_used, n_valid


def kernel(x, c, ctx, c_ctx, mod_w, mod_b, w_in, conv_w, lam_q1, lam_k1, lam_q2, lam_k2, subln_g, a_log, dt_bias,
           onorm_g, w_pa, w_pb, w_o, ln1_g, ln1_b, router_w, router_b, w_exp1, w_exp3, w_exp2, ln2_g, ln2_b):
    b, s, d = x.shape
    nc = ctx.shape[1]
    depth = mod_w.shape[0]
    assert s % (TM * ATTN_Q_SUBTILES) == 0 and nc % TM == 0 and s % nc == 0 and s % GRID_W == 0
    assert d == SUBLANES * LANES
    t_lat, t_ctx = b * s, b * nc
    t = t_lat + t_ctx
    n_lat_tiles = t_lat // TM
    tiles_per_seq = s // TM
    alpha = (2.0 * depth) ** 0.25

    x_all = jnp.concatenate([x.reshape(t_lat, d), ctx.reshape(t_ctx, d)], axis=0)
    cc = _pad_rows(jnp.concatenate([c, c_ctx[None, :]], axis=0), SUBLANES)
    mods = _mod_tables(cc, mod_w, mod_b)
    tile_row = jnp.concatenate([jnp.arange(n_lat_tiles, dtype=jnp.int32) // tiles_per_seq,
                                jnp.full((t_ctx // TM,), b, jnp.int32)])
    cs_tab = _rope_table(s, nc)
    rw_t = router_w.T.astype(BF16)
    rb = router_b.reshape(N_EXPERTS, 1)

    for l in range(depth):
        lam_init = 0.8 - 0.6 * math.exp(-0.3 * l)
        modt = mods[l].reshape(SUBLANES, 6, d)[tile_row]
        qk, vt, g, z, ab, gate = _inproj(x_all, modt, _pack_w_in(w_in[l]), cs_tab, n_lat_tiles, tiles_per_seq)

        lamv = _pad_rows(_pad_lanes(jnp.stack([lam_q1[l], lam_k1[l], lam_q2[l], lam_k2[l]])), SUBLANES)
        sg = subln_g[l].reshape(LANES, 1)
        ya_lat = _attention(qk, vt, lamv, sg, lam_init=lam_init, b=b, s=s, nc=nc, latent=True)
        ya_ctx = _attention(qk, vt, lamv, sg, lam_init=lam_init, b=b, s=s, nc=nc, latent=False)

        gparams = _pad_rows(_pad_lanes(jnp.stack([a_log[l].reshape(-1), dt_bias[l].reshape(-1)])), SUBLANES)
        qg, kg, vg, gc = _gdn_prep(g, ab, _pad_rows(conv_w[l], SUBLANES), gparams, n_lat_tiles, tiles_per_seq)
        o_f, o_b = _gdn(qg, kg, vg, gc, b=b, s=s, nc=nc)

        t_out = t_lat if l == depth - 1 else t
        x_all, ht, cls, ew = _merge(x_all, ya_lat, ya_ctx, o_f, o_b, z, gate, modt, w_pa[l].astype(BF16), w_pb[l].astype(BF16),
                                    w_o[l].astype(BF16), onorm_g[l].reshape(1, LANES), ln1_g[l].reshape(1, d),
                                    ln1_b[l].reshape(1, d), rw_t, rb, alpha=alpha, t=t_out)

        slot_src, w_slot, block_ea, block_eb, n_used, n_valid = _dispatch(cls[0], ew[:2].T, t_out)
        yt = _experts(ht, w_slot, block_ea, block_eb, n_used, n_valid, slot_src, w_exp1[l].astype(BF16),
                      w_exp3[l].astype(BF16), w_exp2[l].astype(BF16), t_out)
        x_all = _ln2(x_all, yt, modt, ln2_g[l].reshape(1, d), ln2_b[l].reshape(1, d), alpha=alpha)
    return x_all.reshape(b, s, d)
```

```python
import functools
import math

import numpy as np

import jax
import jax.numpy as jnp
from jax import lax
from jax.experimental import pallas as pl
from jax.experimental.pallas import tpu as pltpu

F32 = jnp.float32
BF16 = jnp.bfloat16

GRID_W = 64
A_HEADS = 4
A_HEAD_DIM = 64
A_WIDTH = A_HEADS * 2 * A_HEAD_DIM
G_HEADS = 4
G_KEY_DIM = 128
G_VAL_DIM = 128
G_QK_WIDTH = G_HEADS * G_KEY_DIM
G_V_WIDTH = G_HEADS * G_VAL_DIM
G_QKV_WIDTH = 2 * G_QK_WIDTH + G_V_WIDTH
CONV_K = 5
CHUNK = 64
N_EXPERTS = 16
N_GROUPS = 4
EXPERTS_PER_GROUP = N_EXPERTS // N_GROUPS
MOE_BLOCK = 256
ROPE_BASE = 10000.0
EPS = 1e-6

LANES = 128
SUBLANES = 8
TM = 256
DMA_GROUP = 32
GDN_CHUNKS_PER_BATCH = 4
ATTN_KEY_CHUNK = 256
ATTN_Q_SUBTILES = 4
VMEM_LIMIT = 56 * 1024 * 1024

OFF_QK = 0
OFF_VA = 2 * A_WIDTH
OFF_G = OFF_VA + A_WIDTH
OFF_Z = OFF_G + G_QKV_WIDTH
OFF_AB = OFF_Z + G_V_WIDTH
OFF_GATE = OFF_AB + LANES
GC_CUM, GC_BETA, GC_EG, GC_EK, GC_ET = 0, 8, 16, 24, 32

PAIRS = tuple((lo, hi) for lo in range(EXPERTS_PER_GROUP) for hi in range(lo + 1, EXPERTS_PER_GROUP))
N_CLASSES = N_GROUPS * len(PAIRS)
CLASS_EA = np.array([g * EXPERTS_PER_GROUP + lo for g in range(N_GROUPS) for lo, _ in PAIRS], np.int32)
CLASS_EB = np.array([g * EXPERTS_PER_GROUP + hi for g in range(N_GROUPS) for _, hi in PAIRS], np.int32)


def _sigmoid(x):
    return 1.0 / (1.0 + jnp.exp(-x))


def _silu(x):
    return x * _sigmoid(x)


def _dot(a, b):
    return jnp.dot(a, b, preferred_element_type=F32)


def _dot_nt(a, b):
    return lax.dot_general(a, b, (((1,), (1,)), ((), ())), preferred_element_type=F32)


def _dot_f32(a, b):
    return jnp.dot(a, b, preferred_element_type=F32, precision=lax.Precision.HIGHEST)


def _params(*sem):
    return pltpu.CompilerParams(dimension_semantics=sem, vmem_limit_bytes=VMEM_LIMIT)


def _to_token_tiles(ref, x):
    rows, width = x.shape
    r = width // LANES
    for c in range(r):
        ref[pl.ds(c, rows, stride=r), :] = x[:, c * LANES:(c + 1) * LANES]


def _from_token_tiles(ref, rows, r):
    return jnp.concatenate([ref[pl.ds(c, rows, stride=r), :] for c in range(r)], axis=-1)


def _mod_kernel(cc_ref, w_ref, b_ref, o_ref):
    h = _silu(cc_ref[...])
    o_ref[...] = _dot(h.astype(BF16), w_ref[...].astype(BF16)) + b_ref[...]


def _mod_tables(cc, mod_w, mod_b):
    depth, d, n = mod_w.shape
    tn = n // 6
    return pl.pallas_call(
        _mod_kernel,
        grid=(depth, n // tn),
        in_specs=[pl.BlockSpec((SUBLANES, d), lambda l, j: (0, 0)),
                  pl.BlockSpec((None, d, tn), lambda l, j: (l, 0, j)),
                  pl.BlockSpec((None, 1, tn), lambda l, j: (l, 0, j))],
        out_specs=pl.BlockSpec((None, SUBLANES, tn), lambda l, j: (l, 0, j)),
        out_shape=jax.ShapeDtypeStruct((depth, SUBLANES, n), F32),
        compiler_params=_params("parallel", "parallel"),
    )(cc, mod_w, mod_b.reshape(depth, 1, n))


def _inproj_kernel(x_ref, mod_ref, w_ref, cs_ref, qk_ref, va_ref, g_ref, z_ref, ab_ref, gate_ref):
    m = mod_ref[...]
    h = (x_ref[...] * (1.0 + m[1:2]) + m[0:1]).astype(BF16)
    cs = cs_ref[...]
    cos, sin = cs[:, :LANES], cs[:, LANES:]
    qk = _dot(h, w_ref[:, OFF_QK:OFF_VA])
    for j in range(2 * A_HEADS):
        a = qk[:, j * LANES:(j + 1) * LANES]
        qk_ref[:, j * LANES:(j + 1) * LANES] = (a * cos + pltpu.roll(a, LANES // 2, 1) * sin).astype(BF16)
    va = _dot(h, w_ref[:, OFF_VA:OFF_G])
    for j in range(A_HEADS):
        va_ref[j * LANES:(j + 1) * LANES, :] = va[:, j * LANES:(j + 1) * LANES].T.astype(BF16)
    g_ref[...] = _dot(h, w_ref[:, OFF_G:OFF_Z])
    z_ref[...] = _dot(h, w_ref[:, OFF_Z:OFF_AB])
    ab_ref[...] = _dot(h, w_ref[:, OFF_AB:OFF_GATE])
    gate_ref[...] = _sigmoid(_dot(h, w_ref[:, OFF_GATE:])).astype(BF16)


def _inproj(x_all, modt, w_cat, cs_tab, n_lat_tiles, tiles_per_seq):
    t, d = x_all.shape
    n = w_cat.shape[1]
    d_gate = n - OFF_GATE

    def row(i):
        return (i, 0)

    def cs_map(i):
        return (jnp.where(i < n_lat_tiles, i % tiles_per_seq, tiles_per_seq), 0)

    out_shape = [jax.ShapeDtypeStruct((t, 2 * A_WIDTH), BF16),
                 jax.ShapeDtypeStruct((A_WIDTH, t), BF16),
                 jax.ShapeDtypeStruct((t, G_QKV_WIDTH), F32),
                 jax.ShapeDtypeStruct((t, G_V_WIDTH), F32),
                 jax.ShapeDtypeStruct((t, LANES), F32),
                 jax.ShapeDtypeStruct((t, d_gate), BF16)]
    out_specs = [pl.BlockSpec((TM, s.shape[1]), row) for s in out_shape]
    out_specs[1] = pl.BlockSpec((A_WIDTH, TM), lambda i: (0, i))
    return pl.pallas_call(
        _inproj_kernel,
        grid=(t // TM,),
        in_specs=[pl.BlockSpec((TM, d), row),
                  pl.BlockSpec((None, 6, d), lambda i: (i, 0, 0)),
                  pl.BlockSpec((d, n), lambda i: (0, 0)),
                  pl.BlockSpec((TM, 2 * LANES), cs_map)],
        out_specs=out_specs,
        out_shape=out_shape,
        compiler_params=_params("parallel"),
    )(x_all, modt, w_cat, cs_tab)


def _row_fold(x, op):
    rows, n = x.shape
    return op(x.reshape(rows // SUBLANES, SUBLANES, n), axis=0)


def _attn_scores(q, kv_refs, s_refs):
    lane = lax.broadcasted_iota(jnp.int32, (1, LANES), 1)
    comp1 = ((lane // (A_HEAD_DIM // 2)) % 2) == 1
    qn = (jnp.where(comp1, jnp.zeros_like(q), q), jnp.where(comp1, q, jnp.zeros_like(q)))
    m_part, off = [None, None], 0
    for k_ref, _ in kv_refs:
        nk = k_ref.shape[0]
        for n in range(2):
            st = _dot_nt(k_ref[...], qn[n])
            s_refs[n][pl.ds(off, nk), :] = st
            mc = _row_fold(st, jnp.max)
            m_part[n] = mc if m_part[n] is None else jnp.maximum(m_part[n], mc)
        off += nk
    return [jnp.max(mp, axis=0, keepdims=True) for mp in m_part]


def _attn_values(lam_ref, kv_refs, g_ref, s_refs, m, lam_init):
    lv = lam_ref[...]
    lam = (jnp.exp(jnp.sum(lv[0:1] * lv[1:2], axis=-1, keepdims=True))
           - jnp.exp(jnp.sum(lv[2:3] * lv[3:4], axis=-1, keepdims=True)) + lam_init)
    l_part, ot, off = [None, None], [None, None], 0
    for k_ref, vt_ref in kv_refs:
        for r0 in range(0, k_ref.shape[0], ATTN_KEY_CHUNK):
            for n in range(2):
                p = jnp.exp(s_refs[n][pl.ds(off + r0, ATTN_KEY_CHUNK), :] - m[n])
                lc = _row_fold(p, jnp.sum)
                l_part[n] = lc if l_part[n] is None else l_part[n] + lc
                oc = _dot(vt_ref[:, pl.ds(r0, ATTN_KEY_CHUNK)], p.astype(BF16))
                ot[n] = oc if ot[n] is None else ot[n] + oc
        off += k_ref.shape[0]
    outs = [ot[n] / jnp.sum(l_part[n], axis=0, keepdims=True) for n in range(2)]
    ot = outs[0] - lam * outs[1]
    r = ot * lax.rsqrt(jnp.mean(ot * ot, axis=0, keepdims=True) + EPS)
    return (r * g_ref[...] * (1.0 - lam_init)).T.astype(BF16)


def _attn_kernel(lam_ref, q_ref, *refs, lam_init, n_kv, n_sub):
    kv_refs = tuple((refs[2 * j], refs[2 * j + 1]) for j in range(n_kv))
    g_ref, o_ref = refs[2 * n_kv], refs[2 * n_kv + 1]
    s_all = refs[2 * n_kv + 2:]
    tq = q_ref.shape[0] // n_sub
    def scores(j):
        s_refs = (s_all[2 * j], s_all[2 * j + 1])
        return s_refs, _attn_scores(q_ref[pl.ds(j * tq, tq), :], kv_refs, s_refs)

    nxt = scores(0)
    for j in range(n_sub):
        s_refs, m = nxt
        if j + 1 < n_sub:
            nxt = scores(j + 1)
        o_ref[pl.ds(j * tq, tq), :] = _attn_values(lam_ref, kv_refs, g_ref, s_refs, m, lam_init)


def _attention(qk, vt, lamv, subln_g, *, lam_init, b, s, nc, latent):
    n_sub = ATTN_Q_SUBTILES if latent else 1
    rows = TM * n_sub
    ctx0 = (b * s) // nc
    kv_specs = [pl.BlockSpec((nc, LANES), lambda bi, h, i: (ctx0 + bi, A_HEADS + h)),
                pl.BlockSpec((LANES, nc), lambda bi, h, i: (h, ctx0 + bi))]
    kv_args = [qk, vt]
    if latent:
        n_q = s // rows
        q_map = lambda bi, h, i: (bi * n_q + i, h)
        kv_specs += [pl.BlockSpec((s, LANES), lambda bi, h, i: (bi, A_HEADS + h)),
                     pl.BlockSpec((LANES, s), lambda bi, h, i: (h, bi))]
        kv_args += [qk, vt]
        n_keys, t_out = nc + s, b * s
    else:
        n_q = nc // rows
        q_map = lambda bi, h, i: ((b * s) // rows + bi * n_q + i, h)
        n_keys, t_out = nc, b * nc
    return pl.pallas_call(
        functools.partial(_attn_kernel, lam_init=lam_init, n_kv=len(kv_args) // 2, n_sub=n_sub),
        grid=(b, A_HEADS, n_q),
        in_specs=[pl.BlockSpec((SUBLANES, LANES), lambda bi, h, i: (0, 0)),
                  pl.BlockSpec((rows, LANES), q_map)] + kv_specs
                 + [pl.BlockSpec((LANES, 1), lambda bi, h, i: (0, 0))],
        out_specs=pl.BlockSpec((rows, LANES), lambda bi, h, i: (bi * n_q + i, h)),
        out_shape=jax.ShapeDtypeStruct((t_out, A_WIDTH), BF16),
        scratch_shapes=[pltpu.VMEM((n_keys, TM), F32)] * (2 * n_sub),
        compiler_params=_params("parallel", "parallel", "arbitrary"),
    )(lamv, qk, *kv_args, subln_g)


def _gdn_prep_kernel(g_ref, prev_ref, next_ref, ab_ref, cw_ref, gp_ref, q_ref, k_ref, v_ref, gc_ref, ext_ref,
                     *, n_lat_tiles, tiles_per_seq):
    i = pl.program_id(0)
    is_lat = i < n_lat_tiles
    first = jnp.logical_or(jnp.logical_not(is_lat), i % tiles_per_seq == 0)
    last = jnp.logical_or(jnp.logical_not(is_lat), i % tiles_per_seq == tiles_per_seq - 1)
    ext_ref[0:SUBLANES, :] = jnp.where(first, 0.0, prev_ref[...])
    ext_ref[SUBLANES:SUBLANES + TM, :] = g_ref[...]
    ext_ref[SUBLANES + TM:, :] = jnp.where(last, 0.0, next_ref[...])
    cw = cw_ref[...]
    acc = None
    for j in range(CONV_K):
        term = ext_ref[pl.ds(SUBLANES - CONV_K // 2 + j, TM), :] * cw[j:j + 1]
        acc = term if acc is None else acc + term
    y = _silu(acc)
    for h in range(G_HEADS):
        for base, ref, scale in ((0, q_ref, G_KEY_DIM ** -0.5), (G_QK_WIDTH, k_ref, 1.0)):
            xh = y[:, base + h * LANES:base + (h + 1) * LANES]
            nrm = xh * lax.rsqrt(jnp.sum(xh * xh, axis=-1, keepdims=True) + EPS)
            ref[:, h * LANES:(h + 1) * LANES] = nrm * scale
    v_ref[...] = y[:, 2 * G_QK_WIDTH:]

    ab = ab_ref[...]
    gp = gp_ref[...]
    lane = lax.broadcasted_iota(jnp.int32, (1, LANES), 1)
    z = ab + gp[1:2]
    softplus = jnp.maximum(z, 0.0) + jnp.log(1.0 + jnp.exp(-jnp.abs(z)))
    g = jnp.where(lane < 2 * G_HEADS, -jnp.exp(gp[0:1]) * softplus, 0.0)
    beta = _sigmoid(ab)
    ri = lax.broadcasted_iota(jnp.int32, (TM, TM), 0)
    ci = lax.broadcasted_iota(jnp.int32, (TM, TM), 1)
    same = (ri // CHUNK) == (ci // CHUNK)
    lower = jnp.where(jnp.logical_and(same, ci <= ri), 1.0, 0.0)
    upper = jnp.where(jnp.logical_and(same, ci >= ri), 1.0, 0.0)
    cum_f = _dot_f32(lower, g)
    cum_b = _dot_f32(upper, g)
    total = cum_f + cum_b - g
    cum = jnp.where(lane < G_HEADS, cum_f, cum_b)
    eg = jnp.exp(cum)
    ek = jnp.exp(total - cum)
    et = jnp.exp(total)
    out = jnp.where(lane < GC_BETA, cum, 0.0)
    out = jnp.where(jnp.logical_and(lane >= GC_BETA, lane < GC_EG), beta, out)
    out = jnp.where(jnp.logical_and(lane >= GC_EG, lane < GC_EK), pltpu.roll(eg, GC_EG, 1), out)
    out = jnp.where(jnp.logical_and(lane >= GC_EK, lane < GC_ET), pltpu.roll(ek, GC_EK, 1), out)
    out = jnp.where(jnp.logical_and(lane >= GC_ET, lane < GC_ET + 8), pltpu.roll(et, GC_ET, 1), out)
    gc_ref[...] = out


def _gdn_prep(g, ab, conv_w8, gparams, n_lat_tiles, tiles_per_seq):
    t, c = g.shape
    nblk8 = t // SUBLANES
    per = TM // SUBLANES
    row = lambda i: (i, 0)
    out_shape = [jax.ShapeDtypeStruct((t, G_QK_WIDTH), F32),
                 jax.ShapeDtypeStruct((t, G_QK_WIDTH), F32),
                 jax.ShapeDtypeStruct((t, G_V_WIDTH), F32),
                 jax.ShapeDtypeStruct((t, LANES), F32)]
    return pl.pallas_call(
        functools.partial(_gdn_prep_kernel, n_lat_tiles=n_lat_tiles, tiles_per_seq=tiles_per_seq),
        grid=(t // TM,),
        in_specs=[pl.BlockSpec((TM, c), row),
                  pl.BlockSpec((SUBLANES, c), lambda i: (jnp.maximum(i * per - 1, 0), 0)),
                  pl.BlockSpec((SUBLANES, c), lambda i: (jnp.minimum((i + 1) * per, nblk8 - 1), 0)),
                  pl.BlockSpec((TM, LANES), row),
                  pl.BlockSpec((SUBLANES, c), lambda i: (0, 0)),
                  pl.BlockSpec((SUBLANES, LANES), lambda i: (0, 0))],
        out_specs=[pl.BlockSpec((TM, s.shape[1]), row) for s in out_shape],
        out_shape=out_shape,
        scratch_shapes=[pltpu.VMEM((TM + 2 * SUBLANES, c), F32)],
        compiler_params=_params("parallel"),
    )(g, g, g, ab, conv_w8, gparams)


def _gdn_masks(c, reverse):
    ii = lax.broadcasted_iota(jnp.int32, (c, 2 * c), 0)
    lane = lax.broadcasted_iota(jnp.int32, (c, 2 * c), 1)
    left = lane < c
    jj = jnp.where(left, lane, lane - c)
    causal = (ii <= jj) if reverse else (ii >= jj)
    strict = (ii < jj) if reverse else (ii > jj)
    return dict(left=left, diag=ii == jj, causal=causal, strict=strict)


def _gdn_setup(q, k, v, gc, gct, col, mk):
    c = q.shape[0]
    gcol = gc[:, GC_CUM + col:GC_CUM + col + 1]
    beta = gc[:, GC_BETA + col:GC_BETA + col + 1]
    eg = gc[:, GC_EG + col:GC_EG + col + 1]
    ek = gc[:, GC_EK + col:GC_EK + col + 1]
    et = gc[0:1, GC_ET + col:GC_ET + col + 1]
    grow = gct[GC_CUM + col:GC_CUM + col + 1, :]
    decay = jnp.exp(jnp.where(mk["causal"], gcol - grow, -1e30))
    kb = k * beta
    k16 = k.astype(BF16)
    both = _dot_nt(jnp.concatenate([kb.astype(BF16), q.astype(BF16)], axis=0),
                   jnp.concatenate([k16, k16], axis=0))
    qk = jnp.where(jnp.logical_and(mk["causal"], mk["left"]), both[c:] * decay, 0.0)
    z = jnp.where(mk["left"], jnp.where(mk["diag"], 1.0, 0.0), jnp.where(mk["strict"], -(both[:c] * decay), 0.0))
    rhs = jnp.concatenate([v * beta, kb * eg], axis=1)
    return dict(z=z, qk=qk.astype(BF16), rhs=rhs, qd=(q * eg).astype(BF16), kdt=(k * ek).T.astype(BF16), et=et)


def _gdn_level(z, mk):
    z16 = z.astype(BF16)
    return jnp.where(mk["left"], z, 0.0) + _dot(z16, jnp.concatenate([jnp.zeros_like(z16), z16], axis=0))


def _gdn_solve(ch, mk):
    y16 = jnp.where(jnp.logical_and(mk["left"], jnp.logical_not(mk["diag"])), ch["z"], 0.0).astype(BF16)
    rhs16 = ch["rhs"].astype(BF16)
    sol = ch["rhs"] + _dot(y16, jnp.concatenate([rhs16, rhs16], axis=0))
    dv = sol.shape[1] // 2
    return sol[:, :dv], jnp.concatenate([sol[:, dv:].astype(BF16), ch["qd"]], axis=0)


def _gdn_kernel(qf_ref, kf_ref, vf_ref, gf_ref, qb_ref, kb_ref, vb_ref, gb_ref, of_ref, ob_ref, s_ref):
    @pl.when(pl.program_id(1) == 0)
    def _():
        s_ref[...] = jnp.zeros_like(s_ref)

    n_chunks = qf_ref.shape[0] // CHUNK
    masks = (_gdn_masks(CHUNK, False), _gdn_masks(CHUNK, True))
    dirs = ((qf_ref, kf_ref, vf_ref, gf_ref, of_ref), (qb_ref, kb_ref, vb_ref, gb_ref, ob_ref))
    state = [s_ref[i] for i in range(2 * G_HEADS)]
    for n0 in range(0, n_chunks, GDN_CHUNKS_PER_BATCH):
        chains = []
        for n in range(n0, n0 + GDN_CHUNKS_PER_BATCH):
            for d, (q_ref, k_ref, v_ref, g_ref, o_ref) in enumerate(dirs):
                rows = pl.ds((n_chunks - 1 - n if d else n) * CHUNK, CHUNK)
                gc = g_ref[rows, :]
                gct = jnp.concatenate([gc, gc], axis=0).T
                for h in range(G_HEADS):
                    cols = pl.ds(h * LANES, LANES)
                    ch = _gdn_setup(q_ref[rows, cols], k_ref[rows, cols], v_ref[rows, cols], gc, gct,
                                    d * G_HEADS + h, masks[d])
                    ch.update(d=d, h=h, rows=rows, cols=cols, out=o_ref)
                    chains.append(ch)
        for _ in range(int(math.log2(CHUNK))):
            for ch in chains:
                ch["z"] = _gdn_level(ch["z"], masks[ch["d"]])
        for ch in chains:
            ch["u"], ch["wq"] = _gdn_solve(ch, masks[ch["d"]])
        per_chunk = 2 * G_HEADS
        for c0 in range(0, len(chains), per_chunk):
            group = chains[c0:c0 + per_chunk]
            ws = [_dot(ch["wq"], state[ch["d"] * G_HEADS + ch["h"]].astype(BF16)) for ch in group]
            for ch, w in zip(group, ws):
                i = ch["d"] * G_HEADS + ch["h"]
                v_new = (ch["u"] - w[:CHUNK]).astype(BF16)
                o = w[CHUNK:] + _dot(ch["qk"], jnp.concatenate([v_new, v_new], axis=0))
                state[i] = state[i] * ch["et"] + _dot(ch["kdt"], v_new)
                ch["out"][ch["rows"], ch["cols"]] = o
    for i in range(2 * G_HEADS):
        s_ref[i] = state[i]


def _gdn(qg, kg, vg, gc, *, b, s, nc):
    t = qg.shape[0]
    tps = s // TM
    nct = nc // TM
    n_steps = nct + tps
    ctx_tile0 = (b * s) // TM

    def fwd(bi, i):
        return (jnp.where(i < nct, ctx_tile0 + bi * nct + i, bi * tps + (i - nct)), 0)

    def bwd(bi, i):
        return (jnp.where(i < nct, ctx_tile0 + bi * nct + (nct - 1 - i), bi * tps + (tps - 1 - (i - nct))), 0)

    def specs(row_map):
        return [pl.BlockSpec((TM, G_QK_WIDTH), row_map), pl.BlockSpec((TM, G_QK_WIDTH), row_map),
                pl.BlockSpec((TM, G_V_WIDTH), row_map), pl.BlockSpec((TM, LANES), row_map)]

    return pl.pallas_call(
        _gdn_kernel,
        grid=(b, n_steps),
        in_specs=specs(fwd) + specs(bwd),
        out_specs=[pl.BlockSpec((TM, G_V_WIDTH), fwd), pl.BlockSpec((TM, G_V_WIDTH), bwd)],
        out_shape=[jax.ShapeDtypeStruct((t, G_V_WIDTH), F32)] * 2,
        scratch_shapes=[pltpu.VMEM((2 * G_HEADS, G_KEY_DIM, G_VAL_DIM), F32)],
        compiler_params=_params("parallel", "arbitrary"),
    )(qg, kg, vg, gc, qg, kg, vg, gc)


def _layer_norm(r, g, b):
    mu = jnp.mean(r, axis=-1, keepdims=True)
    rc = r - mu
    var = jnp.mean(rc * rc, axis=-1, keepdims=True)
    return rc * lax.rsqrt(var + EPS) * g + b


def _merge_kernel(x_ref, yal_ref, yac_ref, of_ref, ob_ref, z_ref, gate_ref, mod_ref, wpa_ref, wpb_ref, wo_ref, on_ref,
                  lng_ref, lnb_ref, rw_ref, rb_ref, xo_ref, ht_ref, cls_ref, ew_ref, *, alpha, n_lat_tiles):
    d = x_ref.shape[1]
    o = of_ref[...] + ob_ref[...]
    parts = []
    for h in range(G_HEADS):
        oh = o[:, h * LANES:(h + 1) * LANES]
        parts.append(oh * lax.rsqrt(jnp.mean(oh * oh, axis=-1, keepdims=True) + EPS) * on_ref[...])
    yb = jnp.concatenate(parts, axis=1) * _silu(z_ref[...])
    ya = jnp.where(pl.program_id(0) < n_lat_tiles, yal_ref[...], yac_ref[...])
    pa = _dot(ya, wpa_ref[...])
    pb = _dot(yb.astype(BF16), wpb_ref[...])
    mix = gate_ref[:, :d] * pa + gate_ref[:, d:] * pb
    y = _dot(mix.astype(BF16), wo_ref[...])
    m = mod_ref[...]
    xn = _layer_norm(alpha * x_ref[...] + m[2:3] * y, lng_ref[...], lnb_ref[...])
    xo_ref[...] = xn
    h2 = xn * (1.0 + m[4:5]) + m[3:4]
    _to_token_tiles(ht_ref, h2)

    score = _sigmoid(_dot_nt(rw_ref[...], h2.astype(BF16)))
    sel = score + rb_ref[...]
    rows = [sel[e:e + 1] for e in range(N_EXPERTS)]
    best = None
    best_val = None
    for gi in range(N_GROUPS):
        a, b, c, dd = rows[gi * EXPERTS_PER_GROUP:(gi + 1) * EXPERTS_PER_GROUP]
        top2 = jnp.maximum(jnp.maximum(jnp.maximum(a + b, a + c), jnp.maximum(a + dd, b + c)),
                           jnp.maximum(b + dd, c + dd))
        if gi == 0:
            best, best_val = jnp.zeros(top2.shape, jnp.int32), top2
        else:
            upd = top2 > best_val
            best = jnp.where(upd, gi, best)
            best_val = jnp.where(upd, top2, best_val)
    in_group, in_score = [], []
    for j in range(EXPERTS_PER_GROUP):
        v = rows[(N_GROUPS - 1) * EXPERTS_PER_GROUP + j]
        sv = score[(N_GROUPS - 1) * EXPERTS_PER_GROUP + j:(N_GROUPS - 1) * EXPERTS_PER_GROUP + j + 1]
        for gi in range(N_GROUPS - 2, -1, -1):
            e = gi * EXPERTS_PER_GROUP + j
            v = jnp.where(best == gi, rows[e], v)
            sv = jnp.where(best == gi, score[e:e + 1], sv)
        in_group.append(v)
        in_score.append(sv)
    l0 = jnp.zeros(best.shape, jnp.int32)
    m0 = in_group[0]
    for j in range(1, EXPERTS_PER_GROUP):
        upd = in_group[j] > m0
        l0 = jnp.where(upd, j, l0)
        m0 = jnp.where(upd, in_group[j], m0)
    l1 = jnp.zeros(best.shape, jnp.int32)
    m1 = jnp.full(m0.shape, -jnp.inf, F32)
    for j in range(EXPERTS_PER_GROUP):
        upd = jnp.logical_and(l0 != j, in_group[j] > m1)
        l1 = jnp.where(upd, j, l1)
        m1 = jnp.where(upd, in_group[j], m1)
    lo = jnp.minimum(l0, l1)
    hi = jnp.maximum(l0, l1)
    w_lo = jnp.zeros(m0.shape, F32)
    w_hi = jnp.zeros(m0.shape, F32)
    for j in range(EXPERTS_PER_GROUP):
        w_lo = jnp.where(lo == j, in_score[j], w_lo)
        w_hi = jnp.where(hi == j, in_score[j], w_hi)
    pair = jnp.where(lo == 0, hi - 1, jnp.where(lo == 1, hi + 1, len(PAIRS) - 1))
    tot = w_lo + w_hi
    ri = lax.broadcasted_iota(jnp.int32, cls_ref.shape, 0)
    cls_ref[...] = jnp.where(ri == 0, best * len(PAIRS) + pair, 0)
    ew_ref[...] = jnp.where(ri == 0, w_lo / tot, jnp.where(ri == 1, w_hi / tot, 0.0))


def _merge(x_all, ya_lat, ya_ctx, o_f, o_b, z, gate, modt, wpa, wpb, wo, onorm, lng, lnb, rw_t, rb, *, alpha, t):
    d = x_all.shape[1]
    r = d // LANES
    n_lat_tiles = ya_lat.shape[0] // TM
    row = lambda i: (i, 0)
    col = lambda i: (0, i)
    const = lambda i: (0, 0)
    out_shape = [jax.ShapeDtypeStruct((t, d), F32),
                 jax.ShapeDtypeStruct((t * r, LANES), F32),
                 jax.ShapeDtypeStruct((SUBLANES, t), jnp.int32),
                 jax.ShapeDtypeStruct((SUBLANES, t), F32)]
    return pl.pallas_call(
        functools.partial(_merge_kernel, alpha=alpha, n_lat_tiles=n_lat_tiles),
        grid=(t // TM,),
        in_specs=[pl.BlockSpec((TM, d), row),
                  pl.BlockSpec((TM, A_WIDTH), lambda i: (jnp.minimum(i, n_lat_tiles - 1), 0)),
                  pl.BlockSpec((TM, A_WIDTH), lambda i: (jnp.maximum(i - n_lat_tiles, 0), 0)),
                  pl.BlockSpec((TM, G_V_WIDTH), row),
                  pl.BlockSpec((TM, G_V_WIDTH), row),
                  pl.BlockSpec((TM, G_V_WIDTH), row),
                  pl.BlockSpec((TM, 2 * d), row),
                  pl.BlockSpec((None, 6, d), lambda i: (i, 0, 0)),
                  pl.BlockSpec(wpa.shape, const),
                  pl.BlockSpec(wpb.shape, const),
                  pl.BlockSpec(wo.shape, const),
                  pl.BlockSpec((1, LANES), const),
                  pl.BlockSpec((1, d), const),
                  pl.BlockSpec((1, d), const),
                  pl.BlockSpec(rw_t.shape, const),
                  pl.BlockSpec(rb.shape, const)],
        out_specs=[pl.BlockSpec((TM, d), row), pl.BlockSpec((TM * r, LANES), row),
                   pl.BlockSpec((SUBLANES, TM), col), pl.BlockSpec((SUBLANES, TM), col)],
        out_shape=out_shape,
        compiler_params=_params("parallel"),
    )(x_all, ya_lat, ya_ctx, o_f, o_b, z, gate, modt, wpa, wpb, wo, onorm, lng, lnb, rw_t, rb)


def _expert_kernel(ea_ref, eb_ref, nu_ref, nv_ref, src_ref, h_hbm, ws_ref, w1a_ref, w3a_ref, w2a_ref,
                   w1b_ref, w3b_ref, w2b_ref, y_hbm, xbuf, ybuf, gsem, ssem, *, r):
    i = pl.program_id(0)
    n_used = nu_ref[0]
    slot = i % 2

    def gather_copy(blk, buf, j):
        tok = src_ref[blk * MOE_BLOCK + j]
        return pltpu.make_async_copy(h_hbm.at[pl.ds(pl.multiple_of(tok * r, r), r), :],
                                     xbuf.at[buf, pl.ds(pl.multiple_of(j * r, r), r), :], gsem.at[buf])

    def scatter_copy(blk, buf, j):
        tok = src_ref[blk * MOE_BLOCK + j]
        return pltpu.make_async_copy(ybuf.at[buf, pl.ds(pl.multiple_of(j * r, r), r), :],
                                     y_hbm.at[pl.ds(pl.multiple_of(tok * r, r), r), :], ssem.at[buf])

    def for_slots(fn):
        def body(j, carry):
            fn(j)
            return carry
        lax.fori_loop(0, MOE_BLOCK, body, 0, unroll=DMA_GROUP)

    def for_token_slots(blk, fn):
        nv = nv_ref[blk]

        def body(g, carry):
            j0 = g * DMA_GROUP

            @pl.when(j0 + DMA_GROUP <= nv)
            def _():
                for k in range(DMA_GROUP):
                    fn(j0 + k)

            @pl.when(jnp.logical_and(j0 < nv, j0 + DMA_GROUP > nv))
            def _():
                for k in range(DMA_GROUP):
                    pl.when(j0 + k < nv)(functools.partial(fn, j0 + k))
            return carry
        lax.fori_loop(0, MOE_BLOCK // DMA_GROUP, body, 0)

    @pl.when(i == 0)
    def _():
        for_slots(lambda j: gather_copy(0, 0, j).start())

    @pl.when(i < n_used)
    def _():
        for_slots(lambda j: gather_copy(i, slot, j).wait())

        @pl.when(i + 1 < n_used)
        def _():
            for_slots(lambda j: gather_copy(i + 1, 1 - slot, j).start())

        x = _from_token_tiles(xbuf.at[slot], MOE_BLOCK, r).astype(BF16)
        ws = ws_ref[...]

        ga, ua = _dot(x, w1a_ref[...]), _dot(x, w3a_ref[...])
        gb, ub = _dot(x, w1b_ref[...]), _dot(x, w3b_ref[...])
        ya = _dot((_silu(ga) * ua).astype(BF16), w2a_ref[...])
        yb = _dot((_silu(gb) * ub).astype(BF16), w2b_ref[...])
        y = ws[:, 0:1] * ya + ws[:, 1:2] * yb

        _to_token_tiles(ybuf.at[slot], y)

        @pl.when(i >= 1)
        def _():
            for_token_slots(i - 1, lambda j: scatter_copy(i - 1, 1 - slot, j).wait())

        for_token_slots(i, lambda j: scatter_copy(i, slot, j).start())

        @pl.when(i == n_used - 1)
        def _():
            for_token_slots(i, lambda j: scatter_copy(i, slot, j).wait())


def _experts(ht, w_slot, block_ea, block_eb, n_used, n_valid, slot_src, w1, w3, w2, n_out_tokens):
    d, f = w1.shape[1], w1.shape[2]
    r = d // LANES
    n_slots = slot_src.shape[0]

    def wspec(shape, which):
        if which == 0:
            return pl.BlockSpec((None,) + shape, lambda i, ea, eb, nu, nv, src: (ea[i], 0, 0))
        return pl.BlockSpec((None,) + shape, lambda i, ea, eb, nu, nv, src: (eb[i], 0, 0))

    grid_spec = pltpu.PrefetchScalarGridSpec(
        num_scalar_prefetch=5,
        grid=(n_slots // MOE_BLOCK,),
        in_specs=[pl.BlockSpec(memory_space=pl.ANY),
                  pl.BlockSpec((MOE_BLOCK, 2), lambda i, ea, eb, nu, nv, src: (i, 0)),
                  wspec((d, f), 0), wspec((d, f), 0), wspec((f, d), 0),
                  wspec((d, f), 1), wspec((d, f), 1), wspec((f, d), 1)],
        out_specs=pl.BlockSpec(memory_space=pl.ANY),
        scratch_shapes=[pltpu.VMEM((2, MOE_BLOCK * r, LANES), F32),
                        pltpu.VMEM((2, MOE_BLOCK * r, LANES), F32),
                        pltpu.SemaphoreType.DMA((2,)),
                        pltpu.SemaphoreType.DMA((2,))],
    )
    return pl.pallas_call(
        functools.partial(_expert_kernel, r=r),
        grid_spec=grid_spec,
        out_shape=jax.ShapeDtypeStruct((n_out_tokens * r, LANES), F32),
        compiler_params=_params("arbitrary"),
    )(block_ea, block_eb, n_used, n_valid, slot_src, ht, w_slot, w1, w3, w2, w1, w3, w2)


def _ln2_kernel(x_ref, yt_ref, mod_ref, g_ref, b_ref, o_ref, *, alpha):
    tm, d = x_ref.shape
    y = _from_token_tiles(yt_ref, tm, d // LANES)
    m = mod_ref[...]
    o_ref[...] = _layer_norm(alpha * x_ref[...] + m[5:6] * y, g_ref[...], b_ref[...])


def _ln2(x_all, yt, modt, g, b, *, alpha):
    t, d = x_all.shape
    r = d // LANES
    row = lambda i: (i, 0)
    const = lambda i: (0, 0)
    return pl.pallas_call(
        functools.partial(_ln2_kernel, alpha=alpha),
        grid=(t // TM,),
        in_specs=[pl.BlockSpec((TM, d), row), pl.BlockSpec((TM * r, LANES), row),
                  pl.BlockSpec((None, 6, d), lambda i: (i, 0, 0)),
                  pl.BlockSpec((1, d), const), pl.BlockSpec((1, d), const)],
        out_specs=pl.BlockSpec((TM, d), row),
        out_shape=jax.ShapeDtypeStruct((t, d), F32),
        compiler_params=_params("parallel"),
    )(x_all, yt, modt, g, b)


def _rope_table(s, nc):
    n_freq = A_HEAD_DIM // 4
    rows = s // GRID_W
    row = jnp.repeat(jnp.arange(rows, dtype=F32), GRID_W)
    col = jnp.tile(jnp.arange(GRID_W, dtype=F32), rows)
    inv = ROPE_BASE ** (-jnp.arange(n_freq, dtype=F32) / n_freq)
    ang = jnp.stack([row[:, None] * inv, col[:, None] * inv], axis=1)
    cos = jnp.cos(ang).reshape(s, 1, 1, 2, n_freq)
    sin = jnp.sin(ang).reshape(s, 1, 1, 2, n_freq)
    cos = jnp.broadcast_to(cos, (s, 2, 2, 2, n_freq)).reshape(s, LANES)
    sign = jnp.array([-1.0, 1.0], F32).reshape(1, 2, 1, 1, 1)
    sin = jnp.broadcast_to(sin * sign, (s, 2, 2, 2, n_freq)).reshape(s, LANES)
    lat = jnp.concatenate([cos, sin], axis=1)
    ident = jnp.concatenate([jnp.ones((nc, LANES), F32), jnp.zeros((nc, LANES), F32)], axis=1)
    return jnp.concatenate([lat, ident], axis=0)


def _rope_perm(w):
    d = w.shape[0]
    n_freq = A_HEAD_DIM // 4
    return w.reshape(d, A_HEADS, 2, 2, 2, n_freq).transpose(0, 1, 4, 2, 3, 5).reshape(d, A_WIDTH)


def _pack_w_in(w):
    d = w.shape[0]
    sizes = (A_WIDTH, A_WIDTH, A_WIDTH, G_QKV_WIDTH, G_V_WIDTH, 2 * G_HEADS, 2 * G_HEADS)
    offs = [0]
    for sz in sizes:
        offs.append(offs[-1] + sz)
    wq, wk, wv, wg, wz, wa, wb = (w[:, offs[i]:offs[i + 1]] for i in range(len(sizes)))
    wgate = w[:, offs[-1]:]
    wab = jnp.concatenate([wa, wb, jnp.zeros((d, LANES - 4 * G_HEADS), w.dtype)], axis=1)
    cat = jnp.concatenate([_rope_perm(wq) * (A_HEAD_DIM ** -0.5), _rope_perm(wk), wv, wg, wz, wab, wgate], axis=1)
    return cat.astype(BF16)


def _pad_rows(a, rows):
    return jnp.concatenate([a, jnp.zeros((rows - a.shape[0],) + a.shape[1:], a.dtype)], axis=0)


def _pad_lanes(a):
    return jnp.concatenate([a, jnp.zeros(a.shape[:-1] + (LANES - a.shape[-1],), a.dtype)], axis=-1)


def _dispatch(cls, ew, t):
    onehot = (cls[:, None] == jnp.arange(N_CLASSES, dtype=jnp.int32)[None, :]).astype(jnp.int32)
    csum = jnp.cumsum(onehot, axis=0)
    rank = jnp.sum(csum * onehot, axis=1) - 1
    counts = csum[-1]
    padded = (counts + MOE_BLOCK - 1) // MOE_BLOCK * MOE_BLOCK
    pad_end = jnp.cumsum(padded)
    pad_start = pad_end - padded
    dest = pad_start[cls] + rank
    n_blocks = -(-t // MOE_BLOCK) + N_CLASSES
    per_tok = jnp.concatenate([jnp.arange(t, dtype=F32)[:, None], ew], axis=1)
    per_slot = jnp.zeros((n_blocks * MOE_BLOCK, per_tok.shape[1]), F32).at[dest].set(per_tok)
    slot_src = per_slot[:, 0].astype(jnp.int32)
    w_slot = per_slot[:, 1:]
    block0 = jnp.arange(n_blocks, dtype=jnp.int32) * MOE_BLOCK
    block_cls = jnp.minimum(jnp.searchsorted(pad_end, block0, side='right'), N_CLASSES - 1)
    n_valid = jnp.clip((pad_start + counts)[block_cls] - block0, 0, MOE_BLOCK).astype(jnp.int32)
    n_used = (pad_end[-1:] // MOE_BLOCK).astype(jnp.int32)
    return slot_src, w_slot, jnp.asarray(CLASS_EA)[block_cls], jnp.asarray(CLASS_EB)[block_cls], n_used, n_valid


def kernel(x, c, ctx, c_ctx, mod_w, mod_b, w_in, conv_w, lam_q1, lam_k1, lam_q2, lam_k2, subln_g, a_log, dt_bias,
           onorm_g, w_pa, w_pb, w_o, ln1_g, ln1_b, router_w, router_b, w_exp1, w_exp3, w_exp2, ln2_g, ln2_b):
    b, s, d = x.shape
    nc = ctx.shape[1]
    depth = mod_w.shape[0]
    assert s % (TM * ATTN_Q_SUBTILES) == 0 and nc % TM == 0 and s % nc == 0 and s % GRID_W == 0
    assert d == SUBLANES * LANES
    t_lat, t_ctx = b * s, b * nc
    t = t_lat + t_ctx
    n_lat_tiles = t_lat // TM
    tiles_per_seq = s // TM
    alpha = (2.0 * depth) ** 0.25

    x_all = jnp.concatenate([x.reshape(t_lat, d), ctx.reshape(t_ctx, d)], axis=0)
    cc = _pad_rows(jnp.concatenate([c, c_ctx[None, :]], axis=0), SUBLANES)
    mods = _mod_tables(cc, mod_w, mod_b)
    tile_row = jnp.concatenate([jnp.arange(n_lat_tiles, dtype=jnp.int32) // tiles_per_seq,
                                jnp.full((t_ctx // TM,), b, jnp.int32)])
    cs_tab = _rope_table(s, nc)
    rw_t = router_w.T.astype(BF16)
    rb = router_b.reshape(N_EXPERTS, 1)

    for l in range(depth):
        lam_init = 0.8 - 0.6 * math.exp(-0.3 * l)
        modt = mods[l].reshape(SUBLANES, 6, d)[tile_row]
        qk, vt, g, z, ab, gate = _inproj(x_all, modt, _pack_w_in(w_in[l]), cs_tab, n_lat_tiles, tiles_per_seq)

        lamv = _pad_rows(_pad_lanes(jnp.stack([lam_q1[l], lam_k1[l], lam_q2[l], lam_k2[l]])), SUBLANES)
        sg = subln_g[l].reshape(LANES, 1)
        ya_lat = _attention(qk, vt, lamv, sg, lam_init=lam_init, b=b, s=s, nc=nc, latent=True)
        ya_ctx = _attention(qk, vt, lamv, sg, lam_init=lam_init, b=b, s=s, nc=nc, latent=False)

        gparams = _pad_rows(_pad_lanes(jnp.stack([a_log[l].reshape(-1), dt_bias[l].reshape(-1)])), SUBLANES)
        qg, kg, vg, gc = _gdn_prep(g, ab, _pad_rows(conv_w[l], SUBLANES), gparams, n_lat_tiles, tiles_per_seq)
        o_f, o_b = _gdn(qg, kg, vg, gc, b=b, s=s, nc=nc)

        t_out = t_lat if l == depth - 1 else t
        x_all, ht, cls, ew = _merge(x_all, ya_lat, ya_ctx, o_f, o_b, z, gate, modt, w_pa[l].astype(BF16), w_pb[l].astype(BF16),
                                    w_o[l].astype(BF16), onorm_g[l].reshape(1, LANES), ln1_g[l].reshape(1, d),
                                    ln1_b[l].reshape(1, d), rw_t, rb, alpha=alpha, t=t_out)

        slot_src, w_slot, block_ea, block_eb, n_used, n_valid = _dispatch(cls[0], ew[:2].T, t_out)
        yt = _experts(ht, w_slot, block_ea, block_eb, n_used, n_valid, slot_src, w_exp1[l].astype(BF16),
                      w_exp3[l].astype(BF16), w_exp2[l].astype(BF16), t_out)
        x_all = _ln2(x_all, yt, modt, ln2_g[l].reshape(1, d), ln2_b[l].reshape(1, d), alpha=alpha)
    return x_all.reshape(b, s, d)
```

```python
import functools
import math

import numpy as np

import jax
import jax.numpy as jnp
from jax import lax
from jax.experimental import pallas as pl
from jax.experimental.pallas import tpu as pltpu

F32 = jnp.float32
BF16 = jnp.bfloat16

GRID_W = 64
A_HEADS = 4
A_HEAD_DIM = 64
A_WIDTH = A_HEADS * 2 * A_HEAD_DIM
G_HEADS = 4
G_KEY_DIM = 128
G_VAL_DIM = 128
G_QK_WIDTH = G_HEADS * G_KEY_DIM
G_V_WIDTH = G_HEADS * G_VAL_DIM
G_QKV_WIDTH = 2 * G_QK_WIDTH + G_V_WIDTH
CONV_K = 5
CHUNK = 64
N_EXPERTS = 16
N_GROUPS = 4
EXPERTS_PER_GROUP = N_EXPERTS // N_GROUPS
MOE_BLOCK = 256
ROPE_BASE = 10000.0
EPS = 1e-6

LANES = 128
SUBLANES = 8
TM = 256
DMA_GROUP = 32
GDN_CHUNKS_PER_BATCH = 4
ATTN_KEY_CHUNK = 256
ATTN_Q_SUBTILES = 4
VMEM_LIMIT = 56 * 1024 * 1024

OFF_QK = 0
OFF_VA = 2 * A_WIDTH
OFF_G = OFF_VA + A_WIDTH
OFF_Z = OFF_G + G_QKV_WIDTH
OFF_AB = OFF_Z + G_V_WIDTH
OFF_GATE = OFF_AB + LANES
GC_CUM, GC_BETA, GC_EG, GC_EK, GC_ET = 0, 8, 16, 24, 32

PAIRS = tuple((lo, hi) for lo in range(EXPERTS_PER_GROUP) for hi in range(lo + 1, EXPERTS_PER_GROUP))
N_CLASSES = N_GROUPS * len(PAIRS)
CLASS_EA = np.array([g * EXPERTS_PER_GROUP + lo for g in range(N_GROUPS) for lo, _ in PAIRS], np.int32)
CLASS_EB = np.array([g * EXPERTS_PER_GROUP + hi for g in range(N_GROUPS) for _, hi in PAIRS], np.int32)


def _sigmoid(x):
    return 1.0 / (1.0 + jnp.exp(-x))


def _silu(x):
    return x * _sigmoid(x)


def _dot(a, b):
    return jnp.dot(a, b, preferred_element_type=F32)


def _dot_nt(a, b):
    return lax.dot_general(a, b, (((1,), (1,)), ((), ())), preferred_element_type=F32)


def _dot_f32(a, b):
    return jnp.dot(a, b, preferred_element_type=F32, precision=lax.Precision.HIGHEST)


def _params(*sem):
    return pltpu.CompilerParams(dimension_semantics=sem, vmem_limit_bytes=VMEM_LIMIT)


def _to_token_tiles(ref, x):
    rows, width = x.shape
    r = width // LANES
    for c in range(r):
        ref[pl.ds(c, rows, stride=r), :] = x[:, c * LANES:(c + 1) * LANES]


def _from_token_tiles(ref, rows, r):
    return jnp.concatenate([ref[pl.ds(c, rows, stride=r), :] for c in range(r)], axis=-1)


def _mod_kernel(cc_ref, w_ref, b_ref, o_ref):
    h = _silu(cc_ref[...])
    o_ref[...] = _dot(h.astype(BF16), w_ref[...].astype(BF16)) + b_ref[...]


def _mod_tables(cc, mod_w, mod_b):
    depth, d, n = mod_w.shape
    tn = n // 6
    return pl.pallas_call(
        _mod_kernel,
        grid=(depth, n // tn),
        in_specs=[pl.BlockSpec((SUBLANES, d), lambda l, j: (0, 0)),
                  pl.BlockSpec((None, d, tn), lambda l, j: (l, 0, j)),
                  pl.BlockSpec((None, 1, tn), lambda l, j: (l, 0, j))],
        out_specs=pl.BlockSpec((None, SUBLANES, tn), lambda l, j: (l, 0, j)),
        out_shape=jax.ShapeDtypeStruct((depth, SUBLANES, n), F32),
        compiler_params=_params("parallel", "parallel"),
    )(cc, mod_w, mod_b.reshape(depth, 1, n))


def _inproj_kernel(x_ref, mod_ref, w_ref, cs_ref, qk_ref, va_ref, g_ref, z_ref, ab_ref, gate_ref):
    m = mod_ref[...]
    h = (x_ref[...] * (1.0 + m[1:2]) + m[0:1]).astype(BF16)
    cs = cs_ref[...]
    cos, sin = cs[:, :LANES], cs[:, LANES:]
    qk = _dot(h, w_ref[:, OFF_QK:OFF_VA])
    for j in range(2 * A_HEADS):
        a = qk[:, j * LANES:(j + 1) * LANES]
        qk_ref[:, j * LANES:(j + 1) * LANES] = (a * cos + pltpu.roll(a, LANES // 2, 1) * sin).astype(BF16)
    va = _dot(h, w_ref[:, OFF_VA:OFF_G])
    for j in range(A_HEADS):
        va_ref[j * LANES:(j + 1) * LANES, :] = va[:, j * LANES:(j + 1) * LANES].T.astype(BF16)
    g_ref[...] = _dot(h, w_ref[:, OFF_G:OFF_Z])
    z_ref[...] = _dot(h, w_ref[:, OFF_Z:OFF_AB])
    ab_ref[...] = _dot(h, w_ref[:, OFF_AB:OFF_GATE])
    gate_ref[...] = _sigmoid(_dot(h, w_ref[:, OFF_GATE:])).astype(BF16)


def _inproj(x_all, modt, w_cat, cs_tab, n_lat_tiles, tiles_per_seq):
    t, d = x_all.shape
    n = w_cat.shape[1]
    d_gate = n - OFF_GATE

    def row(i):
        return (i, 0)

    def cs_map(i):
        return (jnp.where(i < n_lat_tiles, i % tiles_per_seq, tiles_per_seq), 0)

    out_shape = [jax.ShapeDtypeStruct((t, 2 * A_WIDTH), BF16),
                 jax.ShapeDtypeStruct((A_WIDTH, t), BF16),
                 jax.ShapeDtypeStruct((t, G_QKV_WIDTH), F32),
                 jax.ShapeDtypeStruct((t, G_V_WIDTH), F32),
                 jax.ShapeDtypeStruct((t, LANES), F32),
                 jax.ShapeDtypeStruct((t, d_gate), BF16)]
    out_specs = [pl.BlockSpec((TM, s.shape[1]), row) for s in out_shape]
    out_specs[1] = pl.BlockSpec((A_WIDTH, TM), lambda i: (0, i))
    return pl.pallas_call(
        _inproj_kernel,
        grid=(t // TM,),
        in_specs=[pl.BlockSpec((TM, d), row),
                  pl.BlockSpec((None, 6, d), lambda i: (i, 0, 0)),
                  pl.BlockSpec((d, n), lambda i: (0, 0)),
                  pl.BlockSpec((TM, 2 * LANES), cs_map)],
        out_specs=out_specs,
        out_shape=out_shape,
        compiler_params=_params("parallel"),
    )(x_all, modt, w_cat, cs_tab)


def _row_fold(x, op):
    rows, n = x.shape
    return op(x.reshape(rows // SUBLANES, SUBLANES, n), axis=0)


def _attn_scores(q, kv_refs, s_refs):
    lane = lax.broadcasted_iota(jnp.int32, (1, LANES), 1)
    comp1 = ((lane // (A_HEAD_DIM // 2)) % 2) == 1
    qn = (jnp.where(comp1, jnp.zeros_like(q), q), jnp.where(comp1, q, jnp.zeros_like(q)))
    m_part, off = [None, None], 0
    for k_ref, _ in kv_refs:
        nk = k_ref.shape[0]
        for n in range(2):
            st = _dot_nt(k_ref[...], qn[n])
            s_refs[n][pl.ds(off, nk), :] = st
            mc = _row_fold(st, jnp.max)
            m_part[n] = mc if m_part[n] is None else jnp.maximum(m_part[n], mc)
        off += nk
    return [jnp.max(mp, axis=0, keepdims=True) for mp in m_part]


def _attn_values(lam_ref, kv_refs, g_ref, s_refs, m, lam_init):
    lv = lam_ref[...]
    lam = (jnp.exp(jnp.sum(lv[0:1] * lv[1:2], axis=-1, keepdims=True))
           - jnp.exp(jnp.sum(lv[2:3] * lv[3:4], axis=-1, keepdims=True)) + lam_init)
    l_part, ot, off = [None, None], [None, None], 0
    for k_ref, vt_ref in kv_refs:
        for r0 in range(0, k_ref.shape[0], ATTN_KEY_CHUNK):
            for n in range(2):
                p = jnp.exp(s_refs[n][pl.ds(off + r0, ATTN_KEY_CHUNK), :] - m[n])
                lc = _row_fold(p, jnp.sum)
                l_part[n] = lc if l_part[n] is None else l_part[n] + lc
                oc = _dot(vt_ref[:, pl.ds(r0, ATTN_KEY_CHUNK)], p.astype(BF16))
                ot[n] = oc if ot[n] is None else ot[n] + oc
        off += k_ref.shape[0]
    outs = [ot[n] / jnp.sum(l_part[n], axis=0, keepdims=True) for n in range(2)]
    ot = outs[0] - lam * outs[1]
    r = ot * lax.rsqrt(jnp.mean(ot * ot, axis=0, keepdims=True) + EPS)
    return (r * g_ref[...] * (1.0 - lam_init)).T.astype(BF16)


def _attn_kernel(lam_ref, q_ref, *refs, lam_init, n_kv, n_sub):
    kv_refs = tuple((refs[2 * j], refs[2 * j + 1]) for j in range(n_kv))
    g_ref, o_ref = refs[2 * n_kv], refs[2 * n_kv + 1]
    s_all = refs[2 * n_kv + 2:]
    tq = q_ref.shape[0] // n_sub
    def scores(j):
        s_refs = (s_all[2 * j], s_all[2 * j + 1])
        return s_refs, _attn_scores(q_ref[pl.ds(j * tq, tq), :], kv_refs, s_refs)

    nxt = scores(0)
    for j in range(n_sub):
        s_refs, m = nxt
        if j + 1 < n_sub:
            nxt = scores(j + 1)
        o_ref[pl.ds(j * tq, tq), :] = _attn_values(lam_ref, kv_refs, g_ref, s_refs, m, lam_init)


def _attention(qk, vt, lamv, subln_g, *, lam_init, b, s, nc, latent):
    n_sub = ATTN_Q_SUBTILES if latent else 1
    rows = TM * n_sub
    ctx0 = (b * s) // nc
    kv_specs = [pl.BlockSpec((nc, LANES), lambda bi, h, i: (ctx0 + bi, A_HEADS + h)),
                pl.BlockSpec((LANES, nc), lambda bi, h, i: (h, ctx0 + bi))]
    kv_args = [qk, vt]
    if latent:
        n_q = s // rows
        q_map = lambda bi, h, i: (bi * n_q + i, h)
        kv_specs += [pl.BlockSpec((s, LANES), lambda bi, h, i: (bi, A_HEADS + h)),
                     pl.BlockSpec((LANES, s), lambda bi, h, i: (h, bi))]
        kv_args += [qk, vt]
        n_keys, t_out = nc + s, b * s
    else:
        n_q = nc // rows
        q_map = lambda bi, h, i: ((b * s) // rows + bi * n_q + i, h)
        n_keys, t_out = nc, b * nc
    return pl.pallas_call(
        functools.partial(_attn_kernel, lam_init=lam_init, n_kv=len(kv_args) // 2, n_sub=n_sub),
        grid=(b, A_HEADS, n_q),
        in_specs=[pl.BlockSpec((SUBLANES, LANES), lambda bi, h, i: (0, 0)),
                  pl.BlockSpec((rows, LANES), q_map)] + kv_specs
                 + [pl.BlockSpec((LANES, 1), lambda bi, h, i: (0, 0))],
        out_specs=pl.BlockSpec((rows, LANES), lambda bi, h, i: (bi * n_q + i, h)),
        out_shape=jax.ShapeDtypeStruct((t_out, A_WIDTH), BF16),
        scratch_shapes=[pltpu.VMEM((n_keys, TM), F32)] * (2 * n_sub),
        compiler_params=_params("parallel", "parallel", "arbitrary"),
    )(lamv, qk, *kv_args, subln_g)


def _gdn_prep_kernel(g_ref, prev_ref, next_ref, ab_ref, cw_ref, gp_ref, q_ref, k_ref, v_ref, gc_ref, ext_ref,
                     *, n_lat_tiles, tiles_per_seq):
    i = pl.program_id(0)
    is_lat = i < n_lat_tiles
    first = jnp.logical_or(jnp.logical_not(is_lat), i % tiles_per_seq == 0)
    last = jnp.logical_or(jnp.logical_not(is_lat), i % tiles_per_seq == tiles_per_seq - 1)
    ext_ref[0:SUBLANES, :] = jnp.where(first, 0.0, prev_ref[...])
    ext_ref[SUBLANES:SUBLANES + TM, :] = g_ref[...]
    ext_ref[SUBLANES + TM:, :] = jnp.where(last, 0.0, next_ref[...])
    cw = cw_ref[...]
    acc = None
    for j in range(CONV_K):
        term = ext_ref[pl.ds(SUBLANES - CONV_K // 2 + j, TM), :] * cw[j:j + 1]
        acc = term if acc is None else acc + term
    y = _silu(acc)
    for h in range(G_HEADS):
        for base, ref, scale in ((0, q_ref, G_KEY_DIM ** -0.5), (G_QK_WIDTH, k_ref, 1.0)):
            xh = y[:, base + h * LANES:base + (h + 1) * LANES]
            nrm = xh * lax.rsqrt(jnp.sum(xh * xh, axis=-1, keepdims=True) + EPS)
            ref[:, h * LANES:(h + 1) * LANES] = nrm * scale
    v_ref[...] = y[:, 2 * G_QK_WIDTH:]

    ab = ab_ref[...]
    gp = gp_ref[...]
    lane = lax.broadcasted_iota(jnp.int32, (1, LANES), 1)
    z = ab + gp[1:2]
    softplus = jnp.maximum(z, 0.0) + jnp.log(1.0 + jnp.exp(-jnp.abs(z)))
    g = jnp.where(lane < 2 * G_HEADS, -jnp.exp(gp[0:1]) * softplus, 0.0)
    beta = _sigmoid(ab)
    ri = lax.broadcasted_iota(jnp.int32, (TM, TM), 0)
    ci = lax.broadcasted_iota(jnp.int32, (TM, TM), 1)
    same = (ri // CHUNK) == (ci // CHUNK)
    lower = jnp.where(jnp.logical_and(same, ci <= ri), 1.0, 0.0)
    upper = jnp.where(jnp.logical_and(same, ci >= ri), 1.0, 0.0)
    cum_f = _dot_f32(lower, g)
    cum_b = _dot_f32(upper, g)
    total = cum_f + cum_b - g
    cum = jnp.where(lane < G_HEADS, cum_f, cum_b)
    eg = jnp.exp(cum)
    ek = jnp.exp(total - cum)
    et = jnp.exp(total)
    out = jnp.where(lane < GC_BETA, cum, 0.0)
    out = jnp.where(jnp.logical_and(lane >= GC_BETA, lane < GC_EG), beta, out)
    out = jnp.where(jnp.logical_and(lane >= GC_EG, lane < GC_EK), pltpu.roll(eg, GC_EG, 1), out)
    out = jnp.where(jnp.logical_and(lane >= GC_EK, lane < GC_ET), pltpu.roll(ek, GC_EK, 1), out)
    out = jnp.where(jnp.logical_and(lane >= GC_ET, lane < GC_ET + 8), pltpu.roll(et, GC_ET, 1), out)
    gc_ref[...] = out


def _gdn_prep(g, ab, conv_w8, gparams, n_lat_tiles, tiles_per_seq):
    t, c = g.shape
    nblk8 = t // SUBLANES
    per = TM // SUBLANES
    row = lambda i: (i, 0)
    out_shape = [jax.ShapeDtypeStruct((t, G_QK_WIDTH), F32),
                 jax.ShapeDtypeStruct((t, G_QK_WIDTH), F32),
                 jax.ShapeDtypeStruct((t, G_V_WIDTH), F32),
                 jax.ShapeDtypeStruct((t, LANES), F32)]
    return pl.pallas_call(
        functools.partial(_gdn_prep_kernel, n_lat_tiles=n_lat_tiles, tiles_per_seq=tiles_per_seq),
        grid=(t // TM,),
        in_specs=[pl.BlockSpec((TM, c), row),
                  pl.BlockSpec((SUBLANES, c), lambda i: (jnp.maximum(i * per - 1, 0), 0)),
                  pl.BlockSpec((SUBLANES, c), lambda i: (jnp.minimum((i + 1) * per, nblk8 - 1), 0)),
                  pl.BlockSpec((TM, LANES), row),
                  pl.BlockSpec((SUBLANES, c), lambda i: (0, 0)),
                  pl.BlockSpec((SUBLANES, LANES), lambda i: (0, 0))],
        out_specs=[pl.BlockSpec((TM, s.shape[1]), row) for s in out_shape],
        out_shape=out_shape,
        scratch_shapes=[pltpu.VMEM((TM + 2 * SUBLANES, c), F32)],
        compiler_params=_params("parallel"),
    )(g, g, g, ab, conv_w8, gparams)


def _gdn_masks(c, reverse):
    ii = lax.broadcasted_iota(jnp.int32, (c, 2 * c), 0)
    lane = lax.broadcasted_iota(jnp.int32, (c, 2 * c), 1)
    left = lane < c
    jj = jnp.where(left, lane, lane - c)
    causal = (ii <= jj) if reverse else (ii >= jj)
    strict = (ii < jj) if reverse else (ii > jj)
    return dict(left=left, diag=ii == jj, causal=causal, strict=strict)


def _gdn_setup(q, k, v, gc, gct, col, mk):
    c = q.shape[0]
    gcol = gc[:, GC_CUM + col:GC_CUM + col + 1]
    beta = gc[:, GC_BETA + col:GC_BETA + col + 1]
    eg = gc[:, GC_EG + col:GC_EG + col + 1]
    ek = gc[:, GC_EK + col:GC_EK + col + 1]
    et = gc[0:1, GC_ET + col:GC_ET + col + 1]
    grow = gct[GC_CUM + col:GC_CUM + col + 1, :]
    decay = jnp.exp(jnp.where(mk["causal"], gcol - grow, -1e30))
    kb = k * beta
    k16 = k.astype(BF16)
    both = _dot_nt(jnp.concatenate([kb.astype(BF16), q.astype(BF16)], axis=0),
                   jnp.concatenate([k16, k16], axis=0))
    qk = jnp.where(jnp.logical_and(mk["causal"], mk["left"]), both[c:] * decay, 0.0)
    z = jnp.where(mk["left"], jnp.where(mk["diag"], 1.0, 0.0), jnp.where(mk["strict"], -(both[:c] * decay), 0.0))
    rhs = jnp.concatenate([v * beta, kb * eg], axis=1)
    return dict(z=z, qk=qk.astype(BF16), rhs=rhs, qd=(q * eg).astype(BF16), kdt=(k * ek).T.astype(BF16), et=et)


def _gdn_level(z, mk):
    z16 = z.astype(BF16)
    return jnp.where(mk["left"], z, 0.0) + _dot(z16, jnp.concatenate([jnp.zeros_like(z16), z16], axis=0))


def _gdn_solve(ch, mk):
    y16 = jnp.where(jnp.logical_and(mk["left"], jnp.logical_not(mk["diag"])), ch["z"], 0.0).astype(BF16)
    rhs16 = ch["rhs"].astype(BF16)
    sol = ch["rhs"] + _dot(y16, jnp.concatenate([rhs16, rhs16], axis=0))
    dv = sol.shape[1] // 2
    return sol[:, :dv], jnp.concatenate([sol[:, dv:].astype(BF16), ch["qd"]], axis=0)


def _gdn_kernel(qf_ref, kf_ref, vf_ref, gf_ref, qb_ref, kb_ref, vb_ref, gb_ref, of_ref, ob_ref, s_ref):
    @pl.when(pl.program_id(1) == 0)
    def _():
        s_ref[...] = jnp.zeros_like(s_ref)

    n_chunks = qf_ref.shape[0] // CHUNK
    masks = (_gdn_masks(CHUNK, False), _gdn_masks(CHUNK, True))
    dirs = ((qf_ref, kf_ref, vf_ref, gf_ref, of_ref), (qb_ref, kb_ref, vb_ref, gb_ref, ob_ref))
    state = [s_ref[i] for i in range(2 * G_HEADS)]
    for n0 in range(0, n_chunks, GDN_CHUNKS_PER_BATCH):
        chains = []
        for n in range(n0, n0 + GDN_CHUNKS_PER_BATCH):
            for d, (q_ref, k_ref, v_ref, g_ref, o_ref) in enumerate(dirs):
                rows = pl.ds((n_chunks - 1 - n if d else n) * CHUNK, CHUNK)
                gc = g_ref[rows, :]
                gct = jnp.concatenate([gc, gc], axis=0).T
                for h in range(G_HEADS):
                    cols = pl.ds(h * LANES, LANES)
                    ch = _gdn_setup(q_ref[rows, cols], k_ref[rows, cols], v_ref[rows, cols], gc, gct,
                                    d * G_HEADS + h, masks[d])
                    ch.update(d=d, h=h, rows=rows, cols=cols, out=o_ref)
                    chains.append(ch)
        for _ in range(int(math.log2(CHUNK))):
            for ch in chains:
                ch["z"] = _gdn_level(ch["z"], masks[ch["d"]])
        for ch in chains:
            ch["u"], ch["wq"] = _gdn_solve(ch, masks[ch["d"]])
        per_chunk = 2 * G_HEADS
        for c0 in range(0, len(chains), per_chunk):
            group = chains[c0:c0 + per_chunk]
            ws = [_dot(ch["wq"], state[ch["d"] * G_HEADS + ch["h"]].astype(BF16)) for ch in group]
            for ch, w in zip(group, ws):
                i = ch["d"] * G_HEADS + ch["h"]
                v_new = (ch["u"] - w[:CHUNK]).astype(BF16)
                o = w[CHUNK:] + _dot(ch["qk"], jnp.concatenate([v_new, v_new], axis=0))
                state[i] = state[i] * ch["et"] + _dot(ch["kdt"], v_new)
                ch["out"][ch["rows"], ch["cols"]] = o
    for i in range(2 * G_HEADS):
        s_ref[i] = state[i]


def _gdn(qg, kg, vg, gc, *, b, s, nc):
    t = qg.shape[0]
    tps = s // TM
    nct = nc // TM
    n_steps = nct + tps
    ctx_tile0 = (b * s) // TM

    def fwd(bi, i):
        return (jnp.where(i < nct, ctx_tile0 + bi * nct + i, bi * tps + (i - nct)), 0)

    def bwd(bi, i):
        return (jnp.where(i < nct, ctx_tile0 + bi * nct + (nct - 1 - i), bi * tps + (tps - 1 - (i - nct))), 0)

    def specs(row_map):
        return [pl.BlockSpec((TM, G_QK_WIDTH), row_map), pl.BlockSpec((TM, G_QK_WIDTH), row_map),
                pl.BlockSpec((TM, G_V_WIDTH), row_map), pl.BlockSpec((TM, LANES), row_map)]

    return pl.pallas_call(
        _gdn_kernel,
        grid=(b, n_steps),
        in_specs=specs(fwd) + specs(bwd),
        out_specs=[pl.BlockSpec((TM, G_V_WIDTH), fwd), pl.BlockSpec((TM, G_V_WIDTH), bwd)],
        out_shape=[jax.ShapeDtypeStruct((t, G_V_WIDTH), F32)] * 2,
        scratch_shapes=[pltpu.VMEM((2 * G_HEADS, G_KEY_DIM, G_VAL_DIM), F32)],
        compiler_params=_params("parallel", "arbitrary"),
    )(qg, kg, vg, gc, qg, kg, vg, gc)


def _layer_norm(r, g, b):
    mu = jnp.mean(r, axis=-1, keepdims=True)
    rc = r - mu
    var = jnp.mean(rc * rc, axis=-1, keepdims=True)
    return rc * lax.rsqrt(var + EPS) * g + b


def _merge_kernel(x_ref, yal_ref, yac_ref, of_ref, ob_ref, z_ref, gate_ref, mod_ref, wpa_ref, wpb_ref, wo_ref, on_ref,
                  lng_ref, lnb_ref, rw_ref, rb_ref, xo_ref, ht_ref, cls_ref, ew_ref, *, alpha, n_lat_tiles):
    d = x_ref.shape[1]
    o = of_ref[...] + ob_ref[...]
    parts = []
    for h in range(G_HEADS):
        oh = o[:, h * LANES:(h + 1) * LANES]
        parts.append(oh * lax.rsqrt(jnp.mean(oh * oh, axis=-1, keepdims=True) + EPS) * on_ref[...])
    yb = jnp.concatenate(parts, axis=1) * _silu(z_ref[...])
    ya = jnp.where(pl.program_id(0) < n_lat_tiles, yal_ref[...], yac_ref[...])
    pa = _dot(ya, wpa_ref[...])
    pb = _dot(yb.astype(BF16), wpb_ref[...])
    mix = gate_ref[:, :d] * pa + gate_ref[:, d:] * pb
    y = _dot(mix.astype(BF16), wo_ref[...])
    m = mod_ref[...]
    xn = _layer_norm(alpha * x_ref[...] + m[2:3] * y, lng_ref[...], lnb_ref[...])
    xo_ref[...] = xn
    h2 = xn * (1.0 + m[4:5]) + m[3:4]
    _to_token_tiles(ht_ref, h2)

    score = _sigmoid(_dot_nt(rw_ref[...], h2.astype(BF16)))
    sel = score + rb_ref[...]
    rows = [sel[e:e + 1] for e in range(N_EXPERTS)]
    best = None
    best_val = None
    for gi in range(N_GROUPS):
        a, b, c, dd = rows[gi * EXPERTS_PER_GROUP:(gi + 1) * EXPERTS_PER_GROUP]
        top2 = jnp.maximum(jnp.maximum(jnp.maximum(a + b, a + c), jnp.maximum(a + dd, b + c)),
                           jnp.maximum(b + dd, c + dd))
        if gi == 0:
            best, best_val = jnp.zeros(top2.shape, jnp.int32), top2
        else:
            upd = top2 > best_val
            best = jnp.where(upd, gi, best)
            best_val = jnp.where(upd, top2, best_val)
    in_group, in_score = [], []
    for j in range(EXPERTS_PER_GROUP):
        v = rows[(N_GROUPS - 1) * EXPERTS_PER_GROUP + j]
        sv = score[(N_GROUPS - 1) * EXPERTS_PER_GROUP + j:(N_GROUPS - 1) * EXPERTS_PER_GROUP + j + 1]
        for gi in range(N_GROUPS - 2, -1, -1):
            e = gi * EXPERTS_PER_GROUP + j
            v = jnp.where(best == gi, rows[e], v)
            sv = jnp.where(best == gi, score[e:e + 1], sv)
        in_group.append(v)
        in_score.append(sv)
    l0 = jnp.zeros(best.shape, jnp.int32)
    m0 = in_group[0]
    for j in range(1, EXPERTS_PER_GROUP):
        upd = in_group[j] > m0
        l0 = jnp.where(upd, j, l0)
        m0 = jnp.where(upd, in_group[j], m0)
    l1 = jnp.zeros(best.shape, jnp.int32)
    m1 = jnp.full(m0.shape, -jnp.inf, F32)
    for j in range(EXPERTS_PER_GROUP):
        upd = jnp.logical_and(l0 != j, in_group[j] > m1)
        l1 = jnp.where(upd, j, l1)
        m1 = jnp.where(upd, in_group[j], m1)
    lo = jnp.minimum(l0, l1)
    hi = jnp.maximum(l0, l1)
    w_lo = jnp.zeros(m0.shape, F32)
    w_hi = jnp.zeros(m0.shape, F32)
    for j in range(EXPERTS_PER_GROUP):
        w_lo = jnp.where(lo == j, in_score[j], w_lo)
        w_hi = jnp.where(hi == j, in_score[j], w_hi)
    pair = jnp.where(lo == 0, hi - 1, jnp.where(lo == 1, hi + 1, len(PAIRS) - 1))
    tot = w_lo + w_hi
    ri = lax.broadcasted_iota(jnp.int32, cls_ref.shape, 0)
    cls_ref[...] = jnp.where(ri == 0, best * len(PAIRS) + pair, 0)
    ew_ref[...] = jnp.where(ri == 0, w_lo / tot, jnp.where(ri == 1, w_hi / tot, 0.0))


def _merge(x_all, ya_lat, ya_ctx, o_f, o_b, z, gate, modt, wpa, wpb, wo, onorm, lng, lnb, rw_t, rb, *, alpha, t):
    d = x_all.shape[1]
    r = d // LANES
    n_lat_tiles = ya_lat.shape[0] // TM
    row = lambda i: (i, 0)
    col = lambda i: (0, i)
    const = lambda i: (0, 0)
    out_shape = [jax.ShapeDtypeStruct((t, d), F32),
                 jax.ShapeDtypeStruct((t * r, LANES), F32),
                 jax.ShapeDtypeStruct((SUBLANES, t), jnp.int32),
                 jax.ShapeDtypeStruct((SUBLANES, t), F32)]
    return pl.pallas_call(
        functools.partial(_merge_kernel, alpha=alpha, n_lat_tiles=n_lat_tiles),
        grid=(t // TM,),
        in_specs=[pl.BlockSpec((TM, d), row),
                  pl.BlockSpec((TM, A_WIDTH), lambda i: (jnp.minimum(i, n_lat_tiles - 1), 0)),
                  pl.BlockSpec((TM, A_WIDTH), lambda i: (jnp.maximum(i - n_lat_tiles, 0), 0)),
                  pl.BlockSpec((TM, G_V_WIDTH), row),
                  pl.BlockSpec((TM, G_V_WIDTH), row),
                  pl.BlockSpec((TM, G_V_WIDTH), row),
                  pl.BlockSpec((TM, 2 * d), row),
                  pl.BlockSpec((None, 6, d), lambda i: (i, 0, 0)),
                  pl.BlockSpec(wpa.shape, const),
                  pl.BlockSpec(wpb.shape, const),
                  pl.BlockSpec(wo.shape, const),
                  pl.BlockSpec((1, LANES), const),
                  pl.BlockSpec((1, d), const),
                  pl.BlockSpec((1, d), const),
                  pl.BlockSpec(rw_t.shape, const),
                  pl.BlockSpec(rb.shape, const)],
        out_specs=[pl.BlockSpec((TM, d), row), pl.BlockSpec((TM * r, LANES), row),
                   pl.BlockSpec((SUBLANES, TM), col), pl.BlockSpec((SUBLANES, TM), col)],
        out_shape=out_shape,
        compiler_params=_params("parallel"),
    )(x_all, ya_lat, ya_ctx, o_f, o_b, z, gate, modt, wpa, wpb, wo, onorm, lng, lnb, rw_t, rb)


def _expert_kernel(ea_ref, eb_ref, nu_ref, nv_ref, src_ref, h_hbm, ws_ref, w1a_ref, w3a_ref, w2a_ref,
                   w1b_ref, w3b_ref, w2b_ref, y_hbm, xbuf, ybuf, gsem, ssem, *, r):
    i = pl.program_id(0)
    n_used = nu_ref[0]
    slot = i % 2

    def gather_copy(blk, buf, j):
        tok = src_ref[blk * MOE_BLOCK + j]
        return pltpu.make_async_copy(h_hbm.at[pl.ds(pl.multiple_of(tok * r, r), r), :],
                                     xbuf.at[buf, pl.ds(pl.multiple_of(j * r, r), r), :], gsem.at[buf])

    def scatter_copy(blk, buf, j):
        tok = src_ref[blk * MOE_BLOCK + j]
        return pltpu.make_async_copy(ybuf.at[buf, pl.ds(pl.multiple_of(j * r, r), r), :],
                                     y_hbm.at[pl.ds(pl.multiple_of(tok * r, r), r), :], ssem.at[buf])

    def for_slots(fn):
        def body(g, carry):
            for k in range(DMA_GROUP):
                fn(g * DMA_GROUP + k, k)
            return carry
        lax.fori_loop(0, MOE_BLOCK // DMA_GROUP, body, 0)

    def for_token_slots(blk, fn):
        nv = nv_ref[blk]

        def body(g, carry):
            j0 = g * DMA_GROUP

            @pl.when(j0 + DMA_GROUP <= nv)
            def _():
                for k in range(DMA_GROUP):
                    fn(j0 + k, k)

            @pl.when(jnp.logical_and(j0 < nv, j0 + DMA_GROUP > nv))
            def _():
                for k in range(DMA_GROUP):
                    pl.when(j0 + k < nv)(functools.partial(fn, j0 + k, k))
            return carry
        lax.fori_loop(0, MOE_BLOCK // DMA_GROUP, body, 0)

    @pl.when(i == 0)
    def _():
        for_slots(lambda j, k: gather_copy(0, 0, j).start(priority=k % 2))

    @pl.when(i < n_used)
    def _():
        for_slots(lambda j, k: gather_copy(i, slot, j).wait())

        @pl.when(i + 1 < n_used)
        def _():
            for_slots(lambda j, k: gather_copy(i + 1, 1 - slot, j).start(priority=k % 2))

        x = _from_token_tiles(xbuf.at[slot], MOE_BLOCK, r).astype(BF16)
        ws = ws_ref[...]

        ga, ua = _dot(x, w1a_ref[...]), _dot(x, w3a_ref[...])
        gb, ub = _dot(x, w1b_ref[...]), _dot(x, w3b_ref[...])
        ya = _dot((_silu(ga) * ua).astype(BF16), w2a_ref[...])
        yb = _dot((_silu(gb) * ub).astype(BF16), w2b_ref[...])
        y = ws[:, 0:1] * ya + ws[:, 1:2] * yb

        _to_token_tiles(ybuf.at[slot], y)

        @pl.when(i >= 1)
        def _():
            for_token_slots(i - 1, lambda j, k: scatter_copy(i - 1, 1 - slot, j).wait())

        for_token_slots(i, lambda j, k: scatter_copy(i, slot, j).start(priority=k % 2))

        @pl.when(i == n_used - 1)
        def _():
            for_token_slots(i, lambda j, k: scatter_copy(i, slot, j).wait())


def _experts(ht, w_slot, block_ea, block_eb, n_used, n_valid, slot_src, w1, w3, w2, n_out_tokens):
    d, f = w1.shape[1], w1.shape[2]
    r = d // LANES
    n_slots = slot_src.shape[0]

    def wspec(shape, which):
        if which == 0:
            return pl.BlockSpec((None,) + shape, lambda i, ea, eb, nu, nv, src: (ea[i], 0, 0))
        return pl.BlockSpec((None,) + shape, lambda i, ea, eb, nu, nv, src: (eb[i], 0, 0))

    grid_spec = pltpu.PrefetchScalarGridSpec(
        num_scalar_prefetch=5,
        grid=(n_slots // MOE_BLOCK,),
        in_specs=[pl.BlockSpec(memory_space=pl.ANY),
                  pl.BlockSpec((MOE_BLOCK, 2), lambda i, ea, eb, nu, nv, src: (i, 0)),
                  wspec((d, f), 0), wspec((d, f), 0), wspec((f, d), 0),
                  wspec((d, f), 1), wspec((d, f), 1), wspec((f, d), 1)],
        out_specs=pl.BlockSpec(memory_space=pl.ANY),
        scratch_shapes=[pltpu.VMEM((2, MOE_BLOCK * r, LANES), F32),
                        pltpu.VMEM((2, MOE_BLOCK * r, LANES), F32),
                        pltpu.SemaphoreType.DMA((2,)),
                        pltpu.SemaphoreType.DMA((2,))],
    )
    return pl.pallas_call(
        functools.partial(_expert_kernel, r=r),
        grid_spec=grid_spec,
        out_shape=jax.ShapeDtypeStruct((n_out_tokens * r, LANES), F32),
        compiler_params=_params("arbitrary"),
    )(block_ea, block_eb, n_used, n_valid, slot_src, ht, w_slot, w1, w3, w2, w1, w3, w2)


def _ln2_kernel(x_ref, yt_ref, mod_ref, g_ref, b_ref, o_ref, *, alpha):
    tm, d = x_ref.shape
    y = _from_token_tiles(yt_ref, tm, d // LANES)
    m = mod_ref[...]
    o_ref[...] = _layer_norm(alpha * x_ref[...] + m[5:6] * y, g_ref[...], b_ref[...])


def _ln2(x_all, yt, modt, g, b, *, alpha):
    t, d = x_all.shape
    r = d // LANES
    row = lambda i: (i, 0)
    const = lambda i: (0, 0)
    return pl.pallas_call(
        functools.partial(_ln2_kernel, alpha=alpha),
        grid=(t // TM,),
        in_specs=[pl.BlockSpec((TM, d), row), pl.BlockSpec((TM * r, LANES), row),
                  pl.BlockSpec((None, 6, d), lambda i: (i, 0, 0)),
                  pl.BlockSpec((1, d), const), pl.BlockSpec((1, d), const)],
        out_specs=pl.BlockSpec((TM, d), row),
        out_shape=jax.ShapeDtypeStruct((t, d), F32),
        compiler_params=_params("parallel"),
    )(x_all, yt, modt, g, b)


def _rope_table(s, nc):
    n_freq = A_HEAD_DIM // 4
    rows = s // GRID_W
    row = jnp.repeat(jnp.arange(rows, dtype=F32), GRID_W)
    col = jnp.tile(jnp.arange(GRID_W, dtype=F32), rows)
    inv = ROPE_BASE ** (-jnp.arange(n_freq, dtype=F32) / n_freq)
    ang = jnp.stack([row[:, None] * inv, col[:, None] * inv], axis=1)
    cos = jnp.cos(ang).reshape(s, 1, 1, 2, n_freq)
    sin = jnp.sin(ang).reshape(s, 1, 1, 2, n_freq)
    cos = jnp.broadcast_to(cos, (s, 2, 2, 2, n_freq)).reshape(s, LANES)
    sign = jnp.array([-1.0, 1.0], F32).reshape(1, 2, 1, 1, 1)
    sin = jnp.broadcast_to(sin * sign, (s, 2, 2, 2, n_freq)).reshape(s, LANES)
    lat = jnp.concatenate([cos, sin], axis=1)
    ident = jnp.concatenate([jnp.ones((nc, LANES), F32), jnp.zeros((nc, LANES), F32)], axis=1)
    return jnp.concatenate([lat, ident], axis=0)


def _rope_perm(w):
    d = w.shape[0]
    n_freq = A_HEAD_DIM // 4
    return w.reshape(d, A_HEADS, 2, 2, 2, n_freq).transpose(0, 1, 4, 2, 3, 5).reshape(d, A_WIDTH)


def _pack_w_in(w):
    d = w.shape[0]
    sizes = (A_WIDTH, A_WIDTH, A_WIDTH, G_QKV_WIDTH, G_V_WIDTH, 2 * G_HEADS, 2 * G_HEADS)
    offs = [0]
    for sz in sizes:
        offs.append(offs[-1] + sz)
    wq, wk, wv, wg, wz, wa, wb = (w[:, offs[i]:offs[i + 1]] for i in range(len(sizes)))
    wgate = w[:, offs[-1]:]
    wab = jnp.concatenate([wa, wb, jnp.zeros((d, LANES - 4 * G_HEADS), w.dtype)], axis=1)
    cat = jnp.concatenate([_rope_perm(wq) * (A_HEAD_DIM ** -0.5), _rope_perm(wk), wv, wg, wz, wab, wgate], axis=1)
    return cat.astype(BF16)


def _pad_rows(a, rows):
    return jnp.concatenate([a, jnp.zeros((rows - a.shape[0],) + a.shape[1:], a.dtype)], axis=0)


def _pad_lanes(a):
    return jnp.concatenate([a, jnp.zeros(a.shape[:-1] + (LANES - a.shape[-1],), a.dtype)], axis=-1)


def _dispatch(cls, ew, t):
    onehot = (cls[:, None] == jnp.arange(N_CLASSES, dtype=jnp.int32)[None, :]).astype(jnp.int32)
    csum = jnp.cumsum(onehot, axis=0)
    rank = jnp.sum(csum * onehot, axis=1) - 1
    counts = csum[-1]
    padded = (counts + MOE_BLOCK - 1) // MOE_BLOCK * MOE_BLOCK
    pad_end = jnp.cumsum(padded)
    pad_start = pad_end - padded
    dest = pad_start[cls] + rank
    n_blocks = -(-t // MOE_BLOCK) + N_CLASSES
    per_tok = jnp.concatenate([jnp.arange(t, dtype=F32)[:, None], ew], axis=1)
    per_slot = jnp.zeros((n_blocks * MOE_BLOCK, per_tok.shape[1]), F32).at[dest].set(per_tok)
    slot_src = per_slot[:, 0].astype(jnp.int32)
    w_slot = per_slot[:, 1:]
    block0 = jnp.arange(n_blocks, dtype=jnp.int32) * MOE_BLOCK
    block_cls = jnp.minimum(jnp.searchsorted(pad_end, block0, side='right'), N_CLASSES - 1)
    n_valid = jnp.clip((pad_start + counts)[block_cls] - block0, 0, MOE_BLOCK).astype(jnp.int32)
    n_used = (pad_end[-1:] // MOE_BLOCK).astype(jnp.int32)
    return slot_src, w_slot, jnp.asarray(CLASS_EA)[block_cls], jnp.asarray(CLASS_EB)[block_cls], n_used, n_valid


def kernel(x, c, ctx, c_ctx, mod_w, mod_b, w_in, conv_w, lam_q1, lam_k1, lam_q2, lam_k2, subln_g, a_log, dt_bias,
           onorm_g, w_pa, w_pb, w_o, ln1_g, ln1_b, router_w, router_b, w_exp1, w_exp3, w_exp2, ln2_g, ln2_b):
    b, s, d = x.shape
    nc = ctx.shape[1]
    depth = mod_w.shape[0]
    assert s % (TM * ATTN_Q_SUBTILES) == 0 and nc % TM == 0 and s % nc == 0 and s % GRID_W == 0
    assert d == SUBLANES * LANES
    t_lat, t_ctx = b * s, b * nc
    t = t_lat + t_ctx
    n_lat_tiles = t_lat // TM
    tiles_per_seq = s // TM
    alpha = (2.0 * depth) ** 0.25

    x_all = jnp.concatenate([x.reshape(t_lat, d), ctx.reshape(t_ctx, d)], axis=0)
    cc = _pad_rows(jnp.concatenate([c, c_ctx[None, :]], axis=0), SUBLANES)
    mods = _mod_tables(cc, mod_w, mod_b)
    tile_row = jnp.concatenate([jnp.arange(n_lat_tiles, dtype=jnp.int32) // tiles_per_seq,
                                jnp.full((t_ctx // TM,), b, jnp.int32)])
    cs_tab = _rope_table(s, nc)
    rw_t = router_w.T.astype(BF16)
    rb = router_b.reshape(N_EXPERTS, 1)

    for l in range(depth):
        lam_init = 0.8 - 0.6 * math.exp(-0.3 * l)
        modt = mods[l].reshape(SUBLANES, 6, d)[tile_row]
        qk, vt, g, z, ab, gate = _inproj(x_all, modt, _pack_w_in(w_in[l]), cs_tab, n_lat_tiles, tiles_per_seq)

        lamv = _pad_rows(_pad_lanes(jnp.stack([lam_q1[l], lam_k1[l], lam_q2[l], lam_k2[l]])), SUBLANES)
        sg = subln_g[l].reshape(LANES, 1)
        ya_lat = _attention(qk, vt, lamv, sg, lam_init=lam_init, b=b, s=s, nc=nc, latent=True)
        ya_ctx = _attention(qk, vt, lamv, sg, lam_init=lam_init, b=b, s=s, nc=nc, latent=False)

        gparams = _pad_rows(_pad_lanes(jnp.stack([a_log[l].reshape(-1), dt_bias[l].reshape(-1)])), SUBLANES)
        qg, kg, vg, gc = _gdn_prep(g, ab, _pad_rows(conv_w[l], SUBLANES), gparams, n_lat_tiles, tiles_per_seq)
        o_f, o_b = _gdn(qg, kg, vg, gc, b=b, s=s, nc=nc)

        t_out = t_lat if l == depth - 1 else t
        x_all, ht, cls, ew = _merge(x_all, ya_lat, ya_ctx, o_f, o_b, z, gate, modt, w_pa[l].astype(BF16), w_pb[l].astype(BF16),
                                    w_o[l].astype(BF16), onorm_g[l].reshape(1, LANES), ln1_g[l].reshape(1, d),
                                    ln1_b[l].reshape(1, d), rw_t, rb, alpha=alpha, t=t_out)

        slot_src, w_slot, block_ea, block_eb, n_used, n_valid = _dispatch(cls[0], ew[:2].T, t_out)
        yt = _experts(ht, w_slot, block_ea, block_eb, n_used, n_valid, slot_src, w_exp1[l].astype(BF16),
                      w_exp3[l].astype(BF16), w_exp2[l].astype(BF16), t_out)
        x_all = _ln2(x_all, yt, modt, ln2_g[l].reshape(1, d), ln2_b[l].reshape(1, d), alpha=alpha)
    return x_all.reshape(b, s, d)
```
